```python
import jax, jax.numpy as jnp
from jax import lax
import numpy as np

D_MODEL = 1024
BATCH = 8
SEQ = 2048
DEPTH = 4
DEC_BATCH = 128
DEC_SEQ = 1
PAST_LEN = 16384
PAGE_SIZE = 128

HEAD_DIM = 64
C_RWKV = 3 * D_MODEL // 8
C_HGRN = 3 * D_MODEL // 8
C_LRU = D_MODEL - C_RWKV - C_HGRN
D_MIX = C_RWKV + C_HGRN + C_LRU
RWKV_HEADS = C_RWKV // HEAD_DIM
HGRN_HEADS = C_HGRN // HEAD_DIM
HGRN_KDIM = 64
HGRN_FDIM = HGRN_HEADS * HGRN_KDIM
HGRN_CHUNK = 64
LRU_BLOCKS = 4
LRU_BLOCK = C_LRU // LRU_BLOCKS
CONV_WIDTH = 4
LRU_C = 8.0
W_LORA = 64
A_LORA = 64
G_LORA = 128
RWKV_COLS = 3 * C_RWKV + W_LORA + A_LORA + G_LORA
HGRN_COLS = 2 * HGRN_FDIM + 2 * C_HGRN
LRU_COLS = 2 * C_LRU
C_IN = RWKV_COLS + HGRN_COLS + LRU_COLS
D_FF = 4 * D_MODEL
NORM_EPS = 1e-6
GN_EPS = 64e-5

kernel_name = 'hybrid_rwkv7_hgrn2_rglru_decode_step'

F32 = jnp.float32


def _rmsnorm(x, g):
    xf = x.astype(F32)
    return (xf * lax.rsqrt(jnp.mean(xf * xf, -1, keepdims=True) + NORM_EPS) * g).astype(x.dtype)


def _rwkv7(pm, S0, w0, w_up, a0, a_up, g_up, k_k, k_a, r_k, ln_w, ln_b):
    B, T, _ = pm.shape
    H, N, c = RWKV_HEADS, HEAD_DIM, C_RWKV
    r, k, v, xw, xa, xg = jnp.split(pm, [c, 2 * c, 3 * c, 3 * c + W_LORA, 3 * c + W_LORA + A_LORA], axis=-1)
    w_log = -jax.nn.softplus(-(w0 + jnp.tanh(xw) @ w_up)) - 0.5
    decay = jnp.exp(-jnp.exp(w_log))
    a = jax.nn.sigmoid(a0 + xa @ a_up)
    g = jax.nn.sigmoid(xg) @ g_up
    hd = lambda t: t.reshape(B, T, H, N)
    kk = hd(k * k_k)
    kk = kk * lax.rsqrt(jnp.maximum(jnp.sum(kk * kk, -1, keepdims=True), 1e-24))
    k = k * (1.0 + (a - 1.0) * k_a)
    r_h, k_h, v_h, a_h, w_h = hd(r), hd(k), hd(v), hd(a), hd(decay)

    def step(S, inp):
        rt, wt, kt, vt, kkt, at = inp
        sa = jnp.einsum('bhvk,bhk->bhv', S, -kkt)
        S = S * wt[:, :, None, :] + sa[..., None] * (kkt * at)[:, :, None, :] + vt[..., None] * kt[:, :, None, :]
        return S, jnp.einsum('bhvk,bhk->bhv', S, rt)

    tm = lambda t: jnp.moveaxis(t, 1, 0)
    S_T, out = lax.scan(step, S0, (tm(r_h), tm(w_h), tm(k_h), tm(v_h), tm(kk), tm(a_h)))
    out = jnp.moveaxis(out, 0, 1)
    mean = jnp.mean(out, -1, keepdims=True)
    var = jnp.mean(jnp.square(out - mean), -1, keepdims=True)
    gn = ((out - mean) * lax.rsqrt(var + GN_EPS)).reshape(B, T, c) * ln_w + ln_b
    bonus = (jnp.sum(r_h * k_h * r_k, -1, keepdims=True) * v_h).reshape(B, T, c)
    return (gn + bonus) * g, S_T


def _hgrn2(ph, S0, lb, norm_g):
    B, T, _ = ph.shape
    H, K, V = HGRN_HEADS, HGRN_KDIM, HEAD_DIM
    q, fpre, iv, g = jnp.split(ph, [HGRN_FDIM, 2 * HGRN_FDIM, 2 * HGRN_FDIM + C_HGRN], axis=-1)
    f = lb + (1.0 - lb) * jax.nn.sigmoid(fpre)
    logf = jnp.log(f)
    kin = (1.0 - lb) * jax.nn.sigmoid(-fpre)
    C = HGRN_CHUNK if T % HGRN_CHUNK == 0 else T
    nC = T // C
    ch = lambda t, d: jnp.moveaxis(t.reshape(B, nC, C, H, d), 1, 0)
    causal = jnp.tril(jnp.ones((C, C), bool))[None, :, :, None, None]

    def chunk_step(S, inp):
        qc, kc, vc, lc = inp
        b = jnp.cumsum(lc, axis=1)
        diff = jnp.where(causal, b[:, :, None] - b[:, None, :], 0.0)
        dmat = jnp.where(causal, jnp.exp(diff), 0.0)
        att = jnp.einsum('bthk,bshk,btshk->bhts', qc, kc, dmat)
        o = jnp.einsum('bhts,bshv->bthv', att, vc) + jnp.einsum('bthk,bhkv->bthv', qc * jnp.exp(b), S)
        b_last = b[:, -1]
        S = jnp.exp(b_last)[..., None] * S + jnp.einsum('bshk,bshv->bhkv', kc * jnp.exp(b_last[:, None] - b), vc)
        return S, o

    S_T, o = lax.scan(chunk_step, S0, (ch(q, K), ch(kin, K), ch(iv, V), ch(logf, K)))
    o = jnp.moveaxis(o, 0, 1).reshape(B, T, H, V)
    o = o * lax.rsqrt(jnp.mean(o * o, -1, keepdims=True) + NORM_EPS)
    return o.reshape(B, T, C_HGRN) * norm_g * jax.nn.silu(g), S_T


def _combine(c1, c2):
    a1, b1 = c1
    a2, b2 = c2
    return a1 * a2, a2 * b1 + b2


def _rglru(pl, h0, conv0, conv_w, conv_b, wa, ba, wx, bx, lam):
    B, T, _ = pl.shape
    xb, gate = jnp.split(pl, [C_LRU], axis=-1)
    xpad = jnp.concatenate([conv0, xb], axis=1)
    xc = conv_b + sum(xpad[:, j:j + T] * conv_w[j] for j in range(CONV_WIDTH))
    new_conv = xpad[:, T:]
    xh = xc.reshape(B, T, LRU_BLOCKS, LRU_BLOCK)
    r = jax.nn.sigmoid(jnp.einsum('btnc,ncd->btnd', xh, wa).reshape(B, T, C_LRU) + ba)
    i = jax.nn.sigmoid(jnp.einsum('btnc,ncd->btnd', xh, wx).reshape(B, T, C_LRU) + bx)
    log_a = -LRU_C * r * jax.nn.softplus(-lam)
    a = jnp.exp(log_a)
    u = xc * i * jnp.sqrt(jnp.maximum(-jnp.expm1(2.0 * log_a), 1e-12))
    u = u.at[:, 0].add(a[:, 0] * h0)
    _, h = lax.associative_scan(_combine, (a, u), axis=1)
    return h * jax.nn.gelu(gate), h[:, -1], new_conv


def _trunk(x, wkv, shift, hgrn, lru, conv, prm):
    dt = x.dtype
    s = jax.nn.softmax(prm['hgrn_lb'].astype(F32), axis=0)
    lb_all = jnp.cumsum(s, axis=0) - s[0:1]
    n_wkv, n_shift, n_hgrn, n_lru, n_conv = [], [], [], [], []
    for l in range(DEPTH):
        xn = _rmsnorm(x, prm['norm1_g'][l])
        w_in = prm['w_in'][l]
        P = (xn @ w_in).astype(F32)
        pr, ph, pl = jnp.split(P, [RWKV_COLS, RWKV_COLS + HGRN_COLS], axis=-1)
        p0 = (shift[l].astype(dt) @ w_in[:, :RWKV_COLS]).astype(F32)
        pr_prev = jnp.concatenate([p0[:, None], pr[:, :-1]], axis=1)
        pm = pr + prm['mu_shift'][l] * (pr_prev - pr)
        o_r, S_r = _rwkv7(pm, wkv[l].astype(F32), prm['rwkv_w0'][l], prm['rwkv_w_up'][l], prm['rwkv_a0'][l],
                          prm['rwkv_a_up'][l], prm['rwkv_g_up'][l], prm['rwkv_k_k'][l], prm['rwkv_k_a'][l],
                          prm['rwkv_r_k'][l], prm['rwkv_ln_w'][l], prm['rwkv_ln_b'][l])
        o_h, S_h = _hgrn2(ph, hgrn[l].astype(F32), lb_all[l], prm['hgrn_norm_g'][l])
        o_l, h_l, c_l = _rglru(pl, lru[l].astype(F32), conv[l].astype(F32), prm['lru_conv_w'][l],
                               prm['lru_conv_b'][l], prm['lru_wa'][l], prm['lru_ba'][l], prm['lru_wx'][l],
                               prm['lru_bx'][l], prm['lru_lambda'][l])
        mix = jnp.concatenate([o_r, o_h, o_l], axis=-1).astype(dt)
        x = x + mix @ prm['w_out'][l]
        xn2 = _rmsnorm(x, prm['norm2_g'][l])
        x = x + jnp.square(jax.nn.relu(xn2 @ prm['mlp_w1'][l])) @ prm['mlp_w2'][l]
        n_wkv.append(S_r.astype(dt)); n_shift.append(xn[:, -1]); n_hgrn.append(S_h.astype(dt))
        n_lru.append(h_l.astype(dt)); n_conv.append(c_l.astype(dt))
    y = _rmsnorm(x, prm['final_g'])
    return (y, jnp.stack(n_wkv), jnp.stack(n_shift), jnp.stack(n_hgrn), jnp.stack(n_lru), jnp.stack(n_conv))


def setup_inputs(seed: int = 0) -> dict:
    key = jax.random.key(seed)
    ks = jax.random.split(key, 40)
    nrm = lambda i, shape, sc: jax.random.normal(ks[i], shape, F32) * sc
    u = jax.random.uniform(ks[30], (DEPTH, C_LRU), F32, 0.9, 0.999) ** (1.0 / LRU_C)
    return {
        'x_prompt': nrm(0, (BATCH, SEQ, D_MODEL), 1.0),
        'x_sample': nrm(1, (DEC_BATCH, DEC_SEQ, D_MODEL), 1.0),
        'state_wkv': nrm(2, (DEPTH, DEC_BATCH, RWKV_HEADS, HEAD_DIM, HEAD_DIM), 0.3),
        'state_shift': nrm(3, (DEPTH, DEC_BATCH, D_MODEL), 1.0),
        'state_hgrn': nrm(4, (DEPTH, DEC_BATCH, HGRN_HEADS, HGRN_KDIM, HEAD_DIM), 0.3),
        'state_lru': nrm(5, (DEPTH, DEC_BATCH, C_LRU), 0.5),
        'state_conv': nrm(6, (DEPTH, DEC_BATCH, CONV_WIDTH - 1, C_LRU), 1.0),
        'norm1_g': 1.0 + nrm(7, (DEPTH, D_MODEL), 0.02),
        'w_in': nrm(8, (DEPTH, D_MODEL, C_IN), D_MODEL ** -0.5),
        'mu_shift': jax.random.uniform(ks[9], (DEPTH, RWKV_COLS), F32),
        'rwkv_w0': jax.random.uniform(ks[10], (DEPTH, C_RWKV), F32, -6.0, -1.0),
        'rwkv_w_up': nrm(11, (DEPTH, W_LORA, C_RWKV), 0.5 * W_LORA ** -0.5),
        'rwkv_a0': nrm(12, (DEPTH, C_RWKV), 0.1),
        'rwkv_a_up': nrm(13, (DEPTH, A_LORA, C_RWKV), 0.5 * A_LORA ** -0.5),
        'rwkv_g_up': nrm(14, (DEPTH, G_LORA, C_RWKV), G_LORA ** -0.5),
        'rwkv_k_k': 0.85 + nrm(15, (DEPTH, C_RWKV), 0.02),
        'rwkv_k_a': 1.0 + nrm(16, (DEPTH, C_RWKV), 0.02),
        'rwkv_r_k': nrm(17, (DEPTH, RWKV_HEADS, HEAD_DIM), 0.1),
        'rwkv_ln_w': 1.0 + nrm(18, (DEPTH, C_RWKV), 0.02),
        'rwkv_ln_b': nrm(19, (DEPTH, C_RWKV), 0.01),
        'hgrn_lb': nrm(20, (DEPTH, HGRN_FDIM), 0.1),
        'hgrn_norm_g': 1.0 + nrm(21, (DEPTH, C_HGRN), 0.02),
        'lru_conv_w': nrm(22, (DEPTH, CONV_WIDTH, C_LRU), CONV_WIDTH ** -0.5),
        'lru_conv_b': nrm(23, (DEPTH, C_LRU), 0.01),
        'lru_wa': nrm(24, (DEPTH, LRU_BLOCKS, LRU_BLOCK, LRU_BLOCK), LRU_BLOCK ** -0.5),
        'lru_ba': nrm(25, (DEPTH, C_LRU), 0.01),
        'lru_wx': nrm(26, (DEPTH, LRU_BLOCKS, LRU_BLOCK, LRU_BLOCK), LRU_BLOCK ** -0.5),
        'lru_bx': nrm(27, (DEPTH, C_LRU), 0.01),
        'lru_lambda': jnp.log(u) - jnp.log1p(-u),
        'w_out': nrm(28, (DEPTH, D_MIX, D_MODEL), D_MIX ** -0.5),
        'norm2_g': 1.0 + nrm(29, (DEPTH, D_MODEL), 0.02),
        'mlp_w1': nrm(31, (DEPTH, D_MODEL, D_FF), D_MODEL ** -0.5),
        'mlp_w2': nrm(32, (DEPTH, D_FF, D_MODEL), D_FF ** -0.5),
        'final_g': 1.0 + nrm(33, (D_MODEL,), 0.02),
    }


def reference(x_prompt, x_sample, state_wkv, state_shift, state_hgrn, state_lru, state_conv,
              norm1_g, w_in, mu_shift, rwkv_w0, rwkv_w_up, rwkv_a0, rwkv_a_up, rwkv_g_up, rwkv_k_k,
              rwkv_k_a, rwkv_r_k, rwkv_ln_w, rwkv_ln_b, hgrn_lb, hgrn_norm_g, lru_conv_w, lru_conv_b,
              lru_wa, lru_ba, lru_wx, lru_bx, lru_lambda, w_out, norm2_g, mlp_w1, mlp_w2, final_g):
    prm = dict(norm1_g=norm1_g, w_in=w_in, mu_shift=mu_shift, rwkv_w0=rwkv_w0, rwkv_w_up=rwkv_w_up,
               rwkv_a0=rwkv_a0, rwkv_a_up=rwkv_a_up, rwkv_g_up=rwkv_g_up, rwkv_k_k=rwkv_k_k,
               rwkv_k_a=rwkv_k_a, rwkv_r_k=rwkv_r_k, rwkv_ln_w=rwkv_ln_w, rwkv_ln_b=rwkv_ln_b,
               hgrn_lb=hgrn_lb, hgrn_norm_g=hgrn_norm_g, lru_conv_w=lru_conv_w, lru_conv_b=lru_conv_b,
               lru_wa=lru_wa, lru_ba=lru_ba, lru_wx=lru_wx, lru_bx=lru_bx, lru_lambda=lru_lambda,
               w_out=w_out, norm2_g=norm2_g, mlp_w1=mlp_w1, mlp_w2=mlp_w2, final_g=final_g)
    Bp, dt = x_prompt.shape[0], x_prompt.dtype
    z_wkv = jnp.zeros((DEPTH, Bp, RWKV_HEADS, HEAD_DIM, HEAD_DIM), dt)
    z_shift = jnp.zeros((DEPTH, Bp, D_MODEL), dt)
    z_hgrn = jnp.zeros((DEPTH, Bp, HGRN_HEADS, HGRN_KDIM, HEAD_DIM), dt)
    z_lru = jnp.zeros((DEPTH, Bp, C_LRU), dt)
    z_conv = jnp.zeros((DEPTH, Bp, CONV_WIDTH - 1, C_LRU), dt)
    y_prompt, p_wkv, p_shift, p_hgrn, p_lru, p_conv = _trunk(x_prompt, z_wkv, z_shift, z_hgrn, z_lru, z_conv, prm)
    y_sample, s_wkv, s_shift, s_hgrn, s_lru, s_conv = _trunk(x_sample, state_wkv, state_shift, state_hgrn,
                                                             state_lru, state_conv, prm)
    return (y_prompt, y_sample, p_wkv, p_shift, p_hgrn, p_lru, p_conv, s_wkv, s_shift, s_hgrn, s_lru, s_conv)
```

```python
import functools

import jax
import jax.numpy as jnp
from jax import lax
from jax.experimental import pallas as pl
from jax.experimental.pallas import tpu as pltpu

F32 = jnp.float32
BF16 = jnp.bfloat16

D_MODEL = 1024
DEPTH = 4
HEAD_DIM = 64
N_HEADS = 6
C_HEADS = N_HEADS * HEAD_DIM
C_LRU = 256
LRU_BLOCKS = 4
CONV_WIDTH = 4
LRU_C = 8.0
W_LORA = 64
A_LORA = 64
G_LORA = 128
RWKV_COLS = 3 * C_HEADS + W_LORA + A_LORA + G_LORA
HGRN_COLS = 4 * C_HEADS
LRU_COLS = 2 * C_LRU
C_IN = RWKV_COLS + HGRN_COLS + LRU_COLS
D_FF = 4 * D_MODEL
NORM_EPS = 1e-6
GN_EPS = 64e-5

LANES = 128
VMEM_LIMIT_BYTES = 48 * 1024 * 1024


def _cparams(*sem):
    return pltpu.CompilerParams(dimension_semantics=sem, vmem_limit_bytes=VMEM_LIMIT_BYTES)


def _dot(a, b):
    return jnp.dot(a.astype(BF16), b.astype(BF16), preferred_element_type=F32)


def _rmsnorm(x, g):
    return x * lax.rsqrt(jnp.mean(x * x, axis=-1, keepdims=True) + NORM_EPS) * g


def _softplus(x):
    return jnp.maximum(x, 0.0) + jnp.log1p(jnp.exp(-jnp.abs(x)))


def _head_sum(x):
    m = x.shape[0]
    lo = lax.broadcasted_iota(jnp.int32, (m, LANES), 1) < HEAD_DIM
    outs = []
    for p in range(x.shape[1] // LANES):
        xp = x[:, p * LANES:(p + 1) * LANES]
        s_lo = jnp.sum(jnp.where(lo, xp, 0.0), axis=1, keepdims=True)
        s_hi = jnp.sum(jnp.where(lo, 0.0, xp), axis=1, keepdims=True)
        outs.append(jnp.where(lo, s_lo, s_hi))
    return jnp.concatenate(outs, axis=1)


def _norm_proj_kernel(x_ref, g_ref, w_ref, xn_ref, pr_ref, ph_ref, pl_ref):
    xn = _rmsnorm(x_ref[...], g_ref[...])
    xn_ref[...] = xn
    p = jnp.dot(xn.astype(BF16), w_ref[...], preferred_element_type=F32)
    pr_ref[...] = p[:, :RWKV_COLS]
    ph_ref[...] = p[:, RWKV_COLS:RWKV_COLS + HGRN_COLS]
    pl_ref[...] = p[:, RWKV_COLS + HGRN_COLS:]


def _norm_proj(x, g, w):
    m = x.shape[0]
    tm = min(m, 256)
    row = lambda c: pl.BlockSpec((tm, c), lambda i: (i, 0))
    full = lambda a: pl.BlockSpec(a.shape, lambda i: (0,) * a.ndim)
    return pl.pallas_call(
        _norm_proj_kernel,
        grid=(m // tm,),
        in_specs=[row(D_MODEL), full(g), full(w)],
        out_specs=[row(D_MODEL), row(RWKV_COLS), row(HGRN_COLS), row(LRU_COLS)],
        out_shape=[jax.ShapeDtypeStruct((m, c), F32) for c in (D_MODEL, RWKV_COLS, HGRN_COLS, LRU_COLS)],
        compiler_params=_cparams("parallel"),
        name="norm_proj",
    )(x, g, w)


def _proj_kernel(x_ref, w_ref, o_ref):
    o_ref[...] = jnp.dot(x_ref[...].astype(BF16), w_ref[...], preferred_element_type=F32)


def _proj(x, w):
    return pl.pallas_call(
        _proj_kernel,
        out_shape=jax.ShapeDtypeStruct((x.shape[0], w.shape[1]), F32),
        compiler_params=pltpu.CompilerParams(vmem_limit_bytes=VMEM_LIMIT_BYTES),
        name="shift_proj",
    )(x, w)


def _out_mlp_kernel(x_ref, mix_ref, wo_ref, g2_ref, w1_ref, w2_ref, o_ref, x1_s, xn_s, acc_s):
    j = pl.program_id(1)

    @pl.when(j == 0)
    def _():
        x1 = x_ref[...] + jnp.dot(mix_ref[...].astype(BF16), wo_ref[...], preferred_element_type=F32)
        x1_s[...] = x1
        xn_s[...] = _rmsnorm(x1, g2_ref[...]).astype(BF16)
        acc_s[...] = jnp.zeros_like(acc_s)

    h = jnp.dot(xn_s[...], w1_ref[...], preferred_element_type=F32)
    h = jnp.square(jnp.maximum(h, 0.0))
    acc_s[...] += jnp.dot(h.astype(BF16), w2_ref[...], preferred_element_type=F32)

    @pl.when(j == pl.num_programs(1) - 1)
    def _():
        o_ref[...] = x1_s[...] + acc_s[...]


def _out_mlp(x, mix, wo, g2, w1, w2):
    m = x.shape[0]
    tm = min(m, 512)
    tf = 1024
    row = pl.BlockSpec((tm, D_MODEL), lambda i, j: (i, 0))
    return pl.pallas_call(
        _out_mlp_kernel,
        grid=(m // tm, D_FF // tf),
        in_specs=[row, row,
                  pl.BlockSpec((D_MODEL, D_MODEL), lambda i, j: (0, 0)),
                  pl.BlockSpec((1, D_MODEL), lambda i, j: (0, 0)),
                  pl.BlockSpec((D_MODEL, tf), lambda i, j: (0, j)),
                  pl.BlockSpec((tf, D_MODEL), lambda i, j: (j, 0))],
        out_specs=row,
        out_shape=jax.ShapeDtypeStruct((m, D_MODEL), F32),
        scratch_shapes=[pltpu.VMEM((tm, D_MODEL), F32), pltpu.VMEM((tm, D_MODEL), BF16),
                        pltpu.VMEM((tm, D_MODEL), F32)],
        compiler_params=_cparams("parallel", "arbitrary"),
        name="out_mlp",
    )(x, mix, wo, g2, w1, w2)


def _final_norm_kernel(x_ref, g_ref, o_ref):
    o_ref[...] = _rmsnorm(x_ref[...], g_ref[...])


def _final_norm(x, g):
    m = x.shape[0]
    tm = min(m, 512)
    row = pl.BlockSpec((tm, D_MODEL), lambda i: (i, 0))
    return pl.pallas_call(
        _final_norm_kernel,
        grid=(m // tm,),
        in_specs=[row, pl.BlockSpec((1, D_MODEL), lambda i: (0, 0))],
        out_specs=row,
        out_shape=jax.ShapeDtypeStruct((m, D_MODEL), F32),
        compiler_params=_cparams("parallel"),
        name="final_norm",
    )(x, g)


def _lb_kernel(p_ref, o_ref):
    p = p_ref[...]
    e = jnp.exp(p - jnp.max(p, axis=0, keepdims=True))
    s = e / jnp.sum(e, axis=0, keepdims=True)
    acc = jnp.zeros_like(s[0:1])
    for l in range(DEPTH):
        if l > 0:
            acc = acc + s[l:l + 1]
        o_ref[l:l + 1, :] = acc


def _hgrn_lower_bounds(hgrn_lb):
    return pl.pallas_call(_lb_kernel, out_shape=jax.ShapeDtypeStruct(hgrn_lb.shape, F32),
                          name="hgrn_lower_bounds")(hgrn_lb)


def _rwkv_pre(pr, prev, mu, w0, w_up, a0, a_up, g_up, k_k, k_a):
    c = C_HEADS
    pm = pr + mu * (prev - pr)
    r = pm[:, :c]
    k0 = pm[:, c:2 * c]
    v = pm[:, 2 * c:3 * c]
    xwa = pm[:, 3 * c:3 * c + W_LORA + A_LORA]
    xg = pm[:, 3 * c + W_LORA + A_LORA:]
    w_log = -_softplus(-(w0 + _dot(jnp.tanh(xwa), w_up))) - 0.5
    decay = jnp.exp(-jnp.exp(w_log))
    a = jax.nn.sigmoid(a0 + _dot(xwa, a_up))
    g = _dot(jax.nn.sigmoid(xg), g_up)
    kk = k0 * k_k
    kk = kk * lax.rsqrt(jnp.maximum(_head_sum(kk * kk), 1e-24))
    k = k0 * (1.0 + (a - 1.0) * k_a)
    return r, decay, k, v, kk, kk * a, g


def _hgrn_pre(fpre, lb):
    sg = jax.nn.sigmoid(fpre)
    f = lb + (1.0 - lb) * sg
    kin = (1.0 - lb) * jax.nn.sigmoid(-fpre)
    return f, kin


def _lru_pre(xb, x1, x2, x3, conv_w, conv_b, wa, ba, wx, bx, lam):
    xc = conv_b + (((x3 * conv_w[0:1] + x2 * conv_w[1:2]) + x1 * conv_w[2:3]) + xb * conv_w[3:4])
    r = jax.nn.sigmoid(_dot(xc, wa) + ba)
    i = jax.nn.sigmoid(_dot(xc, wx) + bx)
    log_a = -LRU_C * r * _softplus(-lam)
    a = jnp.exp(log_a)
    one_minus_a2 = -jnp.tanh(log_a) * (jnp.exp(2.0 * log_a) + 1.0)
    u = xc * i * jnp.sqrt(jnp.maximum(one_minus_a2, 1e-12))
    return a, u


_N_PRE_PARAMS = 16


def _pre_compute(pr, prev, ph, xb, x1, x2, x3, prm, outs):
    (mu, w0, w_up, a0, a_up, g_up, k_k, k_a, lb, conv_w, conv_b, wa, ba, wx, bx, lam) = prm
    r_o, w_o, k_o, v_o, kk_o, ka_o, g_o, f_o, kin_o, a_o, u_o = outs
    r, w, k, v, kk, ka, g = _rwkv_pre(pr, prev, mu[...], w0[...], w_up[...], a0[...], a_up[...],
                                      g_up[...], k_k[...], k_a[...])
    f, kin = _hgrn_pre(ph[:, C_HEADS:2 * C_HEADS], lb[...])
    a, u = _lru_pre(xb, x1, x2, x3, conv_w[...], conv_b[...], wa[...], ba[...], wx[...], bx[...], lam[...])
    return ((r_o, r), (w_o, w), (k_o, k), (v_o, v), (kk_o, kk), (ka_o, ka), (g_o, g),
            (f_o, f), (kin_o, kin), (a_o, a), (u_o, u))


def _pre_seq_kernel(*refs, tm):
    pr_ref, ph_ref, pl_ref, p0_ref, c0_ref = refs[:5]
    prm = refs[5:5 + _N_PRE_PARAMS]
    outs = refs[5 + _N_PRE_PARAMS:5 + _N_PRE_PARAMS + 11]
    carry_s, xbuf_s = refs[5 + _N_PRE_PARAMS + 11:]
    j = pl.program_id(1)

    @pl.when(j == 0)
    def _():
        carry_s[...] = p0_ref[0]
        xbuf_s[5:8, :] = c0_ref[0]

    pr = pr_ref[0]
    row0 = lax.broadcasted_iota(jnp.int32, pr.shape, 0) == 0
    prev = jnp.where(row0, carry_s[...], pltpu.roll(pr, 1, 0))
    carry_s[...] = pr[tm - 1:tm, :]

    xb = pl_ref[0][:, :C_LRU]
    xbuf_s[8:8 + tm, :] = xb
    x1 = xbuf_s[7:7 + tm, :]
    x2 = xbuf_s[6:6 + tm, :]
    x3 = xbuf_s[5:5 + tm, :]
    res = _pre_compute(pr, prev, ph_ref[0], xb, x1, x2, x3, prm, outs)
    xbuf_s[5:8, :] = xb[tm - 3:tm, :]
    for o_ref, val in res:
        o_ref[0] = val


def _pre_step_kernel(*refs):
    pr_ref, ph_ref, pl_ref, p0_ref, c0_ref, c1_ref, c2_ref = refs[:7]
    prm = refs[7:7 + _N_PRE_PARAMS]
    outs = refs[7 + _N_PRE_PARAMS:]
    xb = pl_ref[...][:, :C_LRU]
    res = _pre_compute(pr_ref[...], p0_ref[...], ph_ref[...], xb, c2_ref[...], c1_ref[...], c0_ref[...],
                       prm, outs)
    for o_ref, val in res:
        o_ref[...] = val


def _mixer_pre(pr, ph, plru, p0, conv0, prm, B, T):
    widths = (C_HEADS,) * 9 + (C_LRU,) * 2
    full = lambda a, nd: pl.BlockSpec(a.shape, (lambda *_: (0,) * a.ndim))
    if T == 1:
        outs = pl.pallas_call(
            _pre_step_kernel,
            out_shape=[jax.ShapeDtypeStruct((B, c), F32) for c in widths],
            compiler_params=pltpu.CompilerParams(vmem_limit_bytes=VMEM_LIMIT_BYTES),
            name="mixer_pre_step",
        )(pr, ph, plru, p0, conv0[:, 0], conv0[:, 1], conv0[:, 2], *prm)
        return outs
    tm = min(T, 256)
    seq = lambda c: pl.BlockSpec((1, tm, c), lambda b, j: (b, j, 0))
    outs = pl.pallas_call(
        functools.partial(_pre_seq_kernel, tm=tm),
        grid=(B, T // tm),
        in_specs=[seq(RWKV_COLS), seq(HGRN_COLS), seq(LRU_COLS),
                  pl.BlockSpec((1, 1, RWKV_COLS), lambda b, j: (b, 0, 0)),
                  pl.BlockSpec((1, CONV_WIDTH - 1, C_LRU), lambda b, j: (b, 0, 0))]
                 + [full(a, 2) for a in prm],
        out_specs=[seq(c) for c in widths],
        out_shape=[jax.ShapeDtypeStruct((B, T, c), F32) for c in widths],
        scratch_shapes=[pltpu.VMEM((1, RWKV_COLS), F32), pltpu.VMEM((tm + 8, C_LRU), F32)],
        compiler_params=_cparams("parallel", "arbitrary"),
        name="mixer_pre_seq",
    )(pr.reshape(B, T, -1), ph.reshape(B, T, -1), plru.reshape(B, T, -1),
      p0.reshape(B, 1, -1), conv0, *prm)
    return [o.reshape(B * T, -1) for o in outs]


def _eye():
    return (lax.broadcasted_iota(jnp.int32, (HEAD_DIM, HEAD_DIM), 0)
            == lax.broadcasted_iota(jnp.int32, (HEAD_DIM, HEAD_DIM), 1))


def _to_col(row, eye):
    return jnp.sum(jnp.where(eye, row, 0.0), axis=1, keepdims=True)


def _to_row(col, eye):
    return jnp.sum(jnp.where(eye, col, 0.0), axis=0, keepdims=True)


def _rwkv_scan_kernel(r_ref, w_ref, k_ref, v_ref, kk_ref, ka_ref, s0_ref, o_ref, sT_ref, S_s, *, bb, tc):
    j = pl.program_id(1)

    @pl.when(j == 0)
    def _():
        S_s[...] = s0_ref[...]

    eye = _eye()

    def step(t, carry):
        for b in range(bb):
            for h in range(N_HEADS):
                idx = (b, t, pl.ds(h, 1), slice(None))
                S = S_s[b, h]
                kk = kk_ref[idx]
                sa = -jnp.sum(S * kk, axis=1, keepdims=True)
                S = S * w_ref[idx] + sa * ka_ref[idx] + _to_col(v_ref[idx], eye) * k_ref[idx]
                S_s[b, h] = S
                o_ref[idx] = _to_row(jnp.sum(S * r_ref[idx], axis=1, keepdims=True), eye)
        return carry

    lax.fori_loop(0, tc, step, 0)

    @pl.when(j == pl.num_programs(1) - 1)
    def _():
        sT_ref[...] = S_s[...]


def _hgrn_scan_kernel(q_ref, f_ref, kin_ref, iv_ref, s0_ref, o_ref, sT_ref, S_s, *, bb, tc):
    j = pl.program_id(1)

    @pl.when(j == 0)
    def _():
        S_s[...] = s0_ref[...]

    eye = _eye()

    def step(t, carry):
        for b in range(bb):
            for h in range(N_HEADS):
                idx = (b, t, pl.ds(h, 1), slice(None))
                S = S_s[b, h]
                S = _to_col(f_ref[idx], eye) * S + _to_col(kin_ref[idx], eye) * iv_ref[idx]
                S_s[b, h] = S
                o_ref[idx] = jnp.sum(S * _to_col(q_ref[idx], eye), axis=0, keepdims=True)
        return carry

    lax.fori_loop(0, tc, step, 0)

    @pl.when(j == pl.num_programs(1) - 1)
    def _():
        sT_ref[...] = S_s[...]


def _lru_scan_kernel(a_ref, u_ref, h0_ref, o_ref, hT_ref, h_s, *, bb, tc):
    j = pl.program_id(1)

    @pl.when(j == 0)
    def _():
        h_s[...] = h0_ref[...]

    def step(t, carry):
        for b in range(bb):
            idx = (b, pl.ds(t, 1), slice(None))
            h = a_ref[idx] * h_s[b] + u_ref[idx]
            h_s[b] = h
            o_ref[idx] = h
        return carry

    lax.fori_loop(0, tc, step, 0)

    @pl.when(j == pl.num_programs(1) - 1)
    def _():
        hT_ref[...] = h_s[...]


def _scan_tiles(B, T):
    bb = 8 if T == 1 else 4
    tc = min(T, 32)
    return bb, tc


def _head_scan(kernel, name, seqs, s0, B, T):
    bb, tc = _scan_tiles(B, T)
    hblk = lambda g: pl.BlockSpec((bb, tc, None, N_HEADS, HEAD_DIM), lambda i, j: (i, j, g, 0, 0))
    sblk = pl.BlockSpec((bb, N_HEADS, HEAD_DIM, HEAD_DIM), lambda i, j: (i, 0, 0, 0))
    return pl.pallas_call(
        functools.partial(kernel, bb=bb, tc=tc),
        grid=(B // bb, T // tc),
        in_specs=[hblk(g) for _, g in seqs] + [sblk],
        out_specs=[hblk(0), sblk],
        out_shape=[jax.ShapeDtypeStruct((B, T, 1, N_HEADS, HEAD_DIM), F32),
                   jax.ShapeDtypeStruct(s0.shape, F32)],
        scratch_shapes=[pltpu.VMEM((bb, N_HEADS, HEAD_DIM, HEAD_DIM), F32)],
        compiler_params=_cparams("parallel", "arbitrary"),
        name=name,
    )(*[a for a, _ in seqs], s0)


def _lru_scan(a, u, h0, B, T):
    bb, tc = _scan_tiles(B, T)
    blk = pl.BlockSpec((bb, tc, C_LRU), lambda i, j: (i, j, 0))
    hblk = pl.BlockSpec((bb, 1, C_LRU), lambda i, j: (i, 0, 0))
    o, hT = pl.pallas_call(
        functools.partial(_lru_scan_kernel, bb=bb, tc=tc),
        grid=(B // bb, T // tc),
        in_specs=[blk, blk, hblk],
        out_specs=[blk, hblk],
        out_shape=[jax.ShapeDtypeStruct((B, T, C_LRU), F32), jax.ShapeDtypeStruct((B, 1, C_LRU), F32)],
        scratch_shapes=[pltpu.VMEM((bb, 1, C_LRU), F32)],
        compiler_params=_cparams("parallel", "arbitrary"),
        name="lru_scan",
    )(a.reshape(B, T, C_LRU), u.reshape(B, T, C_LRU), h0.reshape(B, 1, C_LRU))
    return o.reshape(B * T, C_LRU), hT.reshape(B, C_LRU)


def _mix_post_kernel(wo_ref, r_ref, k_ref, v_ref, g_ref, rk_ref, lnw_ref, lnb_ref,
                     ho_ref, hg_ref, hng_ref, lh_ref, lg_ref, mix_ref):
    out = wo_ref[...]
    inv_n = 1.0 / HEAD_DIM
    mean = _head_sum(out) * inv_n
    d = out - mean
    var = _head_sum(d * d) * inv_n
    gn = d * lax.rsqrt(var + GN_EPS) * lnw_ref[...] + lnb_ref[...]
    bonus = _head_sum(r_ref[...] * k_ref[...] * rk_ref[...]) * v_ref[...]
    mix_ref[:, :C_HEADS] = (gn + bonus) * g_ref[...]

    o = ho_ref[...]
    o = o * lax.rsqrt(_head_sum(o * o) * inv_n + NORM_EPS)
    mix_ref[:, C_HEADS:2 * C_HEADS] = o * hng_ref[...] * jax.nn.silu(hg_ref[...])

    mix_ref[:, 2 * C_HEADS:] = lh_ref[...] * jax.nn.gelu(lg_ref[...])


def _mix_post(wo, r, k, v, g, r_k, ln_w, ln_b, ho, ph, hng, lh, plru):
    m = wo.shape[0]
    tm = min(m, 512)
    hrow = pl.BlockSpec((tm, C_HEADS), lambda i: (i, 0))
    prow = lambda c: pl.BlockSpec((1, c), lambda i: (0, 0))
    return pl.pallas_call(
        _mix_post_kernel,
        grid=(m // tm,),
        in_specs=[hrow, hrow, hrow, hrow, hrow, prow(C_HEADS), prow(C_HEADS), prow(C_HEADS),
                  hrow, pl.BlockSpec((tm, C_HEADS), lambda i: (i, 3)), prow(C_HEADS),
                  pl.BlockSpec((tm, C_LRU), lambda i: (i, 0)),
                  pl.BlockSpec((tm, C_LRU), lambda i: (i, 1))],
        out_specs=pl.BlockSpec((tm, D_MODEL), lambda i: (i, 0)),
        out_shape=jax.ShapeDtypeStruct((m, D_MODEL), F32),
        compiler_params=_cparams("parallel"),
        name="mix_post",
    )(wo, r, k, v, g, r_k, ln_w, ln_b, ho, ph, hng, lh, plru)


def _block_diag(w):
    out = jnp.zeros((C_LRU, C_LRU), w.dtype)
    n = C_LRU // LRU_BLOCKS
    for i in range(LRU_BLOCKS):
        out = out.at[i * n:(i + 1) * n, i * n:(i + 1) * n].set(w[i])
    return out


def _prepare_params(p):
    row = lambda a: a.reshape(1, -1)
    zpad = jnp.zeros((W_LORA, C_HEADS), F32)
    layers = []
    for l in range(DEPTH):
        pre = (row(p['mu_shift'][l]), row(p['rwkv_w0'][l]),
               jnp.concatenate([p['rwkv_w_up'][l], zpad], 0).astype(BF16),
               row(p['rwkv_a0'][l]),
               jnp.concatenate([zpad, p['rwkv_a_up'][l]], 0).astype(BF16),
               p['rwkv_g_up'][l].astype(BF16), row(p['rwkv_k_k'][l]), row(p['rwkv_k_a'][l]),
               None,
               p['lru_conv_w'][l], row(p['lru_conv_b'][l]),
               _block_diag(p['lru_wa'][l]).astype(BF16), row(p['lru_ba'][l]),
               _block_diag(p['lru_wx'][l]).astype(BF16), row(p['lru_bx'][l]), row(p['lru_lambda'][l]))
        layers.append(dict(
            norm1_g=row(p['norm1_g'][l]), w_in=p['w_in'][l].astype(BF16), pre=pre,
            r_k=row(p['rwkv_r_k'][l]), ln_w=row(p['rwkv_ln_w'][l]), ln_b=row(p['rwkv_ln_b'][l]),
            hgrn_norm_g=row(p['hgrn_norm_g'][l]),
            w_out=p['w_out'][l].astype(BF16), norm2_g=row(p['norm2_g'][l]),
            mlp_w1=p['mlp_w1'][l].astype(BF16), mlp_w2=p['mlp_w2'][l].astype(BF16)))
    return layers


def _trunk(x, wkv, shift, hgrn, lru, conv, layers, lb_all, final_g):
    B, T, _ = x.shape
    m = B * T
    x = x.reshape(m, D_MODEL)
    heads = lambda a: a.reshape(B, T, -1, N_HEADS, HEAD_DIM)
    n_wkv, n_shift, n_hgrn, n_lru, n_conv = [], [], [], [], []
    for l, lp in enumerate(layers):
        xn, pr, ph, plru = _norm_proj(x, lp['norm1_g'], lp['w_in'])
        p0 = _proj(shift[l], lp['w_in'][:, :RWKV_COLS])
        prm = list(lp['pre'])
        prm[8] = lb_all[l:l + 1]
        r, w, k, v, kk, ka, g, f, kin, a, u = _mixer_pre(pr, ph, plru, p0, conv[l], prm, B, T)
        wo, S_r = _head_scan(_rwkv_scan_kernel, "rwkv_scan",
                             [(heads(t), 0) for t in (r, w, k, v, kk, ka)], wkv[l], B, T)
        ph4 = heads(ph)
        ho, S_h = _head_scan(_hgrn_scan_kernel, "hgrn_scan",
                             [(ph4, 0), (heads(f), 0), (heads(kin), 0), (ph4, 2)], hgrn[l], B, T)
        lh, h_l = _lru_scan(a, u, lru[l], B, T)
        mix = _mix_post(wo.reshape(m, C_HEADS), r, k, v, g, lp['r_k'], lp['ln_w'], lp['ln_b'],
                        ho.reshape(m, C_HEADS), ph, lp['hgrn_norm_g'], lh, plru)
        x = _out_mlp(x, mix, lp['w_out'], lp['norm2_g'], lp['mlp_w1'], lp['mlp_w2'])
        xb = plru.reshape(B, T, LRU_COLS)[:, :, :C_LRU]
        hist = jnp.concatenate([conv[l], xb[:, max(T - (CONV_WIDTH - 1), 0):]], axis=1)
        n_wkv.append(S_r)
        n_shift.append(xn.reshape(B, T, D_MODEL)[:, -1])
        n_hgrn.append(S_h)
        n_lru.append(h_l)
        n_conv.append(hist[:, -(CONV_WIDTH - 1):])
    y = _final_norm(x, final_g).reshape(B, T, D_MODEL)
    return (y, jnp.stack(n_wkv), jnp.stack(n_shift), jnp.stack(n_hgrn), jnp.stack(n_lru), jnp.stack(n_conv))


def kernel(x_prompt, x_sample, state_wkv, state_shift, state_hgrn, state_lru, state_conv, norm1_g, w_in, mu_shift, rwkv_w0, rwkv_w_up, rwkv_a0, rwkv_a_up, rwkv_g_up, rwkv_k_k, rwkv_k_a, rwkv_r_k, rwkv_ln_w, rwkv_ln_b, hgrn_lb, hgrn_norm_g, lru_conv_w, lru_conv_b, lru_wa, lru_ba, lru_wx, lru_bx, lru_lambda, w_out, norm2_g, mlp_w1, mlp_w2, final_g):
    prm = dict(norm1_g=norm1_g, w_in=w_in, mu_shift=mu_shift, rwkv_w0=rwkv_w0, rwkv_w_up=rwkv_w_up,
               rwkv_a0=rwkv_a0, rwkv_a_up=rwkv_a_up, rwkv_g_up=rwkv_g_up, rwkv_k_k=rwkv_k_k,
               rwkv_k_a=rwkv_k_a, rwkv_r_k=rwkv_r_k.reshape(DEPTH, C_HEADS), rwkv_ln_w=rwkv_ln_w,
               rwkv_ln_b=rwkv_ln_b, hgrn_norm_g=hgrn_norm_g, lru_conv_w=lru_conv_w,
               lru_conv_b=lru_conv_b, lru_wa=lru_wa, lru_ba=lru_ba, lru_wx=lru_wx, lru_bx=lru_bx,
               lru_lambda=lru_lambda, w_out=w_out, norm2_g=norm2_g, mlp_w1=mlp_w1, mlp_w2=mlp_w2)
    layers = _prepare_params(prm)
    lb_all = _hgrn_lower_bounds(hgrn_lb)
    fg = final_g.reshape(1, D_MODEL)
    Bp = x_prompt.shape[0]
    dt = x_prompt.dtype
    z_wkv = jnp.zeros((DEPTH, Bp, N_HEADS, HEAD_DIM, HEAD_DIM), dt)
    z_shift = jnp.zeros((DEPTH, Bp, D_MODEL), dt)
    z_hgrn = jnp.zeros((DEPTH, Bp, N_HEADS, HEAD_DIM, HEAD_DIM), dt)
    z_lru = jnp.zeros((DEPTH, Bp, C_LRU), dt)
    z_conv = jnp.zeros((DEPTH, Bp, CONV_WIDTH - 1, C_LRU), dt)
    y_p, p_wkv, p_shift, p_hgrn, p_lru, p_conv = _trunk(x_prompt, z_wkv, z_shift, z_hgrn, z_lru, z_conv,
                                                         layers, lb_all, fg)
    y_s, s_wkv, s_shift, s_hgrn, s_lru, s_conv = _trunk(x_sample, state_wkv, state_shift, state_hgrn,
                                                         state_lru, state_conv, layers, lb_all, fg)
    return (y_p, y_s, p_wkv, p_shift, p_hgrn, p_lru, p_conv, s_wkv, s_shift, s_hgrn, s_lru, s_conv)
```

```python
import functools

import jax
import jax.numpy as jnp
from jax import lax
from jax.experimental import pallas as pl
from jax.experimental.pallas import tpu as pltpu

F32 = jnp.float32
BF16 = jnp.bfloat16

D_MODEL = 1024
DEPTH = 4
HEAD_DIM = 64
N_HEADS = 6
C_HEADS = N_HEADS * HEAD_DIM
C_LRU = 256
LRU_BLOCKS = 4
CONV_WIDTH = 4
LRU_C = 8.0
W_LORA = 64
A_LORA = 64
G_LORA = 128
RWKV_COLS = 3 * C_HEADS + W_LORA + A_LORA + G_LORA
HGRN_COLS = 4 * C_HEADS
LRU_COLS = 2 * C_LRU
C_IN = RWKV_COLS + HGRN_COLS + LRU_COLS
D_FF = 4 * D_MODEL
NORM_EPS = 1e-6
GN_EPS = 64e-5

LANES = 128
VMEM_LIMIT_BYTES = 48 * 1024 * 1024


def _cparams(*sem):
    return pltpu.CompilerParams(dimension_semantics=sem, vmem_limit_bytes=VMEM_LIMIT_BYTES)


def _dot(a, b):
    return jnp.dot(a.astype(BF16), b.astype(BF16), preferred_element_type=F32)


def _rmsnorm(x, g):
    return x * lax.rsqrt(jnp.mean(x * x, axis=-1, keepdims=True) + NORM_EPS) * g


def _softplus(x):
    return jnp.maximum(x, 0.0) + jnp.log1p(jnp.exp(-jnp.abs(x)))


def _head_sum(x):
    m = x.shape[0]
    lo = lax.broadcasted_iota(jnp.int32, (m, LANES), 1) < HEAD_DIM
    outs = []
    for p in range(x.shape[1] // LANES):
        xp = x[:, p * LANES:(p + 1) * LANES]
        s_lo = jnp.sum(jnp.where(lo, xp, 0.0), axis=1, keepdims=True)
        s_hi = jnp.sum(jnp.where(lo, 0.0, xp), axis=1, keepdims=True)
        outs.append(jnp.where(lo, s_lo, s_hi))
    return jnp.concatenate(outs, axis=1)


def _norm_proj_kernel(x_ref, g_ref, w_ref, xn_ref, pr_ref, ph_ref, pl_ref):
    xn = _rmsnorm(x_ref[...], g_ref[...])
    xn_ref[...] = xn
    p = jnp.dot(xn.astype(BF16), w_ref[...], preferred_element_type=F32)
    pr_ref[...] = p[:, :RWKV_COLS]
    ph_ref[...] = p[:, RWKV_COLS:RWKV_COLS + HGRN_COLS]
    pl_ref[...] = p[:, RWKV_COLS + HGRN_COLS:]


def _norm_proj(x, g, w):
    m = x.shape[0]
    tm = min(m, 256)
    row = lambda c: pl.BlockSpec((tm, c), lambda i: (i, 0))
    full = lambda a: pl.BlockSpec(a.shape, lambda i: (0,) * a.ndim)
    return pl.pallas_call(
        _norm_proj_kernel,
        grid=(m // tm,),
        in_specs=[row(D_MODEL), full(g), full(w)],
        out_specs=[row(D_MODEL), row(RWKV_COLS), row(HGRN_COLS), row(LRU_COLS)],
        out_shape=[jax.ShapeDtypeStruct((m, c), F32) for c in (D_MODEL, RWKV_COLS, HGRN_COLS, LRU_COLS)],
        compiler_params=_cparams("parallel"),
        name="norm_proj",
    )(x, g, w)


def _proj_kernel(x_ref, w_ref, o_ref):
    o_ref[...] = jnp.dot(x_ref[...].astype(BF16), w_ref[...], preferred_element_type=F32)


def _proj(x, w):
    return pl.pallas_call(
        _proj_kernel,
        out_shape=jax.ShapeDtypeStruct((x.shape[0], w.shape[1]), F32),
        compiler_params=pltpu.CompilerParams(vmem_limit_bytes=VMEM_LIMIT_BYTES),
        name="shift_proj",
    )(x, w)


def _out_mlp_kernel(x_ref, or_ref, oh_ref, ol_ref, wo_ref, g2_ref, w1_ref, w2_ref, o_ref, x1_s, xn_s, acc_s):
    j = pl.program_id(1)

    @pl.when(j == 0)
    def _():
        mix = jnp.concatenate([or_ref[...], oh_ref[...], ol_ref[...]], axis=1)
        x1 = x_ref[...] + jnp.dot(mix.astype(BF16), wo_ref[...], preferred_element_type=F32)
        x1_s[...] = x1
        xn_s[...] = _rmsnorm(x1, g2_ref[...]).astype(BF16)
        acc_s[...] = jnp.zeros_like(acc_s)

    h = jnp.dot(xn_s[...], w1_ref[...], preferred_element_type=F32)
    h = jnp.square(jnp.maximum(h, 0.0))
    acc_s[...] += jnp.dot(h.astype(BF16), w2_ref[...], preferred_element_type=F32)

    @pl.when(j == pl.num_programs(1) - 1)
    def _():
        o_ref[...] = x1_s[...] + acc_s[...]


def _out_mlp(x, o_r, o_h, o_l, wo, g2, w1, w2):
    m = x.shape[0]
    tm = min(m, 512)
    tf = 1024
    row = pl.BlockSpec((tm, D_MODEL), lambda i, j: (i, 0))
    part = lambda c: pl.BlockSpec((tm, c), lambda i, j: (i, 0))
    return pl.pallas_call(
        _out_mlp_kernel,
        grid=(m // tm, D_FF // tf),
        in_specs=[row, part(C_HEADS), part(C_HEADS), part(C_LRU),
                  pl.BlockSpec((D_MODEL, D_MODEL), lambda i, j: (0, 0)),
                  pl.BlockSpec((1, D_MODEL), lambda i, j: (0, 0)),
                  pl.BlockSpec((D_MODEL, tf), lambda i, j: (0, j)),
                  pl.BlockSpec((tf, D_MODEL), lambda i, j: (j, 0))],
        out_specs=row,
        out_shape=jax.ShapeDtypeStruct((m, D_MODEL), F32),
        scratch_shapes=[pltpu.VMEM((tm, D_MODEL), F32), pltpu.VMEM((tm, D_MODEL), BF16),
                        pltpu.VMEM((tm, D_MODEL), F32)],
        compiler_params=_cparams("parallel", "arbitrary"),
        name="out_mlp",
    )(x, o_r, o_h, o_l, wo, g2, w1, w2)


def _final_norm_kernel(x_ref, g_ref, o_ref):
    o_ref[...] = _rmsnorm(x_ref[...], g_ref[...])


def _final_norm(x, g):
    m = x.shape[0]
    tm = min(m, 512)
    row = pl.BlockSpec((tm, D_MODEL), lambda i: (i, 0))
    return pl.pallas_call(
        _final_norm_kernel,
        grid=(m // tm,),
        in_specs=[row, pl.BlockSpec((1, D_MODEL), lambda i: (0, 0))],
        out_specs=row,
        out_shape=jax.ShapeDtypeStruct((m, D_MODEL), F32),
        compiler_params=_cparams("parallel"),
        name="final_norm",
    )(x, g)


def _lb_kernel(p_ref, o_ref):
    p = p_ref[...]
    e = jnp.exp(p - jnp.max(p, axis=0, keepdims=True))
    s = e / jnp.sum(e, axis=0, keepdims=True)
    acc = jnp.zeros_like(s[0:1])
    for l in range(DEPTH):
        if l > 0:
            acc = acc + s[l:l + 1]
        o_ref[l:l + 1, :] = acc


def _hgrn_lower_bounds(hgrn_lb):
    return pl.pallas_call(_lb_kernel, out_shape=jax.ShapeDtypeStruct(hgrn_lb.shape, F32),
                          name="hgrn_lower_bounds")(hgrn_lb)


def _rwkv_pre(pr, prev, mu, w0, w_up, a0, a_up, g_up, k_k, k_a):
    c = C_HEADS
    pm = pr + mu * (prev - pr)
    r = pm[:, :c]
    k0 = pm[:, c:2 * c]
    v = pm[:, 2 * c:3 * c]
    xwa = pm[:, 3 * c:3 * c + W_LORA + A_LORA]
    xg = pm[:, 3 * c + W_LORA + A_LORA:]
    w_log = -_softplus(-(w0 + _dot(jnp.tanh(xwa), w_up))) - 0.5
    log_decay = -jnp.exp(w_log)
    a = jax.nn.sigmoid(a0 + _dot(xwa, a_up))
    g = _dot(jax.nn.sigmoid(xg), g_up)
    kk = k0 * k_k
    kk = kk * lax.rsqrt(jnp.maximum(_head_sum(kk * kk), 1e-24))
    k = k0 * (1.0 + (a - 1.0) * k_a)
    return r, log_decay, k, v, kk, kk * a, g


def _hgrn_pre(fpre, lb):
    sg = jax.nn.sigmoid(fpre)
    f = lb + (1.0 - lb) * sg
    kin = (1.0 - lb) * jax.nn.sigmoid(-fpre)
    return f, kin


def _lru_pre(xb, x1, x2, x3, conv_w, conv_b, wa, ba, wx, bx, lam):
    xc = conv_b + (((x3 * conv_w[0:1] + x2 * conv_w[1:2]) + x1 * conv_w[2:3]) + xb * conv_w[3:4])
    r = jax.nn.sigmoid(_dot(xc, wa) + ba)
    i = jax.nn.sigmoid(_dot(xc, wx) + bx)
    log_a = -LRU_C * r * _softplus(-lam)
    a = jnp.exp(log_a)
    one_minus_a2 = -jnp.tanh(log_a) * (jnp.exp(2.0 * log_a) + 1.0)
    u = xc * i * jnp.sqrt(jnp.maximum(one_minus_a2, 1e-12))
    return a, u


_N_PRE_PARAMS = 16


def _pre_compute(pr, prev, ph, xb, x1, x2, x3, prm, outs):
    (mu, w0, w_up, a0, a_up, g_up, k_k, k_a, lb, conv_w, conv_b, wa, ba, wx, bx, lam) = prm
    r_o, w_o, k_o, v_o, kk_o, ka_o, g_o, f_o, kin_o, a_o, u_o = outs
    r, lw, k, v, kk, ka, g = _rwkv_pre(pr, prev, mu[...], w0[...], w_up[...], a0[...], a_up[...],
                                       g_up[...], k_k[...], k_a[...])
    w = jnp.exp(lw)
    f, kin = _hgrn_pre(ph[:, C_HEADS:2 * C_HEADS], lb[...])
    a, u = _lru_pre(xb, x1, x2, x3, conv_w[...], conv_b[...], wa[...], ba[...], wx[...], bx[...], lam[...])
    return ((r_o, r), (w_o, w), (k_o, k), (v_o, v), (kk_o, kk), (ka_o, ka), (g_o, g),
            (f_o, f), (kin_o, kin), (a_o, a), (u_o, u))


def _pre_step_kernel(*refs):
    pr_ref, ph_ref, pl_ref, p0_ref, c0_ref, c1_ref, c2_ref = refs[:7]
    prm = refs[7:7 + _N_PRE_PARAMS]
    outs = refs[7 + _N_PRE_PARAMS:]
    xb = pl_ref[...][:, :C_LRU]
    res = _pre_compute(pr_ref[...], p0_ref[...], ph_ref[...], xb, c2_ref[...], c1_ref[...], c0_ref[...],
                       prm, outs)
    for o_ref, val in res:
        o_ref[...] = val


def _mixer_pre_step(pr, ph, plru, p0, conv0, prm):
    widths = (C_HEADS,) * 9 + (C_LRU,) * 2
    return pl.pallas_call(
        _pre_step_kernel,
        out_shape=[jax.ShapeDtypeStruct((pr.shape[0], c), F32) for c in widths],
        compiler_params=pltpu.CompilerParams(vmem_limit_bytes=VMEM_LIMIT_BYTES),
        name="mixer_pre_step",
    )(pr, ph, plru, p0, conv0[:, 0], conv0[:, 1], conv0[:, 2], *prm)


def _eye():
    return (lax.broadcasted_iota(jnp.int32, (HEAD_DIM, HEAD_DIM), 0)
            == lax.broadcasted_iota(jnp.int32, (HEAD_DIM, HEAD_DIM), 1))


def _to_col(row, eye):
    return jnp.sum(jnp.where(eye, row, 0.0), axis=1, keepdims=True)


def _to_row(col, eye):
    return jnp.sum(jnp.where(eye, col, 0.0), axis=0, keepdims=True)


def _rwkv_scan_kernel(r_ref, w_ref, k_ref, v_ref, kk_ref, ka_ref, s0_ref, o_ref, sT_ref, S_s, *, bb, tc):
    j = pl.program_id(1)

    @pl.when(j == 0)
    def _():
        S_s[...] = s0_ref[...]

    eye = _eye()

    def step(t, carry):
        for b in range(bb):
            for h in range(N_HEADS):
                idx = (b, t, pl.ds(h, 1), slice(None))
                S = S_s[b, h]
                kk = kk_ref[idx]
                sa = -jnp.sum(S * kk, axis=1, keepdims=True)
                S = S * w_ref[idx] + sa * ka_ref[idx] + _to_col(v_ref[idx], eye) * k_ref[idx]
                S_s[b, h] = S
                o_ref[idx] = _to_row(jnp.sum(S * r_ref[idx], axis=1, keepdims=True), eye)
        return carry

    lax.fori_loop(0, tc, step, 0)

    @pl.when(j == pl.num_programs(1) - 1)
    def _():
        sT_ref[...] = S_s[...]


def _hgrn_scan_kernel(q_ref, f_ref, kin_ref, iv_ref, s0_ref, o_ref, sT_ref, S_s, *, bb, tc):
    j = pl.program_id(1)

    @pl.when(j == 0)
    def _():
        S_s[...] = s0_ref[...]

    eye = _eye()

    def step(t, carry):
        for b in range(bb):
            for h in range(N_HEADS):
                idx = (b, t, pl.ds(h, 1), slice(None))
                S = S_s[b, h]
                S = _to_col(f_ref[idx], eye) * S + _to_col(kin_ref[idx], eye) * iv_ref[idx]
                S_s[b, h] = S
                o_ref[idx] = jnp.sum(S * _to_col(q_ref[idx], eye), axis=0, keepdims=True)
        return carry

    lax.fori_loop(0, tc, step, 0)

    @pl.when(j == pl.num_programs(1) - 1)
    def _():
        sT_ref[...] = S_s[...]


def _lru_scan_kernel(a_ref, u_ref, h0_ref, o_ref, hT_ref, h_s, *, bb, tc):
    j = pl.program_id(1)

    @pl.when(j == 0)
    def _():
        h_s[...] = h0_ref[...]

    def step(t, carry):
        for b in range(bb):
            idx = (b, pl.ds(t, 1), slice(None))
            h = a_ref[idx] * h_s[b] + u_ref[idx]
            h_s[b] = h
            o_ref[idx] = h
        return carry

    lax.fori_loop(0, tc, step, 0)

    @pl.when(j == pl.num_programs(1) - 1)
    def _():
        hT_ref[...] = h_s[...]


def _scan_tiles(B, T):
    bb = 8 if T == 1 else 4
    tc = min(T, 32)
    return bb, tc


def _head_scan(kernel, name, seqs, s0, B, T):
    bb, tc = _scan_tiles(B, T)
    hblk = lambda g: pl.BlockSpec((bb, tc, None, N_HEADS, HEAD_DIM), lambda i, j: (i, j, g, 0, 0))
    sblk = pl.BlockSpec((bb, N_HEADS, HEAD_DIM, HEAD_DIM), lambda i, j: (i, 0, 0, 0))
    return pl.pallas_call(
        functools.partial(kernel, bb=bb, tc=tc),
        grid=(B // bb, T // tc),
        in_specs=[hblk(g) for _, g in seqs] + [sblk],
        out_specs=[hblk(0), sblk],
        out_shape=[jax.ShapeDtypeStruct((B, T, 1, N_HEADS, HEAD_DIM), F32),
                   jax.ShapeDtypeStruct(s0.shape, F32)],
        scratch_shapes=[pltpu.VMEM((bb, N_HEADS, HEAD_DIM, HEAD_DIM), F32)],
        compiler_params=_cparams("parallel", "arbitrary"),
        name=name,
    )(*[a for a, _ in seqs], s0)


def _lru_scan(a, u, h0, B, T):
    bb, tc = _scan_tiles(B, T)
    blk = pl.BlockSpec((bb, tc, C_LRU), lambda i, j: (i, j, 0))
    hblk = pl.BlockSpec((bb, 1, C_LRU), lambda i, j: (i, 0, 0))
    o, hT = pl.pallas_call(
        functools.partial(_lru_scan_kernel, bb=bb, tc=tc),
        grid=(B // bb, T // tc),
        in_specs=[blk, blk, hblk],
        out_specs=[blk, hblk],
        out_shape=[jax.ShapeDtypeStruct((B, T, C_LRU), F32), jax.ShapeDtypeStruct((B, 1, C_LRU), F32)],
        scratch_shapes=[pltpu.VMEM((bb, 1, C_LRU), F32)],
        compiler_params=_cparams("parallel", "arbitrary"),
        name="lru_scan",
    )(a.reshape(B, T, C_LRU), u.reshape(B, T, C_LRU), h0.reshape(B, 1, C_LRU))
    return o.reshape(B * T, C_LRU), hT.reshape(B, C_LRU)


def _rwkv_post(out, r, k, v, g, r_k, ln_w, ln_b):
    inv_n = 1.0 / HEAD_DIM
    mean = _head_sum(out) * inv_n
    d = out - mean
    var = _head_sum(d * d) * inv_n
    gn = d * lax.rsqrt(var + GN_EPS) * ln_w + ln_b
    bonus = _head_sum(r * k * r_k) * v
    return (gn + bonus) * g


def _hgrn_post(o, gate, norm_g):
    o = o * lax.rsqrt(_head_sum(o * o) * (1.0 / HEAD_DIM) + NORM_EPS)
    return o * norm_g * jax.nn.silu(gate)


def _mix_post_kernel(wo_ref, r_ref, k_ref, v_ref, g_ref, rk_ref, lnw_ref, lnb_ref,
                     ho_ref, hg_ref, hng_ref, lh_ref, lg_ref, mix_ref):
    mix_ref[:, :C_HEADS] = _rwkv_post(wo_ref[...], r_ref[...], k_ref[...], v_ref[...], g_ref[...],
                                      rk_ref[...], lnw_ref[...], lnb_ref[...])
    mix_ref[:, C_HEADS:2 * C_HEADS] = _hgrn_post(ho_ref[...], hg_ref[...], hng_ref[...])
    mix_ref[:, 2 * C_HEADS:] = lh_ref[...] * jax.nn.gelu(lg_ref[...])


def _mix_post(wo, r, k, v, g, r_k, ln_w, ln_b, ho, ph, hng, lh, plru):
    m = wo.shape[0]
    tm = min(m, 512)
    hrow = pl.BlockSpec((tm, C_HEADS), lambda i: (i, 0))
    prow = lambda c: pl.BlockSpec((1, c), lambda i: (0, 0))
    return pl.pallas_call(
        _mix_post_kernel,
        grid=(m // tm,),
        in_specs=[hrow, hrow, hrow, hrow, hrow, prow(C_HEADS), prow(C_HEADS), prow(C_HEADS),
                  hrow, pl.BlockSpec((tm, C_HEADS), lambda i: (i, 3)), prow(C_HEADS),
                  pl.BlockSpec((tm, C_LRU), lambda i: (i, 0)),
                  pl.BlockSpec((tm, C_LRU), lambda i: (i, 1))],
        out_specs=pl.BlockSpec((tm, D_MODEL), lambda i: (i, 0)),
        out_shape=jax.ShapeDtypeStruct((m, D_MODEL), F32),
        compiler_params=_cparams("parallel"),
        name="mix_post",
    )(wo, r, k, v, g, r_k, ln_w, ln_b, ho, ph, hng, lh, plru)


CHUNK = 64
SUB = 16
PAIR = 2 * HEAD_DIM
N_PAIRS = N_HEADS // 2

NN = ((1,), (0,))
NT = ((1,), (1,))


def _mm(a, b, dims, passes):
    dn = (dims, ((), ()))
    if passes == 6:
        return lax.dot_general(a, b, dn, precision=lax.Precision.HIGHEST, preferred_element_type=F32)
    ah, bh = a.astype(BF16), b.astype(BF16)
    out = lax.dot_general(ah, bh, dn, preferred_element_type=F32)
    if passes == 3:
        al = (a - ah.astype(F32)).astype(BF16)
        bl = (b - bh.astype(F32)).astype(BF16)
        out = (out + lax.dot_general(ah, bl, dn, preferred_element_type=F32)
               + lax.dot_general(al, bh, dn, preferred_element_type=F32))
    return out


def _iota2(shape):
    return lax.broadcasted_iota(jnp.int32, shape, 0), lax.broadcasted_iota(jnp.int32, shape, 1)


def _pair_state_load(s_ref, S_s, transpose):
    z = jnp.zeros((HEAD_DIM, HEAD_DIM), F32)
    for p in range(N_PAIRS):
        a, b = s_ref[0, 2 * p], s_ref[0, 2 * p + 1]
        if transpose:
            a, b = a.T, b.T
        S_s[p] = jnp.concatenate([jnp.concatenate([a, z], axis=1), jnp.concatenate([z, b], axis=1)], axis=0)


def _pair_state_store(S_s, s_ref, transpose):
    for p in range(N_PAIRS):
        S = S_s[p]
        a, b = S[:HEAD_DIM, :HEAD_DIM], S[HEAD_DIM:, HEAD_DIM:]
        if transpose:
            a, b = a.T, b.T
        s_ref[0, 2 * p] = a
        s_ref[0, 2 * p + 1] = b


def _unit_lower_inverse(L, passes):
    r, c = _iota2(L.shape)
    eye = (r == c).astype(F32)
    diag_blk = (r >> 4) == (c >> 4)
    Ld = jnp.where(diag_blk, L, 0.0)
    N = jnp.where(diag_blk, 0.0, L)
    mm = lambda x, y: _mm(x, y, NN, passes)
    L2 = mm(Ld, Ld)
    L4 = mm(L2, L2)
    L8 = mm(L4, L4)
    Dinv = mm(mm(mm(eye - Ld, eye + L2), eye + L4), eye + L8)
    M = mm(Dinv, N)
    X = mm(eye - M, eye + mm(M, M))
    return mm(X, Dinv)


RWKV_A_PASSES = 1
RWKV_INV_PASSES = 3
RWKV_APPLY_PASSES = 1


def _rwkv_seq_kernel(pr_ref, p0_ref, s0_ref, mu, w0, w_up, a0, a_up, g_up, k_k, k_a, r_k, ln_w, ln_b,
                     o_ref, sT_ref, carry_s, S_s):
    j = pl.program_id(1)
    C = CHUNK

    @pl.when(j == 0)
    def _():
        carry_s[...] = p0_ref[0]
        _pair_state_load(s0_ref, S_s, transpose=False)

    pr = pr_ref[0]
    row0 = lax.broadcasted_iota(jnp.int32, pr.shape, 0) == 0
    prev = jnp.where(row0, carry_s[...], pltpu.roll(pr, 1, 0))
    carry_s[...] = pr[C - 1:C, :]

    r, lw, k, v, kk, kb, g = _rwkv_pre(pr, prev, mu[...], w0[...], w_up[...], a0[...], a_up[...],
                                       g_up[...], k_k[...], k_a[...])
    tr, tc = _iota2((C, C))
    cum = _mm((tr >= tc).astype(F32), lw, NN, 6)
    cum_last = cum[C - 1:C, :]
    e_inv = jnp.exp(-cum)
    e_rel = jnp.exp(cum_last - cum)
    e_last = jnp.exp(cum_last)
    Kg = kk * jnp.exp(cum - lw)
    Rg = r * jnp.exp(cum)
    Ki, Bi = k * e_inv, kb * e_inv
    Kt, Bt = k * e_rel, kb * e_rel

    lo = lax.broadcasted_iota(jnp.int32, (C, PAIR), 1) < HEAD_DIM
    pr_, pc_ = _iota2((PAIR, PAIR))
    strict = (pr_ & (HEAD_DIM - 1)) > (pc_ & (HEAD_DIM - 1))
    incl = (pr_ & (HEAD_DIM - 1)) >= (pc_ & (HEAD_DIM - 1))
    same_head = (pr_ >= HEAD_DIM) == (pc_ >= HEAD_DIM)
    by_head = lambda x: jnp.concatenate([jnp.where(lo, x, 0.0), jnp.where(lo, 0.0, x)], axis=0)
    twice = lambda x: jnp.concatenate([x, x], axis=0)
    pick = lambda x: jnp.where(lo, x[:C], x[C:])

    outs = []
    for p in range(N_PAIRS):
        sl = slice(p * PAIR, (p + 1) * PAIR)
        lhs = jnp.concatenate([by_head(Kg[:, sl]), by_head(Rg[:, sl])], axis=0)
        rhs = jnp.concatenate([by_head(Bi[:, sl]), by_head(Ki[:, sl])], axis=0)
        G = _mm(lhs, rhs, NT, RWKV_A_PASSES)
        Akb = jnp.where(strict, G[:PAIR, :PAIR], 0.0)
        Akk = jnp.where(strict, G[:PAIR, PAIR:], 0.0)
        Arb = jnp.where(incl, G[PAIR:, :PAIR], 0.0)
        Ark = jnp.where(incl, G[PAIR:, PAIR:], 0.0)
        Tinv = _unit_lower_inverse(Akb, RWKV_INV_PASSES)

        S = S_s[p]
        H0 = _mm(jnp.concatenate([Kg[:, sl], Rg[:, sl]], axis=0), S, NT, RWKV_APPLY_PASSES)
        V = v[:, sl]
        VV = twice(V)
        rhs_u = twice(H0[:C]) + _mm(Akk, VV, NN, RWKV_APPLY_PASSES)
        U = pick(_mm(Tinv, rhs_u, NN, RWKV_APPLY_PASSES))
        O = pick(twice(H0[C:]) + _mm(jnp.concatenate([Ark, -Arb], axis=1),
                                     jnp.concatenate([VV, twice(U)], axis=0), NN, RWKV_APPLY_PASSES))
        Z = _mm(jnp.concatenate([V, U], axis=0).T,
                jnp.concatenate([Kt[:, sl], -Bt[:, sl]], axis=0), NN, RWKV_APPLY_PASSES)
        S_s[p] = S * e_last[:, sl] + jnp.where(same_head, Z, 0.0)
        outs.append(O)

    out = jnp.concatenate(outs, axis=1)
    o_ref[0] = _rwkv_post(out, r, k, v, g, r_k[...], ln_w[...], ln_b[...])

    @pl.when(j == pl.num_programs(1) - 1)
    def _():
        _pair_state_store(S_s, sT_ref, transpose=False)


def _hgrn_seq_kernel(ph_ref, s0_ref, lb, norm_g, o_ref, sT_ref, S_s, o_s):
    j = pl.program_id(1)
    C = CHUNK

    @pl.when(j == 0)
    def _():
        _pair_state_load(s0_ref, S_s, transpose=True)

    ph = ph_ref[0]
    q = ph[:, :C_HEADS]
    fpre = ph[:, C_HEADS:2 * C_HEADS]
    iv = ph[:, 2 * C_HEADS:3 * C_HEADS]
    f, kin = _hgrn_pre(fpre, lb[...])
    logf = jnp.log(f)
    tr, tc = _iota2((C, C))
    sub_tri = ((tr >= tc) & ((tr >> 4) == (tc >> 4))).astype(F32)
    cum = _mm(sub_tri, logf, NN, 6)
    qd = q * jnp.exp(cum)

    lo = lax.broadcasted_iota(jnp.int32, (SUB, PAIR), 1) < HEAD_DIM
    trow = lax.broadcasted_iota(jnp.int32, (SUB, PAIR), 0)
    pr_, pc_ = _iota2((PAIR, PAIR))
    same_head = (pr_ >= HEAD_DIM) == (pc_ >= HEAD_DIM)

    for i in range(C // SUB):
        rows = slice(i * SUB, (i + 1) * SUB)
        cum_i = cum[rows]
        cum_last = cum_i[SUB - 1:SUB]
        kd_i = kin[rows] * jnp.exp(cum_last - cum_i)
        e_last = jnp.exp(cum_last)
        q_i, kin_i, iv_i, qd_i = q[rows], kin[rows], iv[rows], qd[rows]
        for p in range(N_PAIRS):
            sl = slice(p * PAIR, (p + 1) * PAIR)
            S = S_s[p]
            o = _mm(qd_i[:, sl], S, NT, 1)
            cp, qp, kp, vp = cum_i[:, sl], q_i[:, sl], kin_i[:, sl], iv_i[:, sl]
            for s in range(SUB):
                causal = trow >= s
                d = jnp.where(causal, cp - cp[s:s + 1], 0.0)
                w = jnp.where(causal, qp * kp[s:s + 1] * jnp.exp(d), 0.0)
                a_lo = jnp.sum(jnp.where(lo, w, 0.0), axis=1, keepdims=True)
                a_hi = jnp.sum(jnp.where(lo, 0.0, w), axis=1, keepdims=True)
                o = o + jnp.where(lo, a_lo, a_hi) * vp[s:s + 1]
            Z = _mm(vp.T, kd_i[:, sl], NN, 1)
            S_s[p] = S * e_last[:, sl] + jnp.where(same_head, Z, 0.0)
            o_s[rows, sl] = o

    o_ref[0] = _hgrn_post(o_s[...], ph[:, 3 * C_HEADS:], norm_g[...])

    @pl.when(j == pl.num_programs(1) - 1)
    def _():
        _pair_state_store(S_s, sT_ref, transpose=True)


def _lru_seq_kernel(pl_ref, c0_ref, h0_ref, conv_w, conv_b, wa, ba, wx, bx, lam,
                    o_ref, hT_ref, xbuf_s, h_s, a_s, u_s, *, tm):
    j = pl.program_id(1)

    @pl.when(j == 0)
    def _():
        xbuf_s[5:8, :] = c0_ref[0]
        h_s[...] = h0_ref[0]

    blk = pl_ref[0]
    xb = blk[:, :C_LRU]
    xbuf_s[8:8 + tm, :] = xb
    x1 = xbuf_s[7:7 + tm, :]
    x2 = xbuf_s[6:6 + tm, :]
    x3 = xbuf_s[5:5 + tm, :]
    a, u = _lru_pre(xb, x1, x2, x3, conv_w[...], conv_b[...], wa[...], ba[...], wx[...], bx[...], lam[...])
    xbuf_s[5:8, :] = xb[tm - 3:tm, :]

    row8 = lax.broadcasted_iota(jnp.int32, a.shape, 0) & 7
    for d in (1, 2, 4):
        ok = row8 >= d
        u = jnp.where(ok, a * pltpu.roll(u, d, 0) + u, u)
        a = jnp.where(ok, a * pltpu.roll(a, d, 0), a)
    a_s[...] = a
    u_s[...] = u

    def group(gi, h):
        i = pl.multiple_of(gi * 8, 8)
        ht = u_s[pl.ds(i, 8), :] + a_s[pl.ds(i, 8), :] * h
        o_ref[0, pl.ds(i, 8), :] = ht
        return ht[7:8, :]

    h = lax.fori_loop(0, tm // 8, group, h_s[...])
    h_s[...] = h
    o_ref[0] = o_ref[0] * jax.nn.gelu(blk[:, C_LRU:])

    @pl.when(j == pl.num_programs(1) - 1)
    def _():
        hT_ref[0] = h


def _full_spec(a):
    return pl.BlockSpec(a.shape, lambda b, j: (0,) * a.ndim)


def _rwkv_seq(pr, p0, s0, prm, B, T):
    seq = lambda c: pl.BlockSpec((1, CHUNK, c), lambda b, j: (b, j, 0))
    sblk = pl.BlockSpec((1, N_HEADS, HEAD_DIM, HEAD_DIM), lambda b, j: (b, 0, 0, 0))
    o, sT = pl.pallas_call(
        _rwkv_seq_kernel,
        grid=(B, T // CHUNK),
        in_specs=[seq(RWKV_COLS), pl.BlockSpec((1, 1, RWKV_COLS), lambda b, j: (b, 0, 0)), sblk]
                 + [_full_spec(a) for a in prm],
        out_specs=[seq(C_HEADS), sblk],
        out_shape=[jax.ShapeDtypeStruct((B, T, C_HEADS), F32), jax.ShapeDtypeStruct(s0.shape, F32)],
        scratch_shapes=[pltpu.VMEM((1, RWKV_COLS), F32), pltpu.VMEM((N_PAIRS, PAIR, PAIR), F32)],
        compiler_params=_cparams("parallel", "arbitrary"),
        name="rwkv_seq",
    )(pr.reshape(B, T, RWKV_COLS), p0.reshape(B, 1, RWKV_COLS), s0, *prm)
    return o.reshape(B * T, C_HEADS), sT


def _hgrn_seq(ph, s0, lb, norm_g, B, T):
    seq = lambda c: pl.BlockSpec((1, CHUNK, c), lambda b, j: (b, j, 0))
    sblk = pl.BlockSpec((1, N_HEADS, HEAD_DIM, HEAD_DIM), lambda b, j: (b, 0, 0, 0))
    o, sT = pl.pallas_call(
        _hgrn_seq_kernel,
        grid=(B, T // CHUNK),
        in_specs=[seq(HGRN_COLS), sblk, _full_spec(lb), _full_spec(norm_g)],
        out_specs=[seq(C_HEADS), sblk],
        out_shape=[jax.ShapeDtypeStruct((B, T, C_HEADS), F32), jax.ShapeDtypeStruct(s0.shape, F32)],
        scratch_shapes=[pltpu.VMEM((N_PAIRS, PAIR, PAIR), F32), pltpu.VMEM((CHUNK, C_HEADS), F32)],
        compiler_params=_cparams("parallel", "arbitrary"),
        name="hgrn_seq",
    )(ph.reshape(B, T, HGRN_COLS), s0, lb, norm_g)
    return o.reshape(B * T, C_HEADS), sT


def _lru_seq(plru, conv0, h0, prm, B, T):
    tm = min(T, 256)
    seq = lambda c: pl.BlockSpec((1, tm, c), lambda b, j: (b, j, 0))
    hblk = pl.BlockSpec((1, 1, C_LRU), lambda b, j: (b, 0, 0))
    o, hT = pl.pallas_call(
        functools.partial(_lru_seq_kernel, tm=tm),
        grid=(B, T // tm),
        in_specs=[seq(LRU_COLS), pl.BlockSpec((1, CONV_WIDTH - 1, C_LRU), lambda b, j: (b, 0, 0)), hblk]
                 + [_full_spec(a) for a in prm],
        out_specs=[seq(C_LRU), hblk],
        out_shape=[jax.ShapeDtypeStruct((B, T, C_LRU), F32), jax.ShapeDtypeStruct((B, 1, C_LRU), F32)],
        scratch_shapes=[pltpu.VMEM((tm + 8, C_LRU), F32), pltpu.VMEM((1, C_LRU), F32),
                        pltpu.VMEM((tm, C_LRU), F32), pltpu.VMEM((tm, C_LRU), F32)],
        compiler_params=_cparams("parallel", "arbitrary"),
        name="lru_seq",
    )(plru.reshape(B, T, LRU_COLS), conv0, h0.reshape(B, 1, C_LRU), *prm)
    return o.reshape(B * T, C_LRU), hT.reshape(B, C_LRU)


def _block_diag(w):
    out = jnp.zeros((C_LRU, C_LRU), w.dtype)
    n = C_LRU // LRU_BLOCKS
    for i in range(LRU_BLOCKS):
        out = out.at[i * n:(i + 1) * n, i * n:(i + 1) * n].set(w[i])
    return out


def _prepare_params(p):
    row = lambda a: a.reshape(1, -1)
    zpad = jnp.zeros((W_LORA, C_HEADS), F32)
    layers = []
    for l in range(DEPTH):
        pre = (row(p['mu_shift'][l]), row(p['rwkv_w0'][l]),
               jnp.concatenate([p['rwkv_w_up'][l], zpad], 0).astype(BF16),
               row(p['rwkv_a0'][l]),
               jnp.concatenate([zpad, p['rwkv_a_up'][l]], 0).astype(BF16),
               p['rwkv_g_up'][l].astype(BF16), row(p['rwkv_k_k'][l]), row(p['rwkv_k_a'][l]),
               None,
               p['lru_conv_w'][l], row(p['lru_conv_b'][l]),
               _block_diag(p['lru_wa'][l]).astype(BF16), row(p['lru_ba'][l]),
               _block_diag(p['lru_wx'][l]).astype(BF16), row(p['lru_bx'][l]), row(p['lru_lambda'][l]))
        layers.append(dict(
            norm1_g=row(p['norm1_g'][l]), w_in=p['w_in'][l].astype(BF16), pre=pre,
            r_k=row(p['rwkv_r_k'][l]), ln_w=row(p['rwkv_ln_w'][l]), ln_b=row(p['rwkv_ln_b'][l]),
            hgrn_norm_g=row(p['hgrn_norm_g'][l]),
            w_out=p['w_out'][l].astype(BF16), norm2_g=row(p['norm2_g'][l]),
            mlp_w1=p['mlp_w1'][l].astype(BF16), mlp_w2=p['mlp_w2'][l].astype(BF16)))
    return layers


def _trunk(x, wkv, shift, hgrn, lru, conv, layers, lb_all, final_g):
    B, T, _ = x.shape
    m = B * T
    assert T == 1 or T % CHUNK == 0, "sequence kernels need T to be a multiple of CHUNK"
    x = x.reshape(m, D_MODEL)
    heads = lambda a: a.reshape(B, T, -1, N_HEADS, HEAD_DIM)
    n_wkv, n_shift, n_hgrn, n_lru, n_conv = [], [], [], [], []
    for l, lp in enumerate(layers):
        xn, pr, ph, plru = _norm_proj(x, lp['norm1_g'], lp['w_in'])
        p0 = _proj(shift[l], lp['w_in'][:, :RWKV_COLS])
        prm = list(lp['pre'])
        prm[8] = lb_all[l:l + 1]
        if T == 1:
            r, w, k, v, kk, ka, g, f, kin, a, u = _mixer_pre_step(pr, ph, plru, p0, conv[l], prm)
            wo, S_r = _head_scan(_rwkv_scan_kernel, "rwkv_scan",
                                 [(heads(t), 0) for t in (r, w, k, v, kk, ka)], wkv[l], B, T)
            ph4 = heads(ph)
            ho, S_h = _head_scan(_hgrn_scan_kernel, "hgrn_scan",
                                 [(ph4, 0), (heads(f), 0), (heads(kin), 0), (ph4, 2)], hgrn[l], B, T)
            lh, h_l = _lru_scan(a, u, lru[l], B, T)
            mix = _mix_post(wo.reshape(m, C_HEADS), r, k, v, g, lp['r_k'], lp['ln_w'], lp['ln_b'],
                            ho.reshape(m, C_HEADS), ph, lp['hgrn_norm_g'], lh, plru)
            o_r, o_h, o_l = mix[:, :C_HEADS], mix[:, C_HEADS:2 * C_HEADS], mix[:, 2 * C_HEADS:]
        else:
            o_r, S_r = _rwkv_seq(pr, p0, wkv[l], prm[:8] + [lp['r_k'], lp['ln_w'], lp['ln_b']], B, T)
            o_h, S_h = _hgrn_seq(ph, hgrn[l], prm[8], lp['hgrn_norm_g'], B, T)
            o_l, h_l = _lru_seq(plru, conv[l], lru[l], prm[9:], B, T)
        x = _out_mlp(x, o_r, o_h, o_l, lp['w_out'], lp['norm2_g'], lp['mlp_w1'], lp['mlp_w2'])
        xb = plru.reshape(B, T, LRU_COLS)[:, :, :C_LRU]
        hist = jnp.concatenate([conv[l], xb[:, max(T - (CONV_WIDTH - 1), 0):]], axis=1)
        n_wkv.append(S_r)
        n_shift.append(xn.reshape(B, T, D_MODEL)[:, -1])
        n_hgrn.append(S_h)
        n_lru.append(h_l)
        n_conv.append(hist[:, -(CONV_WIDTH - 1):])
    y = _final_norm(x, final_g).reshape(B, T, D_MODEL)
    return (y, jnp.stack(n_wkv), jnp.stack(n_shift), jnp.stack(n_hgrn), jnp.stack(n_lru), jnp.stack(n_conv))


def kernel(x_prompt, x_sample, state_wkv, state_shift, state_hgrn, state_lru, state_conv, norm1_g, w_in, mu_shift, rwkv_w0, rwkv_w_up, rwkv_a0, rwkv_a_up, rwkv_g_up, rwkv_k_k, rwkv_k_a, rwkv_r_k, rwkv_ln_w, rwkv_ln_b, hgrn_lb, hgrn_norm_g, lru_conv_w, lru_conv_b, lru_wa, lru_ba, lru_wx, lru_bx, lru_lambda, w_out, norm2_g, mlp_w1, mlp_w2, final_g):
    prm = dict(norm1_g=norm1_g, w_in=w_in, mu_shift=mu_shift, rwkv_w0=rwkv_w0, rwkv_w_up=rwkv_w_up,
               rwkv_a0=rwkv_a0, rwkv_a_up=rwkv_a_up, rwkv_g_up=rwkv_g_up, rwkv_k_k=rwkv_k_k,
               rwkv_k_a=rwkv_k_a, rwkv_r_k=rwkv_r_k.reshape(DEPTH, C_HEADS), rwkv_ln_w=rwkv_ln_w,
               rwkv_ln_b=rwkv_ln_b, hgrn_norm_g=hgrn_norm_g, lru_conv_w=lru_conv_w,
               lru_conv_b=lru_conv_b, lru_wa=lru_wa, lru_ba=lru_ba, lru_wx=lru_wx, lru_bx=lru_bx,
               lru_lambda=lru_lambda, w_out=w_out, norm2_g=norm2_g, mlp_w1=mlp_w1, mlp_w2=mlp_w2)
    layers = _prepare_params(prm)
    lb_all = _hgrn_lower_bounds(hgrn_lb)
    fg = final_g.reshape(1, D_MODEL)
    Bp = x_prompt.shape[0]
    dt = x_prompt.dtype
    z_wkv = jnp.zeros((DEPTH, Bp, N_HEADS, HEAD_DIM, HEAD_DIM), dt)
    z_shift = jnp.zeros((DEPTH, Bp, D_MODEL), dt)
    z_hgrn = jnp.zeros((DEPTH, Bp, N_HEADS, HEAD_DIM, HEAD_DIM), dt)
    z_lru = jnp.zeros((DEPTH, Bp, C_LRU), dt)
    z_conv = jnp.zeros((DEPTH, Bp, CONV_WIDTH - 1, C_LRU), dt)
    y_p, p_wkv, p_shift, p_hgrn, p_lru, p_conv = _trunk(x_prompt, z_wkv, z_shift, z_hgrn, z_lru, z_conv,
                                                         layers, lb_all, fg)
    y_s, s_wkv, s_shift, s_hgrn, s_lru, s_conv = _trunk(x_sample, state_wkv, state_shift, state_hgrn,
                                                         state_lru, state_conv, layers, lb_all, fg)
    return (y_p, y_s, p_wkv, p_shift, p_hgrn, p_lru, p_conv, s_wkv, s_shift, s_hgrn, s_lru, s_conv)
```

```python
import functools

import jax
import jax.numpy as jnp
from jax import lax
from jax.experimental import pallas as pl
from jax.experimental.pallas import tpu as pltpu

F32 = jnp.float32
BF16 = jnp.bfloat16

D_MODEL = 1024
DEPTH = 4
HEAD_DIM = 64
N_HEADS = 6
C_HEADS = N_HEADS * HEAD_DIM
C_LRU = 256
LRU_BLOCKS = 4
CONV_WIDTH = 4
LRU_C = 8.0
W_LORA = 64
A_LORA = 64
G_LORA = 128
RWKV_COLS = 3 * C_HEADS + W_LORA + A_LORA + G_LORA
HGRN_COLS = 4 * C_HEADS
LRU_COLS = 2 * C_LRU
C_IN = RWKV_COLS + HGRN_COLS + LRU_COLS
D_FF = 4 * D_MODEL
NORM_EPS = 1e-6
GN_EPS = 64e-5

LANES = 128
VMEM_LIMIT_BYTES = 48 * 1024 * 1024


def _cparams(*sem):
    return pltpu.CompilerParams(dimension_semantics=sem, vmem_limit_bytes=VMEM_LIMIT_BYTES)


def _dot(a, b):
    return jnp.dot(a.astype(BF16), b.astype(BF16), preferred_element_type=F32)


def _rmsnorm(x, g):
    return x * lax.rsqrt(jnp.mean(x * x, axis=-1, keepdims=True) + NORM_EPS) * g


def _softplus(x):
    return jnp.maximum(x, 0.0) + jnp.log1p(jnp.exp(-jnp.abs(x)))


def _head_sum(x):
    m = x.shape[0]
    lo = lax.broadcasted_iota(jnp.int32, (m, LANES), 1) < HEAD_DIM
    outs = []
    for p in range(x.shape[1] // LANES):
        xp = x[:, p * LANES:(p + 1) * LANES]
        s_lo = jnp.sum(jnp.where(lo, xp, 0.0), axis=1, keepdims=True)
        s_hi = jnp.sum(jnp.where(lo, 0.0, xp), axis=1, keepdims=True)
        outs.append(jnp.where(lo, s_lo, s_hi))
    return jnp.concatenate(outs, axis=1)


def _norm_proj_kernel(x_ref, g_ref, w_ref, xn_ref, pr_ref, ph_ref, pl_ref):
    xn = _rmsnorm(x_ref[...], g_ref[...])
    xn_ref[...] = xn
    p = jnp.dot(xn.astype(BF16), w_ref[...], preferred_element_type=F32)
    pr_ref[...] = p[:, :RWKV_COLS]
    ph_ref[...] = p[:, RWKV_COLS:RWKV_COLS + HGRN_COLS]
    pl_ref[...] = p[:, RWKV_COLS + HGRN_COLS:]


def _norm_proj(x, g, w):
    m = x.shape[0]
    tm = min(m, 256)
    row = lambda c: pl.BlockSpec((tm, c), lambda i: (i, 0))
    full = lambda a: pl.BlockSpec(a.shape, lambda i: (0,) * a.ndim)
    return pl.pallas_call(
        _norm_proj_kernel,
        grid=(m // tm,),
        in_specs=[row(D_MODEL), full(g), full(w)],
        out_specs=[row(D_MODEL), row(RWKV_COLS), row(HGRN_COLS), row(LRU_COLS)],
        out_shape=[jax.ShapeDtypeStruct((m, c), F32) for c in (D_MODEL, RWKV_COLS, HGRN_COLS, LRU_COLS)],
        compiler_params=_cparams("parallel"),
        name="norm_proj",
    )(x, g, w)


def _proj_kernel(x_ref, w_ref, o_ref):
    o_ref[...] = jnp.dot(x_ref[...].astype(BF16), w_ref[...], preferred_element_type=F32)


def _proj(x, w):
    return pl.pallas_call(
        _proj_kernel,
        out_shape=jax.ShapeDtypeStruct((x.shape[0], w.shape[1]), F32),
        compiler_params=pltpu.CompilerParams(vmem_limit_bytes=VMEM_LIMIT_BYTES),
        name="shift_proj",
    )(x, w)


def _out_mlp_kernel(x_ref, or_ref, oh_ref, ol_ref, wo_ref, g2_ref, w1_ref, w2_ref, o_ref, x1_s, xn_s, acc_s):
    j = pl.program_id(1)

    @pl.when(j == 0)
    def _():
        mix = jnp.concatenate([or_ref[...], oh_ref[...], ol_ref[...]], axis=1)
        x1 = x_ref[...] + jnp.dot(mix.astype(BF16), wo_ref[...], preferred_element_type=F32)
        x1_s[...] = x1
        xn_s[...] = _rmsnorm(x1, g2_ref[...]).astype(BF16)
        acc_s[...] = jnp.zeros_like(acc_s)

    h = jnp.dot(xn_s[...], w1_ref[...], preferred_element_type=F32)
    h = jnp.square(jnp.maximum(h, 0.0))
    acc_s[...] += jnp.dot(h.astype(BF16), w2_ref[...], preferred_element_type=F32)

    @pl.when(j == pl.num_programs(1) - 1)
    def _():
        o_ref[...] = x1_s[...] + acc_s[...]


def _out_mlp(x, o_r, o_h, o_l, wo, g2, w1, w2):
    m = x.shape[0]
    tm = min(m, 512)
    tf = 1024
    row = pl.BlockSpec((tm, D_MODEL), lambda i, j: (i, 0))
    part = lambda c: pl.BlockSpec((tm, c), lambda i, j: (i, 0))
    return pl.pallas_call(
        _out_mlp_kernel,
        grid=(m // tm, D_FF // tf),
        in_specs=[row, part(C_HEADS), part(C_HEADS), part(C_LRU),
                  pl.BlockSpec((D_MODEL, D_MODEL), lambda i, j: (0, 0)),
                  pl.BlockSpec((1, D_MODEL), lambda i, j: (0, 0)),
                  pl.BlockSpec((D_MODEL, tf), lambda i, j: (0, j)),
                  pl.BlockSpec((tf, D_MODEL), lambda i, j: (j, 0))],
        out_specs=row,
        out_shape=jax.ShapeDtypeStruct((m, D_MODEL), F32),
        scratch_shapes=[pltpu.VMEM((tm, D_MODEL), F32), pltpu.VMEM((tm, D_MODEL), BF16),
                        pltpu.VMEM((tm, D_MODEL), F32)],
        compiler_params=_cparams("parallel", "arbitrary"),
        name="out_mlp",
    )(x, o_r, o_h, o_l, wo, g2, w1, w2)


def _final_norm_kernel(x_ref, g_ref, o_ref):
    o_ref[...] = _rmsnorm(x_ref[...], g_ref[...])


def _final_norm(x, g):
    m = x.shape[0]
    tm = min(m, 512)
    row = pl.BlockSpec((tm, D_MODEL), lambda i: (i, 0))
    return pl.pallas_call(
        _final_norm_kernel,
        grid=(m // tm,),
        in_specs=[row, pl.BlockSpec((1, D_MODEL), lambda i: (0, 0))],
        out_specs=row,
        out_shape=jax.ShapeDtypeStruct((m, D_MODEL), F32),
        compiler_params=_cparams("parallel"),
        name="final_norm",
    )(x, g)


def _lb_kernel(p_ref, o_ref):
    p = p_ref[...]
    e = jnp.exp(p - jnp.max(p, axis=0, keepdims=True))
    s = e / jnp.sum(e, axis=0, keepdims=True)
    acc = jnp.zeros_like(s[0:1])
    for l in range(DEPTH):
        if l > 0:
            acc = acc + s[l:l + 1]
        o_ref[l:l + 1, :] = acc


def _hgrn_lower_bounds(hgrn_lb):
    return pl.pallas_call(_lb_kernel, out_shape=jax.ShapeDtypeStruct(hgrn_lb.shape, F32),
                          name="hgrn_lower_bounds")(hgrn_lb)


def _rwkv_pre(pr, prev, mu, w0, w_up, a0, a_up, g_up, k_k, k_a):
    c = C_HEADS
    pm = pr + mu * (prev - pr)
    r = pm[:, :c]
    k0 = pm[:, c:2 * c]
    v = pm[:, 2 * c:3 * c]
    xwa = pm[:, 3 * c:3 * c + W_LORA + A_LORA]
    xg = pm[:, 3 * c + W_LORA + A_LORA:]
    w_log = -_softplus(-(w0 + _dot(jnp.tanh(xwa), w_up))) - 0.5
    log_decay = -jnp.exp(w_log)
    a = jax.nn.sigmoid(a0 + _dot(xwa, a_up))
    g = _dot(jax.nn.sigmoid(xg), g_up)
    kk = k0 * k_k
    kk = kk * lax.rsqrt(jnp.maximum(_head_sum(kk * kk), 1e-24))
    k = k0 * (1.0 + (a - 1.0) * k_a)
    return r, log_decay, k, v, kk, kk * a, g


def _hgrn_pre(fpre, lb):
    sg = jax.nn.sigmoid(fpre)
    f = lb + (1.0 - lb) * sg
    kin = (1.0 - lb) * jax.nn.sigmoid(-fpre)
    return f, kin


def _lru_pre(xb, x1, x2, x3, conv_w, conv_b, wa, ba, wx, bx, lam):
    xc = conv_b + (((x3 * conv_w[0:1] + x2 * conv_w[1:2]) + x1 * conv_w[2:3]) + xb * conv_w[3:4])
    r = jax.nn.sigmoid(_dot(xc, wa) + ba)
    i = jax.nn.sigmoid(_dot(xc, wx) + bx)
    log_a = -LRU_C * r * _softplus(-lam)
    a = jnp.exp(log_a)
    one_minus_a2 = -jnp.tanh(log_a) * (jnp.exp(2.0 * log_a) + 1.0)
    u = xc * i * jnp.sqrt(jnp.maximum(one_minus_a2, 1e-12))
    return a, u


_N_PRE_PARAMS = 16


def _pre_compute(pr, prev, ph, xb, x1, x2, x3, prm, outs):
    (mu, w0, w_up, a0, a_up, g_up, k_k, k_a, lb, conv_w, conv_b, wa, ba, wx, bx, lam) = prm
    r_o, w_o, k_o, v_o, kk_o, ka_o, g_o, f_o, kin_o, a_o, u_o = outs
    r, lw, k, v, kk, ka, g = _rwkv_pre(pr, prev, mu[...], w0[...], w_up[...], a0[...], a_up[...],
                                       g_up[...], k_k[...], k_a[...])
    w = jnp.exp(lw)
    f, kin = _hgrn_pre(ph[:, C_HEADS:2 * C_HEADS], lb[...])
    a, u = _lru_pre(xb, x1, x2, x3, conv_w[...], conv_b[...], wa[...], ba[...], wx[...], bx[...], lam[...])
    return ((r_o, r), (w_o, w), (k_o, k), (v_o, v), (kk_o, kk), (ka_o, ka), (g_o, g),
            (f_o, f), (kin_o, kin), (a_o, a), (u_o, u))


def _pre_step_kernel(*refs):
    pr_ref, ph_ref, pl_ref, p0_ref, c0_ref, c1_ref, c2_ref = refs[:7]
    prm = refs[7:7 + _N_PRE_PARAMS]
    outs = refs[7 + _N_PRE_PARAMS:]
    xb = pl_ref[...][:, :C_LRU]
    res = _pre_compute(pr_ref[...], p0_ref[...], ph_ref[...], xb, c2_ref[...], c1_ref[...], c0_ref[...],
                       prm, outs)
    for o_ref, val in res:
        o_ref[...] = val


def _mixer_pre_step(pr, ph, plru, p0, conv0, prm):
    widths = (C_HEADS,) * 9 + (C_LRU,) * 2
    return pl.pallas_call(
        _pre_step_kernel,
        out_shape=[jax.ShapeDtypeStruct((pr.shape[0], c), F32) for c in widths],
        compiler_params=pltpu.CompilerParams(vmem_limit_bytes=VMEM_LIMIT_BYTES),
        name="mixer_pre_step",
    )(pr, ph, plru, p0, conv0[:, 0], conv0[:, 1], conv0[:, 2], *prm)


def _eye():
    return (lax.broadcasted_iota(jnp.int32, (HEAD_DIM, HEAD_DIM), 0)
            == lax.broadcasted_iota(jnp.int32, (HEAD_DIM, HEAD_DIM), 1))


def _to_col(row, eye):
    return jnp.sum(jnp.where(eye, row, 0.0), axis=1, keepdims=True)


def _to_row(col, eye):
    return jnp.sum(jnp.where(eye, col, 0.0), axis=0, keepdims=True)


def _rwkv_scan_kernel(r_ref, w_ref, k_ref, v_ref, kk_ref, ka_ref, s0_ref, o_ref, sT_ref, S_s, *, bb, tc):
    j = pl.program_id(1)

    @pl.when(j == 0)
    def _():
        S_s[...] = s0_ref[...]

    eye = _eye()

    def step(t, carry):
        for b in range(bb):
            for h in range(N_HEADS):
                idx = (b, t, pl.ds(h, 1), slice(None))
                S = S_s[b, h]
                kk = kk_ref[idx]
                sa = -jnp.sum(S * kk, axis=1, keepdims=True)
                S = S * w_ref[idx] + sa * ka_ref[idx] + _to_col(v_ref[idx], eye) * k_ref[idx]
                S_s[b, h] = S
                o_ref[idx] = _to_row(jnp.sum(S * r_ref[idx], axis=1, keepdims=True), eye)
        return carry

    lax.fori_loop(0, tc, step, 0)

    @pl.when(j == pl.num_programs(1) - 1)
    def _():
        sT_ref[...] = S_s[...]


def _hgrn_scan_kernel(q_ref, f_ref, kin_ref, iv_ref, s0_ref, o_ref, sT_ref, S_s, *, bb, tc):
    j = pl.program_id(1)

    @pl.when(j == 0)
    def _():
        S_s[...] = s0_ref[...]

    eye = _eye()

    def step(t, carry):
        for b in range(bb):
            for h in range(N_HEADS):
                idx = (b, t, pl.ds(h, 1), slice(None))
                S = S_s[b, h]
                S = _to_col(f_ref[idx], eye) * S + _to_col(kin_ref[idx], eye) * iv_ref[idx]
                S_s[b, h] = S
                o_ref[idx] = jnp.sum(S * _to_col(q_ref[idx], eye), axis=0, keepdims=True)
        return carry

    lax.fori_loop(0, tc, step, 0)

    @pl.when(j == pl.num_programs(1) - 1)
    def _():
        sT_ref[...] = S_s[...]


def _lru_scan_kernel(a_ref, u_ref, h0_ref, o_ref, hT_ref, h_s, *, bb, tc):
    j = pl.program_id(1)

    @pl.when(j == 0)
    def _():
        h_s[...] = h0_ref[...]

    def step(t, carry):
        for b in range(bb):
            idx = (b, pl.ds(t, 1), slice(None))
            h = a_ref[idx] * h_s[b] + u_ref[idx]
            h_s[b] = h
            o_ref[idx] = h
        return carry

    lax.fori_loop(0, tc, step, 0)

    @pl.when(j == pl.num_programs(1) - 1)
    def _():
        hT_ref[...] = h_s[...]


def _scan_tiles(B, T):
    bb = 8 if T == 1 else 4
    tc = min(T, 32)
    return bb, tc


def _head_scan(kernel, name, seqs, s0, B, T):
    bb, tc = _scan_tiles(B, T)
    hblk = lambda g: pl.BlockSpec((bb, tc, None, N_HEADS, HEAD_DIM), lambda i, j: (i, j, g, 0, 0))
    sblk = pl.BlockSpec((bb, N_HEADS, HEAD_DIM, HEAD_DIM), lambda i, j: (i, 0, 0, 0))
    return pl.pallas_call(
        functools.partial(kernel, bb=bb, tc=tc),
        grid=(B // bb, T // tc),
        in_specs=[hblk(g) for _, g in seqs] + [sblk],
        out_specs=[hblk(0), sblk],
        out_shape=[jax.ShapeDtypeStruct((B, T, 1, N_HEADS, HEAD_DIM), F32),
                   jax.ShapeDtypeStruct(s0.shape, F32)],
        scratch_shapes=[pltpu.VMEM((bb, N_HEADS, HEAD_DIM, HEAD_DIM), F32)],
        compiler_params=_cparams("parallel", "arbitrary"),
        name=name,
    )(*[a for a, _ in seqs], s0)


def _lru_scan(a, u, h0, B, T):
    bb, tc = _scan_tiles(B, T)
    blk = pl.BlockSpec((bb, tc, C_LRU), lambda i, j: (i, j, 0))
    hblk = pl.BlockSpec((bb, 1, C_LRU), lambda i, j: (i, 0, 0))
    o, hT = pl.pallas_call(
        functools.partial(_lru_scan_kernel, bb=bb, tc=tc),
        grid=(B // bb, T // tc),
        in_specs=[blk, blk, hblk],
        out_specs=[blk, hblk],
        out_shape=[jax.ShapeDtypeStruct((B, T, C_LRU), F32), jax.ShapeDtypeStruct((B, 1, C_LRU), F32)],
        scratch_shapes=[pltpu.VMEM((bb, 1, C_LRU), F32)],
        compiler_params=_cparams("parallel", "arbitrary"),
        name="lru_scan",
    )(a.reshape(B, T, C_LRU), u.reshape(B, T, C_LRU), h0.reshape(B, 1, C_LRU))
    return o.reshape(B * T, C_LRU), hT.reshape(B, C_LRU)


def _rwkv_post(out, r, k, v, g, r_k, ln_w, ln_b):
    inv_n = 1.0 / HEAD_DIM
    mean = _head_sum(out) * inv_n
    d = out - mean
    var = _head_sum(d * d) * inv_n
    gn = d * lax.rsqrt(var + GN_EPS) * ln_w + ln_b
    bonus = _head_sum(r * k * r_k) * v
    return (gn + bonus) * g


def _hgrn_post(o, gate, norm_g):
    o = o * lax.rsqrt(_head_sum(o * o) * (1.0 / HEAD_DIM) + NORM_EPS)
    return o * norm_g * jax.nn.silu(gate)


def _mix_post_kernel(wo_ref, r_ref, k_ref, v_ref, g_ref, rk_ref, lnw_ref, lnb_ref,
                     ho_ref, hg_ref, hng_ref, lh_ref, lg_ref, mix_ref):
    mix_ref[:, :C_HEADS] = _rwkv_post(wo_ref[...], r_ref[...], k_ref[...], v_ref[...], g_ref[...],
                                      rk_ref[...], lnw_ref[...], lnb_ref[...])
    mix_ref[:, C_HEADS:2 * C_HEADS] = _hgrn_post(ho_ref[...], hg_ref[...], hng_ref[...])
    mix_ref[:, 2 * C_HEADS:] = lh_ref[...] * jax.nn.gelu(lg_ref[...])


def _mix_post(wo, r, k, v, g, r_k, ln_w, ln_b, ho, ph, hng, lh, plru):
    m = wo.shape[0]
    tm = min(m, 512)
    hrow = pl.BlockSpec((tm, C_HEADS), lambda i: (i, 0))
    prow = lambda c: pl.BlockSpec((1, c), lambda i: (0, 0))
    return pl.pallas_call(
        _mix_post_kernel,
        grid=(m // tm,),
        in_specs=[hrow, hrow, hrow, hrow, hrow, prow(C_HEADS), prow(C_HEADS), prow(C_HEADS),
                  hrow, pl.BlockSpec((tm, C_HEADS), lambda i: (i, 3)), prow(C_HEADS),
                  pl.BlockSpec((tm, C_LRU), lambda i: (i, 0)),
                  pl.BlockSpec((tm, C_LRU), lambda i: (i, 1))],
        out_specs=pl.BlockSpec((tm, D_MODEL), lambda i: (i, 0)),
        out_shape=jax.ShapeDtypeStruct((m, D_MODEL), F32),
        compiler_params=_cparams("parallel"),
        name="mix_post",
    )(wo, r, k, v, g, r_k, ln_w, ln_b, ho, ph, hng, lh, plru)


CHUNK = 64
SUB = 16
PAIR = 2 * HEAD_DIM
N_PAIRS = N_HEADS // 2

NN = ((1,), (0,))
NT = ((1,), (1,))


def _mm(a, b, dims, passes):
    dn = (dims, ((), ()))
    if passes == 6:
        return lax.dot_general(a, b, dn, precision=lax.Precision.HIGHEST, preferred_element_type=F32)
    ah, bh = a.astype(BF16), b.astype(BF16)
    out = lax.dot_general(ah, bh, dn, preferred_element_type=F32)
    if passes == 3:
        al = (a - ah.astype(F32)).astype(BF16)
        bl = (b - bh.astype(F32)).astype(BF16)
        out = (out + lax.dot_general(ah, bl, dn, preferred_element_type=F32)
               + lax.dot_general(al, bh, dn, preferred_element_type=F32))
    return out


def _iota2(shape):
    return lax.broadcasted_iota(jnp.int32, shape, 0), lax.broadcasted_iota(jnp.int32, shape, 1)


def _pair_state_load(s_ref, S_s, transpose):
    z = jnp.zeros((HEAD_DIM, HEAD_DIM), F32)
    for p in range(N_PAIRS):
        a, b = s_ref[0, 2 * p], s_ref[0, 2 * p + 1]
        if transpose:
            a, b = a.T, b.T
        S_s[p] = jnp.concatenate([jnp.concatenate([a, z], axis=1), jnp.concatenate([z, b], axis=1)], axis=0)


def _pair_state_store(S_s, s_ref, transpose):
    for p in range(N_PAIRS):
        S = S_s[p]
        a, b = S[:HEAD_DIM, :HEAD_DIM], S[HEAD_DIM:, HEAD_DIM:]
        if transpose:
            a, b = a.T, b.T
        s_ref[0, 2 * p] = a
        s_ref[0, 2 * p + 1] = b


RWKV_SEQS = 4
GROUP_HEADS = 4
GROUP = GROUP_HEADS * HEAD_DIM
RWKV_GROUPS = RWKV_SEQS * N_HEADS // GROUP_HEADS


def _unit_lower_inverse(Ls):
    r, c = _iota2(Ls[0].shape)
    eye = (r == c).astype(F32)
    diag_blk = (r >> 4) == (c >> 4)
    mm = lambda xs, ys: [_mm(x, y, NN, 1) for x, y in zip(xs, ys)]
    Ld = [jnp.where(diag_blk, L, 0.0) for L in Ls]
    N = [jnp.where(diag_blk, 0.0, L) for L in Ls]
    L2 = mm(Ld, Ld)
    L4 = mm(L2, L2)
    P1 = mm([eye - x for x in Ld], [eye + x for x in L2])
    L8 = mm(L4, L4)
    P2 = mm([eye + x for x in L4], [eye + x for x in L8])
    Dinv = mm(P1, P2)
    M = mm(Dinv, N)
    M2 = mm(M, M)
    X = mm([eye - x for x in M], [eye + x for x in M2])
    return mm(X, Dinv)


def _rwkv_seq_kernel(pr_ref, p0_ref, s0_ref, mu, w0, w_up, a0, a_up, g_up, k_k, k_a, r_k, ln_w, ln_b,
                     o_ref, sT_ref, carry_s, S_s):
    j = pl.program_id(1)
    C, nb, nh = CHUNK, RWKV_SEQS, GROUP_HEADS
    head_of = lambda hw: (hw // N_HEADS, hw % N_HEADS)
    z64 = jnp.zeros((HEAD_DIM, HEAD_DIM), F32)

    @pl.when(j == 0)
    def _():
        carry_s[...] = p0_ref[...]
        for gi in range(RWKV_GROUPS):
            rows = []
            for i in range(nh):
                b, h = head_of(gi * nh + i)
                rows.append(jnp.concatenate([s0_ref[b, h] if i2 == i else z64 for i2 in range(nh)], axis=1))
            S_s[gi] = jnp.concatenate(rows, axis=0)

    pr = pr_ref[...].reshape(nb * C, RWKV_COLS)
    prow = lax.broadcasted_iota(jnp.int32, pr.shape, 0)
    prev = pltpu.roll(pr, 1, 0)
    for b in range(nb):
        prev = jnp.where(prow == b * C, carry_s[b], prev)
        carry_s[b] = pr[(b + 1) * C - 1:(b + 1) * C, :]

    r, lw, k, v, kk, kb, g = _rwkv_pre(pr, prev, mu[...], w0[...], w_up[...], a0[...], a_up[...],
                                       g_up[...], k_k[...], k_a[...])
    tr, tc = _iota2((nb * C, nb * C))
    tri = ((tr >= tc) & ((tr >> 6) == (tc >> 6))).astype(F32)
    cum = _mm(tri, lw, NN, 6)

    wide = lambda x: jnp.concatenate([x[b * C:(b + 1) * C] for b in range(nb)], axis=1)
    cum_w, lw_w = wide(cum), wide(lw)
    r_w, k_w, v_w, kk_w, kb_w = wide(r), wide(k), wide(v), wide(kk), wide(kb)
    cum_last = cum_w[C - 1:C, :]
    e_inv = jnp.exp(-cum_w)
    e_rel = jnp.exp(cum_last - cum_w)
    e_last = jnp.exp(cum_last)
    Kg = kk_w * jnp.exp(cum_w - lw_w)
    Rg = r_w * jnp.exp(cum_w)
    Ki, Bi = k_w * e_inv, kb_w * e_inv
    Kt, Bt = k_w * e_rel, kb_w * e_rel

    lane_head = lax.broadcasted_iota(jnp.int32, (C, GROUP), 1) >> 6
    gr, gc = _iota2((GROUP, GROUP))
    strict = (gr & (HEAD_DIM - 1)) > (gc & (HEAD_DIM - 1))
    incl = (gr & (HEAD_DIM - 1)) >= (gc & (HEAD_DIM - 1))
    same_head = (gr >> 6) == (gc >> 6)
    by_head = lambda x: jnp.concatenate([jnp.where(lane_head == i, x, 0.0) for i in range(nh)], axis=0)
    rep = lambda x: jnp.concatenate([x] * nh, axis=0)

    def pick(x):
        out = x[:C]
        for i in range(1, nh):
            out = jnp.where(lane_head == i, x[i * C:(i + 1) * C], out)
        return out

    groups = range(RWKV_GROUPS)
    sl = [slice(gi * GROUP, (gi + 1) * GROUP) for gi in groups]
    G = [_mm(jnp.concatenate([by_head(Kg[:, s]), by_head(Rg[:, s])], axis=0),
             jnp.concatenate([by_head(Bi[:, s]), by_head(Ki[:, s])], axis=0), NT, 1) for s in sl]
    Tinv = _unit_lower_inverse([jnp.where(strict, x[:GROUP, :GROUP], 0.0) for x in G])
    S = [S_s[gi] for gi in groups]
    H0 = [_mm(jnp.concatenate([Kg[:, s], Rg[:, s]], axis=0), S[gi], NT, 1) for gi, s in enumerate(sl)]
    VV = [rep(v_w[:, s]) for s in sl]
    rhs_u = [rep(H0[gi][:C]) + _mm(jnp.where(strict, G[gi][:GROUP, GROUP:], 0.0), VV[gi], NN, 1)
             for gi in groups]
    U = [pick(_mm(Tinv[gi], rhs_u[gi], NN, 1)) for gi in groups]
    O = [pick(rep(H0[gi][C:])
              + _mm(jnp.concatenate([jnp.where(incl, G[gi][GROUP:, GROUP:], 0.0),
                                     jnp.where(incl, -G[gi][GROUP:, :GROUP], 0.0)], axis=1),
                    jnp.concatenate([VV[gi], rep(U[gi])], axis=0), NN, 1)) for gi in groups]
    for gi, s in enumerate(sl):
        Z = _mm(jnp.concatenate([v_w[:, s], U[gi]], axis=0).T,
                jnp.concatenate([Kt[:, s], -Bt[:, s]], axis=0), NN, 1)
        S_s[gi] = S[gi] * e_last[:, s] + jnp.where(same_head, Z, 0.0)

    out_w = jnp.concatenate(O, axis=1)
    out = jnp.concatenate([out_w[:, b * C_HEADS:(b + 1) * C_HEADS] for b in range(nb)], axis=0)
    res = _rwkv_post(out, r, k, v, g, r_k[...], ln_w[...], ln_b[...])
    o_ref[...] = res.reshape(nb, C, C_HEADS)

    @pl.when(j == pl.num_programs(1) - 1)
    def _():
        for gi in range(RWKV_GROUPS):
            Sg = S_s[gi]
            for i in range(nh):
                b, h = head_of(gi * nh + i)
                sT_ref[b, h] = Sg[i * HEAD_DIM:(i + 1) * HEAD_DIM, i * HEAD_DIM:(i + 1) * HEAD_DIM]


def _hgrn_seq_kernel(ph_ref, s0_ref, lb, norm_g, o_ref, sT_ref, S_s, o_s):
    j = pl.program_id(1)
    C = CHUNK

    @pl.when(j == 0)
    def _():
        _pair_state_load(s0_ref, S_s, transpose=True)

    ph = ph_ref[0]
    q = ph[:, :C_HEADS]
    fpre = ph[:, C_HEADS:2 * C_HEADS]
    iv = ph[:, 2 * C_HEADS:3 * C_HEADS]
    f, kin = _hgrn_pre(fpre, lb[...])
    logf = jnp.log(f)
    tr, tc = _iota2((C, C))
    sub_tri = ((tr >= tc) & ((tr >> 4) == (tc >> 4))).astype(F32)
    cum = _mm(sub_tri, logf, NN, 6)
    qd = q * jnp.exp(cum)

    lo = lax.broadcasted_iota(jnp.int32, (SUB, PAIR), 1) < HEAD_DIM
    trow = lax.broadcasted_iota(jnp.int32, (SUB, PAIR), 0)
    pr_, pc_ = _iota2((PAIR, PAIR))
    same_head = (pr_ >= HEAD_DIM) == (pc_ >= HEAD_DIM)

    for i in range(C // SUB):
        rows = slice(i * SUB, (i + 1) * SUB)
        cum_i = cum[rows]
        cum_last = cum_i[SUB - 1:SUB]
        kd_i = kin[rows] * jnp.exp(cum_last - cum_i)
        e_last = jnp.exp(cum_last)
        q_i, kin_i, iv_i, qd_i = q[rows], kin[rows], iv[rows], qd[rows]
        for p in range(N_PAIRS):
            sl = slice(p * PAIR, (p + 1) * PAIR)
            S = S_s[p]
            o = _mm(qd_i[:, sl], S, NT, 1)
            cp, qp, kp, vp = cum_i[:, sl], q_i[:, sl], kin_i[:, sl], iv_i[:, sl]
            for s in range(SUB):
                causal = trow >= s
                d = jnp.where(causal, cp - cp[s:s + 1], 0.0)
                w = jnp.where(causal, qp * kp[s:s + 1] * jnp.exp(d), 0.0)
                a_lo = jnp.sum(jnp.where(lo, w, 0.0), axis=1, keepdims=True)
                a_hi = jnp.sum(jnp.where(lo, 0.0, w), axis=1, keepdims=True)
                o = o + jnp.where(lo, a_lo, a_hi) * vp[s:s + 1]
            Z = _mm(vp.T, kd_i[:, sl], NN, 1)
            S_s[p] = S * e_last[:, sl] + jnp.where(same_head, Z, 0.0)
            o_s[rows, sl] = o

    o_ref[0] = _hgrn_post(o_s[...], ph[:, 3 * C_HEADS:], norm_g[...])

    @pl.when(j == pl.num_programs(1) - 1)
    def _():
        _pair_state_store(S_s, sT_ref, transpose=True)


def _lru_seq_kernel(pl_ref, c0_ref, h0_ref, conv_w, conv_b, wa, ba, wx, bx, lam,
                    o_ref, hT_ref, xbuf_s, h_s, a_s, u_s, *, tm):
    j = pl.program_id(1)

    @pl.when(j == 0)
    def _():
        xbuf_s[5:8, :] = c0_ref[0]
        h_s[...] = h0_ref[0]

    blk = pl_ref[0]
    xb = blk[:, :C_LRU]
    xbuf_s[8:8 + tm, :] = xb
    x1 = xbuf_s[7:7 + tm, :]
    x2 = xbuf_s[6:6 + tm, :]
    x3 = xbuf_s[5:5 + tm, :]
    a, u = _lru_pre(xb, x1, x2, x3, conv_w[...], conv_b[...], wa[...], ba[...], wx[...], bx[...], lam[...])
    xbuf_s[5:8, :] = xb[tm - 3:tm, :]

    row8 = lax.broadcasted_iota(jnp.int32, a.shape, 0) & 7
    for d in (1, 2, 4):
        ok = row8 >= d
        u = jnp.where(ok, a * pltpu.roll(u, d, 0) + u, u)
        a = jnp.where(ok, a * pltpu.roll(a, d, 0), a)
    a_s[...] = a
    u_s[...] = u

    def group(gi, h):
        i = pl.multiple_of(gi * 8, 8)
        ht = u_s[pl.ds(i, 8), :] + a_s[pl.ds(i, 8), :] * h
        o_ref[0, pl.ds(i, 8), :] = ht
        return ht[7:8, :]

    h = lax.fori_loop(0, tm // 8, group, h_s[...])
    h_s[...] = h
    o_ref[0] = o_ref[0] * jax.nn.gelu(blk[:, C_LRU:])

    @pl.when(j == pl.num_programs(1) - 1)
    def _():
        hT_ref[0] = h


def _full_spec(a):
    return pl.BlockSpec(a.shape, lambda b, j: (0,) * a.ndim)


def _rwkv_seq(pr, p0, s0, prm, B, T):
    nb = RWKV_SEQS
    assert B % nb == 0
    seq = lambda c: pl.BlockSpec((nb, CHUNK, c), lambda b, j: (b, j, 0))
    sblk = pl.BlockSpec((nb, N_HEADS, HEAD_DIM, HEAD_DIM), lambda b, j: (b, 0, 0, 0))
    o, sT = pl.pallas_call(
        _rwkv_seq_kernel,
        grid=(B // nb, T // CHUNK),
        in_specs=[seq(RWKV_COLS), pl.BlockSpec((nb, 1, RWKV_COLS), lambda b, j: (b, 0, 0)), sblk]
                 + [_full_spec(a) for a in prm],
        out_specs=[seq(C_HEADS), sblk],
        out_shape=[jax.ShapeDtypeStruct((B, T, C_HEADS), F32), jax.ShapeDtypeStruct(s0.shape, F32)],
        scratch_shapes=[pltpu.VMEM((nb, 1, RWKV_COLS), F32), pltpu.VMEM((RWKV_GROUPS, GROUP, GROUP), F32)],
        compiler_params=_cparams("parallel", "arbitrary"),
        name="rwkv_seq",
    )(pr.reshape(B, T, RWKV_COLS), p0.reshape(B, 1, RWKV_COLS), s0, *prm)
    return o.reshape(B * T, C_HEADS), sT


def _hgrn_seq(ph, s0, lb, norm_g, B, T):
    seq = lambda c: pl.BlockSpec((1, CHUNK, c), lambda b, j: (b, j, 0))
    sblk = pl.BlockSpec((1, N_HEADS, HEAD_DIM, HEAD_DIM), lambda b, j: (b, 0, 0, 0))
    o, sT = pl.pallas_call(
        _hgrn_seq_kernel,
        grid=(B, T // CHUNK),
        in_specs=[seq(HGRN_COLS), sblk, _full_spec(lb), _full_spec(norm_g)],
        out_specs=[seq(C_HEADS), sblk],
        out_shape=[jax.ShapeDtypeStruct((B, T, C_HEADS), F32), jax.ShapeDtypeStruct(s0.shape, F32)],
        scratch_shapes=[pltpu.VMEM((N_PAIRS, PAIR, PAIR), F32), pltpu.VMEM((CHUNK, C_HEADS), F32)],
        compiler_params=_cparams("parallel", "arbitrary"),
        name="hgrn_seq",
    )(ph.reshape(B, T, HGRN_COLS), s0, lb, norm_g)
    return o.reshape(B * T, C_HEADS), sT


def _lru_seq(plru, conv0, h0, prm, B, T):
    tm = min(T, 256)
    seq = lambda c: pl.BlockSpec((1, tm, c), lambda b, j: (b, j, 0))
    hblk = pl.BlockSpec((1, 1, C_LRU), lambda b, j: (b, 0, 0))
    o, hT = pl.pallas_call(
        functools.partial(_lru_seq_kernel, tm=tm),
        grid=(B, T // tm),
        in_specs=[seq(LRU_COLS), pl.BlockSpec((1, CONV_WIDTH - 1, C_LRU), lambda b, j: (b, 0, 0)), hblk]
                 + [_full_spec(a) for a in prm],
        out_specs=[seq(C_LRU), hblk],
        out_shape=[jax.ShapeDtypeStruct((B, T, C_LRU), F32), jax.ShapeDtypeStruct((B, 1, C_LRU), F32)],
        scratch_shapes=[pltpu.VMEM((tm + 8, C_LRU), F32), pltpu.VMEM((1, C_LRU), F32),
                        pltpu.VMEM((tm, C_LRU), F32), pltpu.VMEM((tm, C_LRU), F32)],
        compiler_params=_cparams("parallel", "arbitrary"),
        name="lru_seq",
    )(plru.reshape(B, T, LRU_COLS), conv0, h0.reshape(B, 1, C_LRU), *prm)
    return o.reshape(B * T, C_LRU), hT.reshape(B, C_LRU)


def _block_diag(w):
    out = jnp.zeros((C_LRU, C_LRU), w.dtype)
    n = C_LRU // LRU_BLOCKS
    for i in range(LRU_BLOCKS):
        out = out.at[i * n:(i + 1) * n, i * n:(i + 1) * n].set(w[i])
    return out


def _prepare_params(p):
    row = lambda a: a.reshape(1, -1)
    zpad = jnp.zeros((W_LORA, C_HEADS), F32)
    layers = []
    for l in range(DEPTH):
        pre = (row(p['mu_shift'][l]), row(p['rwkv_w0'][l]),
               jnp.concatenate([p['rwkv_w_up'][l], zpad], 0).astype(BF16),
               row(p['rwkv_a0'][l]),
               jnp.concatenate([zpad, p['rwkv_a_up'][l]], 0).astype(BF16),
               p['rwkv_g_up'][l].astype(BF16), row(p['rwkv_k_k'][l]), row(p['rwkv_k_a'][l]),
               None,
               p['lru_conv_w'][l], row(p['lru_conv_b'][l]),
               _block_diag(p['lru_wa'][l]).astype(BF16), row(p['lru_ba'][l]),
               _block_diag(p['lru_wx'][l]).astype(BF16), row(p['lru_bx'][l]), row(p['lru_lambda'][l]))
        layers.append(dict(
            norm1_g=row(p['norm1_g'][l]), w_in=p['w_in'][l].astype(BF16), pre=pre,
            r_k=row(p['rwkv_r_k'][l]), ln_w=row(p['rwkv_ln_w'][l]), ln_b=row(p['rwkv_ln_b'][l]),
            hgrn_norm_g=row(p['hgrn_norm_g'][l]),
            w_out=p['w_out'][l].astype(BF16), norm2_g=row(p['norm2_g'][l]),
            mlp_w1=p['mlp_w1'][l].astype(BF16), mlp_w2=p['mlp_w2'][l].astype(BF16)))
    return layers


def _trunk(x, wkv, shift, hgrn, lru, conv, layers, lb_all, final_g):
    B, T, _ = x.shape
    m = B * T
    assert T == 1 or T % CHUNK == 0, "sequence kernels need T to be a multiple of CHUNK"
    x = x.reshape(m, D_MODEL)
    heads = lambda a: a.reshape(B, T, -1, N_HEADS, HEAD_DIM)
    n_wkv, n_shift, n_hgrn, n_lru, n_conv = [], [], [], [], []
    for l, lp in enumerate(layers):
        xn, pr, ph, plru = _norm_proj(x, lp['norm1_g'], lp['w_in'])
        p0 = _proj(shift[l], lp['w_in'][:, :RWKV_COLS])
        prm = list(lp['pre'])
        prm[8] = lb_all[l:l + 1]
        if T == 1:
            r, w, k, v, kk, ka, g, f, kin, a, u = _mixer_pre_step(pr, ph, plru, p0, conv[l], prm)
            wo, S_r = _head_scan(_rwkv_scan_kernel, "rwkv_scan",
                                 [(heads(t), 0) for t in (r, w, k, v, kk, ka)], wkv[l], B, T)
            ph4 = heads(ph)
            ho, S_h = _head_scan(_hgrn_scan_kernel, "hgrn_scan",
                                 [(ph4, 0), (heads(f), 0), (heads(kin), 0), (ph4, 2)], hgrn[l], B, T)
            lh, h_l = _lru_scan(a, u, lru[l], B, T)
            mix = _mix_post(wo.reshape(m, C_HEADS), r, k, v, g, lp['r_k'], lp['ln_w'], lp['ln_b'],
                            ho.reshape(m, C_HEADS), ph, lp['hgrn_norm_g'], lh, plru)
            o_r, o_h, o_l = mix[:, :C_HEADS], mix[:, C_HEADS:2 * C_HEADS], mix[:, 2 * C_HEADS:]
        else:
            o_r, S_r = _rwkv_seq(pr, p0, wkv[l], prm[:8] + [lp['r_k'], lp['ln_w'], lp['ln_b']], B, T)
            o_h, S_h = _hgrn_seq(ph, hgrn[l], prm[8], lp['hgrn_norm_g'], B, T)
            o_l, h_l = _lru_seq(plru, conv[l], lru[l], prm[9:], B, T)
        x = _out_mlp(x, o_r, o_h, o_l, lp['w_out'], lp['norm2_g'], lp['mlp_w1'], lp['mlp_w2'])
        xb = plru.reshape(B, T, LRU_COLS)[:, :, :C_LRU]
        hist = jnp.concatenate([conv[l], xb[:, max(T - (CONV_WIDTH - 1), 0):]], axis=1)
        n_wkv.append(S_r)
        n_shift.append(xn.reshape(B, T, D_MODEL)[:, -1])
        n_hgrn.append(S_h)
        n_lru.append(h_l)
        n_conv.append(hist[:, -(CONV_WIDTH - 1):])
    y = _final_norm(x, final_g).reshape(B, T, D_MODEL)
    return (y, jnp.stack(n_wkv), jnp.stack(n_shift), jnp.stack(n_hgrn), jnp.stack(n_lru), jnp.stack(n_conv))


def kernel(x_prompt, x_sample, state_wkv, state_shift, state_hgrn, state_lru, state_conv, norm1_g, w_in, mu_shift, rwkv_w0, rwkv_w_up, rwkv_a0, rwkv_a_up, rwkv_g_up, rwkv_k_k, rwkv_k_a, rwkv_r_k, rwkv_ln_w, rwkv_ln_b, hgrn_lb, hgrn_norm_g, lru_conv_w, lru_conv_b, lru_wa, lru_ba, lru_wx, lru_bx, lru_lambda, w_out, norm2_g, mlp_w1, mlp_w2, final_g):
    prm = dict(norm1_g=norm1_g, w_in=w_in, mu_shift=mu_shift, rwkv_w0=rwkv_w0, rwkv_w_up=rwkv_w_up,
               rwkv_a0=rwkv_a0, rwkv_a_up=rwkv_a_up, rwkv_g_up=rwkv_g_up, rwkv_k_k=rwkv_k_k,
               rwkv_k_a=rwkv_k_a, rwkv_r_k=rwkv_r_k.reshape(DEPTH, C_HEADS), rwkv_ln_w=rwkv_ln_w,
               rwkv_ln_b=rwkv_ln_b, hgrn_norm_g=hgrn_norm_g, lru_conv_w=lru_conv_w,
               lru_conv_b=lru_conv_b, lru_wa=lru_wa, lru_ba=lru_ba, lru_wx=lru_wx, lru_bx=lru_bx,
               lru_lambda=lru_lambda, w_out=w_out, norm2_g=norm2_g, mlp_w1=mlp_w1, mlp_w2=mlp_w2)
    layers = _prepare_params(prm)
    lb_all = _hgrn_lower_bounds(hgrn_lb)
    fg = final_g.reshape(1, D_MODEL)
    Bp = x_prompt.shape[0]
    dt = x_prompt.dtype
    z_wkv = jnp.zeros((DEPTH, Bp, N_HEADS, HEAD_DIM, HEAD_DIM), dt)
    z_shift = jnp.zeros((DEPTH, Bp, D_MODEL), dt)
    z_hgrn = jnp.zeros((DEPTH, Bp, N_HEADS, HEAD_DIM, HEAD_DIM), dt)
    z_lru = jnp.zeros((DEPTH, Bp, C_LRU), dt)
    z_conv = jnp.zeros((DEPTH, Bp, CONV_WIDTH - 1, C_LRU), dt)
    y_p, p_wkv, p_shift, p_hgrn, p_lru, p_conv = _trunk(x_prompt, z_wkv, z_shift, z_hgrn, z_lru, z_conv,
                                                         layers, lb_all, fg)
    y_s, s_wkv, s_shift, s_hgrn, s_lru, s_conv = _trunk(x_sample, state_wkv, state_shift, state_hgrn,
                                                         state_lru, state_conv, layers, lb_all, fg)
    return (y_p, y_s, p_wkv, p_shift, p_hgrn, p_lru, p_conv, s_wkv, s_shift, s_hgrn, s_lru, s_conv)
```

```python
import functools

import jax
import jax.numpy as jnp
from jax import lax
from jax.experimental import pallas as pl
from jax.experimental.pallas import tpu as pltpu

F32 = jnp.float32
BF16 = jnp.bfloat16

D_MODEL = 1024
DEPTH = 4
HEAD_DIM = 64
N_HEADS = 6
C_HEADS = N_HEADS * HEAD_DIM
C_LRU = 256
LRU_BLOCKS = 4
CONV_WIDTH = 4
LRU_C = 8.0
W_LORA = 64
A_LORA = 64
G_LORA = 128
RWKV_COLS = 3 * C_HEADS + W_LORA + A_LORA + G_LORA
HGRN_COLS = 4 * C_HEADS
LRU_COLS = 2 * C_LRU
C_IN = RWKV_COLS + HGRN_COLS + LRU_COLS
D_FF = 4 * D_MODEL
NORM_EPS = 1e-6
GN_EPS = 64e-5

LANES = 128
VMEM_LIMIT_BYTES = 48 * 1024 * 1024


def _cparams(*sem):
    return pltpu.CompilerParams(dimension_semantics=sem, vmem_limit_bytes=VMEM_LIMIT_BYTES)


def _dot(a, b):
    return jnp.dot(a.astype(BF16), b.astype(BF16), preferred_element_type=F32)


def _rmsnorm(x, g):
    return x * lax.rsqrt(jnp.mean(x * x, axis=-1, keepdims=True) + NORM_EPS) * g


def _softplus(x):
    return jnp.maximum(x, 0.0) + jnp.log1p(jnp.exp(-jnp.abs(x)))


def _head_sum(x):
    m = x.shape[0]
    lo = lax.broadcasted_iota(jnp.int32, (m, LANES), 1) < HEAD_DIM
    outs = []
    for p in range(x.shape[1] // LANES):
        xp = x[:, p * LANES:(p + 1) * LANES]
        s_lo = jnp.sum(jnp.where(lo, xp, 0.0), axis=1, keepdims=True)
        s_hi = jnp.sum(jnp.where(lo, 0.0, xp), axis=1, keepdims=True)
        outs.append(jnp.where(lo, s_lo, s_hi))
    return jnp.concatenate(outs, axis=1)


def _norm_proj_kernel(x_ref, g_ref, w_ref, xn_ref, pr_ref, ph_ref, pl_ref):
    xn = _rmsnorm(x_ref[...], g_ref[...])
    xn_ref[...] = xn
    p = jnp.dot(xn.astype(BF16), w_ref[...], preferred_element_type=F32)
    pr_ref[...] = p[:, :RWKV_COLS]
    ph_ref[...] = p[:, RWKV_COLS:RWKV_COLS + HGRN_COLS]
    pl_ref[...] = p[:, RWKV_COLS + HGRN_COLS:]


def _norm_proj(x, g, w):
    m = x.shape[0]
    tm = min(m, 256)
    row = lambda c: pl.BlockSpec((tm, c), lambda i: (i, 0))
    full = lambda a: pl.BlockSpec(a.shape, lambda i: (0,) * a.ndim)
    return pl.pallas_call(
        _norm_proj_kernel,
        grid=(m // tm,),
        in_specs=[row(D_MODEL), full(g), full(w)],
        out_specs=[row(D_MODEL), row(RWKV_COLS), row(HGRN_COLS), row(LRU_COLS)],
        out_shape=[jax.ShapeDtypeStruct((m, c), F32) for c in (D_MODEL, RWKV_COLS, HGRN_COLS, LRU_COLS)],
        compiler_params=_cparams("parallel"),
        name="norm_proj",
    )(x, g, w)


def _proj_kernel(x_ref, w_ref, o_ref):
    o_ref[...] = jnp.dot(x_ref[...].astype(BF16), w_ref[...], preferred_element_type=F32)


def _proj(x, w):
    return pl.pallas_call(
        _proj_kernel,
        out_shape=jax.ShapeDtypeStruct((x.shape[0], w.shape[1]), F32),
        compiler_params=pltpu.CompilerParams(vmem_limit_bytes=VMEM_LIMIT_BYTES),
        name="shift_proj",
    )(x, w)


def _out_mlp_kernel(x_ref, or_ref, oh_ref, ol_ref, wo_ref, g2_ref, w1_ref, w2_ref, o_ref, x1_s, xn_s, acc_s):
    j = pl.program_id(1)

    @pl.when(j == 0)
    def _():
        mix = jnp.concatenate([or_ref[...], oh_ref[...], ol_ref[...]], axis=1)
        x1 = x_ref[...] + jnp.dot(mix.astype(BF16), wo_ref[...], preferred_element_type=F32)
        x1_s[...] = x1
        xn_s[...] = _rmsnorm(x1, g2_ref[...]).astype(BF16)
        acc_s[...] = jnp.zeros_like(acc_s)

    h = jnp.dot(xn_s[...], w1_ref[...], preferred_element_type=F32)
    h = jnp.square(jnp.maximum(h, 0.0))
    acc_s[...] += jnp.dot(h.astype(BF16), w2_ref[...], preferred_element_type=F32)

    @pl.when(j == pl.num_programs(1) - 1)
    def _():
        o_ref[...] = x1_s[...] + acc_s[...]


def _out_mlp(x, o_r, o_h, o_l, wo, g2, w1, w2):
    m = x.shape[0]
    tm = min(m, 512)
    tf = 2048
    row = pl.BlockSpec((tm, D_MODEL), lambda i, j: (i, 0))
    part = lambda c: pl.BlockSpec((tm, c), lambda i, j: (i, 0))
    return pl.pallas_call(
        _out_mlp_kernel,
        grid=(m // tm, D_FF // tf),
        in_specs=[row, part(C_HEADS), part(C_HEADS), part(C_LRU),
                  pl.BlockSpec((D_MODEL, D_MODEL), lambda i, j: (0, 0)),
                  pl.BlockSpec((1, D_MODEL), lambda i, j: (0, 0)),
                  pl.BlockSpec((D_MODEL, tf), lambda i, j: (0, j)),
                  pl.BlockSpec((tf, D_MODEL), lambda i, j: (j, 0))],
        out_specs=row,
        out_shape=jax.ShapeDtypeStruct((m, D_MODEL), F32),
        scratch_shapes=[pltpu.VMEM((tm, D_MODEL), F32), pltpu.VMEM((tm, D_MODEL), BF16),
                        pltpu.VMEM((tm, D_MODEL), F32)],
        compiler_params=_cparams("parallel", "arbitrary"),
        name="out_mlp",
    )(x, o_r, o_h, o_l, wo, g2, w1, w2)


def _final_norm_kernel(x_ref, g_ref, o_ref):
    o_ref[...] = _rmsnorm(x_ref[...], g_ref[...])


def _final_norm(x, g):
    m = x.shape[0]
    tm = min(m, 512)
    row = pl.BlockSpec((tm, D_MODEL), lambda i: (i, 0))
    return pl.pallas_call(
        _final_norm_kernel,
        grid=(m // tm,),
        in_specs=[row, pl.BlockSpec((1, D_MODEL), lambda i: (0, 0))],
        out_specs=row,
        out_shape=jax.ShapeDtypeStruct((m, D_MODEL), F32),
        compiler_params=_cparams("parallel"),
        name="final_norm",
    )(x, g)


def _lb_kernel(p_ref, o_ref):
    p = p_ref[...]
    e = jnp.exp(p - jnp.max(p, axis=0, keepdims=True))
    s = e / jnp.sum(e, axis=0, keepdims=True)
    acc = jnp.zeros_like(s[0:1])
    for l in range(DEPTH):
        if l > 0:
            acc = acc + s[l:l + 1]
        o_ref[l:l + 1, :] = acc


def _hgrn_lower_bounds(hgrn_lb):
    return pl.pallas_call(_lb_kernel, out_shape=jax.ShapeDtypeStruct(hgrn_lb.shape, F32),
                          name="hgrn_lower_bounds")(hgrn_lb)


def _rwkv_pre(pr, prev, mu, w0, w_up, a0, a_up, g_up, k_k, k_a):
    c = C_HEADS
    pm = pr + mu * (prev - pr)
    r = pm[:, :c]
    k0 = pm[:, c:2 * c]
    v = pm[:, 2 * c:3 * c]
    xwa = pm[:, 3 * c:3 * c + W_LORA + A_LORA]
    xg = pm[:, 3 * c + W_LORA + A_LORA:]
    w_log = -_softplus(-(w0 + _dot(jnp.tanh(xwa), w_up))) - 0.5
    log_decay = -jnp.exp(w_log)
    a = jax.nn.sigmoid(a0 + _dot(xwa, a_up))
    g = _dot(jax.nn.sigmoid(xg), g_up)
    kk = k0 * k_k
    kk = kk * lax.rsqrt(jnp.maximum(_head_sum(kk * kk), 1e-24))
    k = k0 * (1.0 + (a - 1.0) * k_a)
    return r, log_decay, k, v, kk, kk * a, g


def _hgrn_pre(fpre, lb):
    sg = jax.nn.sigmoid(fpre)
    f = lb + (1.0 - lb) * sg
    kin = (1.0 - lb) * jax.nn.sigmoid(-fpre)
    return f, kin


def _lru_pre(xb, x1, x2, x3, conv_w, conv_b, wa, ba, wx, bx, lam):
    xc = conv_b + (((x3 * conv_w[0:1] + x2 * conv_w[1:2]) + x1 * conv_w[2:3]) + xb * conv_w[3:4])
    r = jax.nn.sigmoid(_dot(xc, wa) + ba)
    i = jax.nn.sigmoid(_dot(xc, wx) + bx)
    log_a = -LRU_C * r * _softplus(-lam)
    a = jnp.exp(log_a)
    one_minus_a2 = -jnp.tanh(log_a) * (jnp.exp(2.0 * log_a) + 1.0)
    u = xc * i * jnp.sqrt(jnp.maximum(one_minus_a2, 1e-12))
    return a, u


_N_PRE_PARAMS = 16


def _pre_compute(pr, prev, ph, xb, x1, x2, x3, prm, outs):
    (mu, w0, w_up, a0, a_up, g_up, k_k, k_a, lb, conv_w, conv_b, wa, ba, wx, bx, lam) = prm
    r_o, w_o, k_o, v_o, kk_o, ka_o, g_o, f_o, kin_o, a_o, u_o = outs
    r, lw, k, v, kk, ka, g = _rwkv_pre(pr, prev, mu[...], w0[...], w_up[...], a0[...], a_up[...],
                                       g_up[...], k_k[...], k_a[...])
    w = jnp.exp(lw)
    f, kin = _hgrn_pre(ph[:, C_HEADS:2 * C_HEADS], lb[...])
    a, u = _lru_pre(xb, x1, x2, x3, conv_w[...], conv_b[...], wa[...], ba[...], wx[...], bx[...], lam[...])
    return ((r_o, r), (w_o, w), (k_o, k), (v_o, v), (kk_o, kk), (ka_o, ka), (g_o, g),
            (f_o, f), (kin_o, kin), (a_o, a), (u_o, u))


def _pre_step_kernel(*refs):
    pr_ref, ph_ref, pl_ref, p0_ref, c0_ref, c1_ref, c2_ref = refs[:7]
    prm = refs[7:7 + _N_PRE_PARAMS]
    outs = refs[7 + _N_PRE_PARAMS:]
    xb = pl_ref[...][:, :C_LRU]
    res = _pre_compute(pr_ref[...], p0_ref[...], ph_ref[...], xb, c2_ref[...], c1_ref[...], c0_ref[...],
                       prm, outs)
    for o_ref, val in res:
        o_ref[...] = val


def _mixer_pre_step(pr, ph, plru, p0, conv0, prm):
    widths = (C_HEADS,) * 9 + (C_LRU,) * 2
    return pl.pallas_call(
        _pre_step_kernel,
        out_shape=[jax.ShapeDtypeStruct((pr.shape[0], c), F32) for c in widths],
        compiler_params=pltpu.CompilerParams(vmem_limit_bytes=VMEM_LIMIT_BYTES),
        name="mixer_pre_step",
    )(pr, ph, plru, p0, conv0[:, 0], conv0[:, 1], conv0[:, 2], *prm)


def _eye():
    return (lax.broadcasted_iota(jnp.int32, (HEAD_DIM, HEAD_DIM), 0)
            == lax.broadcasted_iota(jnp.int32, (HEAD_DIM, HEAD_DIM), 1))


def _to_col(row, eye):
    return jnp.sum(jnp.where(eye, row, 0.0), axis=1, keepdims=True)


def _to_row(col, eye):
    return jnp.sum(jnp.where(eye, col, 0.0), axis=0, keepdims=True)


def _rwkv_scan_kernel(r_ref, w_ref, k_ref, v_ref, kk_ref, ka_ref, s0_ref, o_ref, sT_ref, S_s, *, bb, tc):
    j = pl.program_id(1)

    @pl.when(j == 0)
    def _():
        S_s[...] = s0_ref[...]

    eye = _eye()

    def step(t, carry):
        for b in range(bb):
            for h in range(N_HEADS):
                idx = (b, t, pl.ds(h, 1), slice(None))
                S = S_s[b, h]
                kk = kk_ref[idx]
                sa = -jnp.sum(S * kk, axis=1, keepdims=True)
                S = S * w_ref[idx] + sa * ka_ref[idx] + _to_col(v_ref[idx], eye) * k_ref[idx]
                S_s[b, h] = S
                o_ref[idx] = _to_row(jnp.sum(S * r_ref[idx], axis=1, keepdims=True), eye)
        return carry

    lax.fori_loop(0, tc, step, 0)

    @pl.when(j == pl.num_programs(1) - 1)
    def _():
        sT_ref[...] = S_s[...]


def _hgrn_scan_kernel(q_ref, f_ref, kin_ref, iv_ref, s0_ref, o_ref, sT_ref, S_s, *, bb, tc):
    j = pl.program_id(1)

    @pl.when(j == 0)
    def _():
        S_s[...] = s0_ref[...]

    eye = _eye()

    def step(t, carry):
        for b in range(bb):
            for h in range(N_HEADS):
                idx = (b, t, pl.ds(h, 1), slice(None))
                S = S_s[b, h]
                S = _to_col(f_ref[idx], eye) * S + _to_col(kin_ref[idx], eye) * iv_ref[idx]
                S_s[b, h] = S
                o_ref[idx] = jnp.sum(S * _to_col(q_ref[idx], eye), axis=0, keepdims=True)
        return carry

    lax.fori_loop(0, tc, step, 0)

    @pl.when(j == pl.num_programs(1) - 1)
    def _():
        sT_ref[...] = S_s[...]


def _lru_scan_kernel(a_ref, u_ref, h0_ref, o_ref, hT_ref, h_s, *, bb, tc):
    j = pl.program_id(1)

    @pl.when(j == 0)
    def _():
        h_s[...] = h0_ref[...]

    def step(t, carry):
        for b in range(bb):
            idx = (b, pl.ds(t, 1), slice(None))
            h = a_ref[idx] * h_s[b] + u_ref[idx]
            h_s[b] = h
            o_ref[idx] = h
        return carry

    lax.fori_loop(0, tc, step, 0)

    @pl.when(j == pl.num_programs(1) - 1)
    def _():
        hT_ref[...] = h_s[...]


def _scan_tiles(B, T):
    bb = 8 if T == 1 else 4
    tc = min(T, 32)
    return bb, tc


def _head_scan(kernel, name, seqs, s0, B, T):
    bb, tc = _scan_tiles(B, T)
    hblk = lambda g: pl.BlockSpec((bb, tc, None, N_HEADS, HEAD_DIM), lambda i, j: (i, j, g, 0, 0))
    sblk = pl.BlockSpec((bb, N_HEADS, HEAD_DIM, HEAD_DIM), lambda i, j: (i, 0, 0, 0))
    return pl.pallas_call(
        functools.partial(kernel, bb=bb, tc=tc),
        grid=(B // bb, T // tc),
        in_specs=[hblk(g) for _, g in seqs] + [sblk],
        out_specs=[hblk(0), sblk],
        out_shape=[jax.ShapeDtypeStruct((B, T, 1, N_HEADS, HEAD_DIM), F32),
                   jax.ShapeDtypeStruct(s0.shape, F32)],
        scratch_shapes=[pltpu.VMEM((bb, N_HEADS, HEAD_DIM, HEAD_DIM), F32)],
        compiler_params=_cparams("parallel", "arbitrary"),
        name=name,
    )(*[a for a, _ in seqs], s0)


def _lru_scan(a, u, h0, B, T):
    bb, tc = _scan_tiles(B, T)
    blk = pl.BlockSpec((bb, tc, C_LRU), lambda i, j: (i, j, 0))
    hblk = pl.BlockSpec((bb, 1, C_LRU), lambda i, j: (i, 0, 0))
    o, hT = pl.pallas_call(
        functools.partial(_lru_scan_kernel, bb=bb, tc=tc),
        grid=(B // bb, T // tc),
        in_specs=[blk, blk, hblk],
        out_specs=[blk, hblk],
        out_shape=[jax.ShapeDtypeStruct((B, T, C_LRU), F32), jax.ShapeDtypeStruct((B, 1, C_LRU), F32)],
        scratch_shapes=[pltpu.VMEM((bb, 1, C_LRU), F32)],
        compiler_params=_cparams("parallel", "arbitrary"),
        name="lru_scan",
    )(a.reshape(B, T, C_LRU), u.reshape(B, T, C_LRU), h0.reshape(B, 1, C_LRU))
    return o.reshape(B * T, C_LRU), hT.reshape(B, C_LRU)


def _rwkv_post(out, r, k, v, g, r_k, ln_w, ln_b):
    inv_n = 1.0 / HEAD_DIM
    mean = _head_sum(out) * inv_n
    d = out - mean
    var = _head_sum(d * d) * inv_n
    gn = d * lax.rsqrt(var + GN_EPS) * ln_w + ln_b
    bonus = _head_sum(r * k * r_k) * v
    return (gn + bonus) * g


def _hgrn_post(o, gate, norm_g):
    o = o * lax.rsqrt(_head_sum(o * o) * (1.0 / HEAD_DIM) + NORM_EPS)
    return o * norm_g * jax.nn.silu(gate)


def _mix_post_kernel(wo_ref, r_ref, k_ref, v_ref, g_ref, rk_ref, lnw_ref, lnb_ref,
                     ho_ref, hg_ref, hng_ref, lh_ref, lg_ref, mix_ref):
    mix_ref[:, :C_HEADS] = _rwkv_post(wo_ref[...], r_ref[...], k_ref[...], v_ref[...], g_ref[...],
                                      rk_ref[...], lnw_ref[...], lnb_ref[...])
    mix_ref[:, C_HEADS:2 * C_HEADS] = _hgrn_post(ho_ref[...], hg_ref[...], hng_ref[...])
    mix_ref[:, 2 * C_HEADS:] = lh_ref[...] * jax.nn.gelu(lg_ref[...])


def _mix_post(wo, r, k, v, g, r_k, ln_w, ln_b, ho, ph, hng, lh, plru):
    m = wo.shape[0]
    tm = min(m, 512)
    hrow = pl.BlockSpec((tm, C_HEADS), lambda i: (i, 0))
    prow = lambda c: pl.BlockSpec((1, c), lambda i: (0, 0))
    return pl.pallas_call(
        _mix_post_kernel,
        grid=(m // tm,),
        in_specs=[hrow, hrow, hrow, hrow, hrow, prow(C_HEADS), prow(C_HEADS), prow(C_HEADS),
                  hrow, pl.BlockSpec((tm, C_HEADS), lambda i: (i, 3)), prow(C_HEADS),
                  pl.BlockSpec((tm, C_LRU), lambda i: (i, 0)),
                  pl.BlockSpec((tm, C_LRU), lambda i: (i, 1))],
        out_specs=pl.BlockSpec((tm, D_MODEL), lambda i: (i, 0)),
        out_shape=jax.ShapeDtypeStruct((m, D_MODEL), F32),
        compiler_params=_cparams("parallel"),
        name="mix_post",
    )(wo, r, k, v, g, r_k, ln_w, ln_b, ho, ph, hng, lh, plru)


CHUNK = 64
CHUNK_LOG2 = 6
SEQS = 4
GROUP_HEADS = 4
GROUP = GROUP_HEADS * HEAD_DIM
N_GROUPS = SEQS * N_HEADS // GROUP_HEADS

NN = ((1,), (0,))
NT = ((1,), (1,))


def _mm(a, b, dims, passes):
    dn = (dims, ((), ()))
    if passes == 6:
        return lax.dot_general(a, b, dn, precision=lax.Precision.HIGHEST, preferred_element_type=F32)
    ah, bh = a.astype(BF16), b.astype(BF16)
    out = lax.dot_general(ah, bh, dn, preferred_element_type=F32)
    if passes == 3:
        al = (a - ah.astype(F32)).astype(BF16)
        bl = (b - bh.astype(F32)).astype(BF16)
        out = (out + lax.dot_general(ah, bl, dn, preferred_element_type=F32)
               + lax.dot_general(al, bh, dn, preferred_element_type=F32))
    return out


def _iota2(shape):
    return lax.broadcasted_iota(jnp.int32, shape, 0), lax.broadcasted_iota(jnp.int32, shape, 1)


def _head_of(group, i):
    hw = group * GROUP_HEADS + i
    return hw // N_HEADS, hw % N_HEADS


def _group_state_load(s_ref, S_s, transpose):
    z = jnp.zeros((HEAD_DIM, HEAD_DIM), F32)
    for gi in range(N_GROUPS):
        rows = []
        for i in range(GROUP_HEADS):
            b, h = _head_of(gi, i)
            blk = s_ref[b, h].T if transpose else s_ref[b, h]
            rows.append(jnp.concatenate([blk if i2 == i else z for i2 in range(GROUP_HEADS)], axis=1))
        S_s[gi] = jnp.concatenate(rows, axis=0)


def _group_state_store(S_s, s_ref, transpose):
    for gi in range(N_GROUPS):
        Sg = S_s[gi]
        for i in range(GROUP_HEADS):
            b, h = _head_of(gi, i)
            blk = Sg[i * HEAD_DIM:(i + 1) * HEAD_DIM, i * HEAD_DIM:(i + 1) * HEAD_DIM]
            s_ref[b, h] = blk.T if transpose else blk


def _wide(x):
    return jnp.concatenate([x[b * CHUNK:(b + 1) * CHUNK] for b in range(SEQS)], axis=1)


def _tall(x):
    return jnp.concatenate([x[:, b * C_HEADS:(b + 1) * C_HEADS] for b in range(SEQS)], axis=0)


def _seq_cumsum(x):
    tr, tc = _iota2((SEQS * CHUNK, SEQS * CHUNK))
    tri = ((tr >= tc) & ((tr >> CHUNK_LOG2) == (tc >> CHUNK_LOG2))).astype(F32)
    return _mm(tri, x, NN, 6)


def _lane_head():
    return lax.broadcasted_iota(jnp.int32, (CHUNK, GROUP), 1) >> 6


def _by_head(x, lane_head):
    return jnp.concatenate([jnp.where(lane_head == i, x, 0.0) for i in range(GROUP_HEADS)], axis=0)


def _rep(x):
    return jnp.concatenate([x] * GROUP_HEADS, axis=0)


def _pick(x, lane_head):
    out = x[:CHUNK]
    for i in range(1, GROUP_HEADS):
        out = jnp.where(lane_head == i, x[i * CHUNK:(i + 1) * CHUNK], out)
    return out


def _unit_lower_inverse(Ls):
    r, c = _iota2(Ls[0].shape)
    eye = (r == c).astype(F32)
    diag_blk = (r >> 4) == (c >> 4)
    mm = lambda xs, ys: [_mm(x, y, NN, 1) for x, y in zip(xs, ys)]
    Ld = [jnp.where(diag_blk, L, 0.0) for L in Ls]
    N = [jnp.where(diag_blk, 0.0, L) for L in Ls]
    L2 = mm(Ld, Ld)
    L4 = mm(L2, L2)
    P1 = mm([eye - x for x in Ld], [eye + x for x in L2])
    L8 = mm(L4, L4)
    P2 = mm([eye + x for x in L4], [eye + x for x in L8])
    Dinv = mm(P1, P2)
    M = mm(Dinv, N)
    M2 = mm(M, M)
    X = mm([eye - x for x in M], [eye + x for x in M2])
    return mm(X, Dinv)


def _rwkv_seq_kernel(pr_ref, p0_ref, s0_ref, mu, w0, w_up, a0, a_up, g_up, k_k, k_a, r_k, ln_w, ln_b,
                     o_ref, sT_ref, carry_s, S_s):
    j = pl.program_id(1)
    C = CHUNK

    @pl.when(j == 0)
    def _():
        carry_s[...] = p0_ref[...]
        _group_state_load(s0_ref, S_s, transpose=False)

    pr = pr_ref[...].reshape(SEQS * C, RWKV_COLS)
    prow = lax.broadcasted_iota(jnp.int32, pr.shape, 0)
    prev = pltpu.roll(pr, 1, 0)
    for b in range(SEQS):
        prev = jnp.where(prow == b * C, carry_s[b], prev)
        carry_s[b] = pr[(b + 1) * C - 1:(b + 1) * C, :]

    r, lw, k, v, kk, kb, g = _rwkv_pre(pr, prev, mu[...], w0[...], w_up[...], a0[...], a_up[...],
                                       g_up[...], k_k[...], k_a[...])
    cum_w, lw_w = _wide(_seq_cumsum(lw)), _wide(lw)
    r_w, k_w, v_w, kk_w, kb_w = _wide(r), _wide(k), _wide(v), _wide(kk), _wide(kb)
    cum_last = cum_w[C - 1:C, :]
    e_inv = jnp.exp(-cum_w)
    e_rel = jnp.exp(cum_last - cum_w)
    e_last = jnp.exp(cum_last)
    Kg = kk_w * jnp.exp(cum_w - lw_w)
    Rg = r_w * jnp.exp(cum_w)
    Ki, Bi = k_w * e_inv, kb_w * e_inv
    Kt, Bt = k_w * e_rel, kb_w * e_rel

    lane_head = _lane_head()
    gr, gc = _iota2((GROUP, GROUP))
    strict = (gr & (HEAD_DIM - 1)) > (gc & (HEAD_DIM - 1))
    incl = (gr & (HEAD_DIM - 1)) >= (gc & (HEAD_DIM - 1))
    same_head = (gr >> 6) == (gc >> 6)
    by_head = lambda x: _by_head(x, lane_head)
    pick = lambda x: _pick(x, lane_head)
    rep = _rep

    groups = range(N_GROUPS)
    sl = [slice(gi * GROUP, (gi + 1) * GROUP) for gi in groups]
    G = [_mm(jnp.concatenate([by_head(Kg[:, s]), by_head(Rg[:, s])], axis=0),
             jnp.concatenate([by_head(Bi[:, s]), by_head(Ki[:, s])], axis=0), NT, 1) for s in sl]
    Tinv = _unit_lower_inverse([jnp.where(strict, x[:GROUP, :GROUP], 0.0) for x in G])
    S = [S_s[gi] for gi in groups]
    H0 = [_mm(jnp.concatenate([Kg[:, s], Rg[:, s]], axis=0), S[gi], NT, 1) for gi, s in enumerate(sl)]
    VV = [rep(v_w[:, s]) for s in sl]
    rhs_u = [rep(H0[gi][:C]) + _mm(jnp.where(strict, G[gi][:GROUP, GROUP:], 0.0), VV[gi], NN, 1)
             for gi in groups]
    U = [pick(_mm(Tinv[gi], rhs_u[gi], NN, 1)) for gi in groups]
    O = [pick(rep(H0[gi][C:])
              + _mm(jnp.concatenate([jnp.where(incl, G[gi][GROUP:, GROUP:], 0.0),
                                     jnp.where(incl, -G[gi][GROUP:, :GROUP], 0.0)], axis=1),
                    jnp.concatenate([VV[gi], rep(U[gi])], axis=0), NN, 1)) for gi in groups]
    for gi, s in enumerate(sl):
        Z = _mm(jnp.concatenate([v_w[:, s], U[gi]], axis=0).T,
                jnp.concatenate([Kt[:, s], -Bt[:, s]], axis=0), NN, 1)
        S_s[gi] = S[gi] * e_last[:, s] + jnp.where(same_head, Z, 0.0)

    res = _rwkv_post(_tall(jnp.concatenate(O, axis=1)), r, k, v, g, r_k[...], ln_w[...], ln_b[...])
    o_ref[...] = res.reshape(SEQS, C, C_HEADS)

    @pl.when(j == pl.num_programs(1) - 1)
    def _():
        _group_state_store(S_s, sT_ref, transpose=False)


def _hgrn_seq_kernel(ph_ref, s0_ref, lb, norm_g, o_ref, sT_ref, S_s):
    j = pl.program_id(1)
    C = CHUNK

    @pl.when(j == 0)
    def _():
        _group_state_load(s0_ref, S_s, transpose=True)

    ph = ph_ref[...].reshape(SEQS * C, HGRN_COLS)
    f, kin = _hgrn_pre(ph[:, C_HEADS:2 * C_HEADS], lb[...])
    cum = _wide(_seq_cumsum(jnp.log(f)))
    q_w, kin_w, iv_w = _wide(ph[:, :C_HEADS]), _wide(kin), _wide(ph[:, 2 * C_HEADS:3 * C_HEADS])
    cum_last = cum[C - 1:C, :]
    e_last = jnp.exp(cum_last)
    qd = q_w * jnp.exp(cum)
    kd = kin_w * jnp.exp(cum_last - cum)

    lane = lax.broadcasted_iota(jnp.int32, (C, GROUP), 1)
    low_pair = lane < 2 * HEAD_DIM
    even_head = (lane & HEAD_DIM) == 0
    by_pair = lambda x: jnp.concatenate([jnp.where(low_pair, x, 0.0), jnp.where(low_pair, 0.0, x)], axis=0)
    by_parity = lambda x: jnp.concatenate([jnp.where(even_head, x, 0.0), jnp.where(even_head, 0.0, x)], axis=0)
    ar, ac = _iota2((2 * C, 2 * C))
    t_pos, s_pos = ar & (C - 1), ac & (C - 1)
    differ = jnp.where(t_pos > s_pos, t_pos ^ s_pos, 0)
    gr, gc = _iota2((GROUP, GROUP))
    same_head = (gr >> 6) == (gc >> 6)
    groups = range(N_GROUPS)
    sl = [slice(gi * GROUP, (gi + 1) * GROUP) for gi in groups]

    att = [jnp.where(t_pos == s_pos, _mm(by_pair(q_w[:, s]), by_parity(kin_w[:, s]), NT, 1), 0.0) for s in sl]
    row = lax.broadcasted_iota(jnp.int32, cum.shape, 0)
    last = cum
    for l in range(CHUNK_LOG2):
        m = 1 << l
        in_low = (row & (2 * m - 1)) < m
        d = cum - jnp.where(in_low, last, pltpu.roll(last, m, 0))
        last = jnp.where(in_low, pltpu.roll(last, C - m, 0), last)
        ql = q_w * jnp.exp(jnp.minimum(d, 0.0))
        kl = kin_w * jnp.exp(jnp.minimum(-d, 0.0))
        level = (differ >> l) == 1
        att = [jnp.where(level, _mm(by_pair(ql[:, s]), by_parity(kl[:, s]), NT, 1), att[gi])
               for gi, s in enumerate(sl)]

    outs = []
    for gi, s in enumerate(sl):
        S = S_s[gi]
        v = iv_w[:, s]
        intra = _mm(att[gi], by_parity(v), NN, 1)
        outs.append(jnp.where(low_pair, intra[:C], intra[C:]) + _mm(qd[:, s], S, NT, 1))
        Z = _mm(v.T, kd[:, s], NN, 1)
        S_s[gi] = S * e_last[:, s] + jnp.where(same_head, Z, 0.0)

    res = _hgrn_post(_tall(jnp.concatenate(outs, axis=1)), ph[:, 3 * C_HEADS:], norm_g[...])
    o_ref[...] = res.reshape(SEQS, C, C_HEADS)

    @pl.when(j == pl.num_programs(1) - 1)
    def _():
        _group_state_store(S_s, sT_ref, transpose=True)


def _lru_seq_kernel(pl_ref, c0_ref, h0_ref, conv_w, conv_b, wa, ba, wx, bx, lam,
                    o_ref, hT_ref, xbuf_s, h_s, a_s, u_s, *, tm):
    j = pl.program_id(1)

    @pl.when(j == 0)
    def _():
        xbuf_s[5:8, :] = c0_ref[0]
        h_s[...] = h0_ref[0]

    blk = pl_ref[0]
    xb = blk[:, :C_LRU]
    xbuf_s[8:8 + tm, :] = xb
    x1 = xbuf_s[7:7 + tm, :]
    x2 = xbuf_s[6:6 + tm, :]
    x3 = xbuf_s[5:5 + tm, :]
    a, u = _lru_pre(xb, x1, x2, x3, conv_w[...], conv_b[...], wa[...], ba[...], wx[...], bx[...], lam[...])
    xbuf_s[5:8, :] = xb[tm - 3:tm, :]

    row8 = lax.broadcasted_iota(jnp.int32, a.shape, 0) & 7
    for d in (1, 2, 4):
        ok = row8 >= d
        u = jnp.where(ok, a * pltpu.roll(u, d, 0) + u, u)
        a = jnp.where(ok, a * pltpu.roll(a, d, 0), a)
    a_s[...] = a
    u_s[...] = u

    def group(gi, h):
        i = pl.multiple_of(gi * 8, 8)
        ht = u_s[pl.ds(i, 8), :] + a_s[pl.ds(i, 8), :] * h
        o_ref[0, pl.ds(i, 8), :] = ht
        return ht[7:8, :]

    h = lax.fori_loop(0, tm // 8, group, h_s[...])
    h_s[...] = h
    o_ref[0] = o_ref[0] * jax.nn.gelu(blk[:, C_LRU:])

    @pl.when(j == pl.num_programs(1) - 1)
    def _():
        hT_ref[0] = h


def _full_spec(a):
    return pl.BlockSpec(a.shape, lambda b, j: (0,) * a.ndim)


def _seq_block(c):
    return pl.BlockSpec((SEQS, CHUNK, c), lambda b, j: (b, j, 0))


_STATE_BLOCK = pl.BlockSpec((SEQS, N_HEADS, HEAD_DIM, HEAD_DIM), lambda b, j: (b, 0, 0, 0))
_GROUP_STATE = pltpu.VMEM((N_GROUPS, GROUP, GROUP), F32)


def _rwkv_seq(pr, p0, s0, prm, B, T):
    o, sT = pl.pallas_call(
        _rwkv_seq_kernel,
        grid=(B // SEQS, T // CHUNK),
        in_specs=[_seq_block(RWKV_COLS), pl.BlockSpec((SEQS, 1, RWKV_COLS), lambda b, j: (b, 0, 0)),
                  _STATE_BLOCK] + [_full_spec(a) for a in prm],
        out_specs=[_seq_block(C_HEADS), _STATE_BLOCK],
        out_shape=[jax.ShapeDtypeStruct((B, T, C_HEADS), F32), jax.ShapeDtypeStruct(s0.shape, F32)],
        scratch_shapes=[pltpu.VMEM((SEQS, 1, RWKV_COLS), F32), _GROUP_STATE],
        compiler_params=_cparams("parallel", "arbitrary"),
        name="rwkv_seq",
    )(pr.reshape(B, T, RWKV_COLS), p0.reshape(B, 1, RWKV_COLS), s0, *prm)
    return o.reshape(B * T, C_HEADS), sT


def _hgrn_seq(ph, s0, lb, norm_g, B, T):
    o, sT = pl.pallas_call(
        _hgrn_seq_kernel,
        grid=(B // SEQS, T // CHUNK),
        in_specs=[_seq_block(HGRN_COLS), _STATE_BLOCK, _full_spec(lb), _full_spec(norm_g)],
        out_specs=[_seq_block(C_HEADS), _STATE_BLOCK],
        out_shape=[jax.ShapeDtypeStruct((B, T, C_HEADS), F32), jax.ShapeDtypeStruct(s0.shape, F32)],
        scratch_shapes=[_GROUP_STATE],
        compiler_params=_cparams("parallel", "arbitrary"),
        name="hgrn_seq",
    )(ph.reshape(B, T, HGRN_COLS), s0, lb, norm_g)
    return o.reshape(B * T, C_HEADS), sT


def _lru_seq(plru, conv0, h0, prm, B, T):
    tm = min(T, 256)
    seq = lambda c: pl.BlockSpec((1, tm, c), lambda b, j: (b, j, 0))
    hblk = pl.BlockSpec((1, 1, C_LRU), lambda b, j: (b, 0, 0))
    o, hT = pl.pallas_call(
        functools.partial(_lru_seq_kernel, tm=tm),
        grid=(B, T // tm),
        in_specs=[seq(LRU_COLS), pl.BlockSpec((1, CONV_WIDTH - 1, C_LRU), lambda b, j: (b, 0, 0)), hblk]
                 + [_full_spec(a) for a in prm],
        out_specs=[seq(C_LRU), hblk],
        out_shape=[jax.ShapeDtypeStruct((B, T, C_LRU), F32), jax.ShapeDtypeStruct((B, 1, C_LRU), F32)],
        scratch_shapes=[pltpu.VMEM((tm + 8, C_LRU), F32), pltpu.VMEM((1, C_LRU), F32),
                        pltpu.VMEM((tm, C_LRU), F32), pltpu.VMEM((tm, C_LRU), F32)],
        compiler_params=_cparams("parallel", "arbitrary"),
        name="lru_seq",
    )(plru.reshape(B, T, LRU_COLS), conv0, h0.reshape(B, 1, C_LRU), *prm)
    return o.reshape(B * T, C_LRU), hT.reshape(B, C_LRU)


def _block_diag(w):
    out = jnp.zeros((C_LRU, C_LRU), w.dtype)
    n = C_LRU // LRU_BLOCKS
    for i in range(LRU_BLOCKS):
        out = out.at[i * n:(i + 1) * n, i * n:(i + 1) * n].set(w[i])
    return out


def _prepare_params(p):
    row = lambda a: a.reshape(1, -1)
    zpad = jnp.zeros((W_LORA, C_HEADS), F32)
    layers = []
    for l in range(DEPTH):
        pre = (row(p['mu_shift'][l]), row(p['rwkv_w0'][l]),
               jnp.concatenate([p['rwkv_w_up'][l], zpad], 0).astype(BF16),
               row(p['rwkv_a0'][l]),
               jnp.concatenate([zpad, p['rwkv_a_up'][l]], 0).astype(BF16),
               p['rwkv_g_up'][l].astype(BF16), row(p['rwkv_k_k'][l]), row(p['rwkv_k_a'][l]),
               None,
               p['lru_conv_w'][l], row(p['lru_conv_b'][l]),
               _block_diag(p['lru_wa'][l]).astype(BF16), row(p['lru_ba'][l]),
               _block_diag(p['lru_wx'][l]).astype(BF16), row(p['lru_bx'][l]), row(p['lru_lambda'][l]))
        layers.append(dict(
            norm1_g=row(p['norm1_g'][l]), w_in=p['w_in'][l].astype(BF16), pre=pre,
            r_k=row(p['rwkv_r_k'][l]), ln_w=row(p['rwkv_ln_w'][l]), ln_b=row(p['rwkv_ln_b'][l]),
            hgrn_norm_g=row(p['hgrn_norm_g'][l]),
            w_out=p['w_out'][l].astype(BF16), norm2_g=row(p['norm2_g'][l]),
            mlp_w1=p['mlp_w1'][l].astype(BF16), mlp_w2=p['mlp_w2'][l].astype(BF16)))
    return layers


def _trunk(x, wkv, shift, hgrn, lru, conv, layers, lb_all, final_g):
    B, T, _ = x.shape
    m = B * T
    assert T == 1 or (T % CHUNK == 0 and B % SEQS == 0), "sequence kernels tile (B, T) by (SEQS, CHUNK)"
    x = x.reshape(m, D_MODEL)
    heads = lambda a: a.reshape(B, T, -1, N_HEADS, HEAD_DIM)
    n_wkv, n_shift, n_hgrn, n_lru, n_conv = [], [], [], [], []
    for l, lp in enumerate(layers):
        xn, pr, ph, plru = _norm_proj(x, lp['norm1_g'], lp['w_in'])
        p0 = _proj(shift[l], lp['w_in'][:, :RWKV_COLS])
        prm = list(lp['pre'])
        prm[8] = lb_all[l:l + 1]
        if T == 1:
            r, w, k, v, kk, ka, g, f, kin, a, u = _mixer_pre_step(pr, ph, plru, p0, conv[l], prm)
            wo, S_r = _head_scan(_rwkv_scan_kernel, "rwkv_scan",
                                 [(heads(t), 0) for t in (r, w, k, v, kk, ka)], wkv[l], B, T)
            ph4 = heads(ph)
            ho, S_h = _head_scan(_hgrn_scan_kernel, "hgrn_scan",
                                 [(ph4, 0), (heads(f), 0), (heads(kin), 0), (ph4, 2)], hgrn[l], B, T)
            lh, h_l = _lru_scan(a, u, lru[l], B, T)
            mix = _mix_post(wo.reshape(m, C_HEADS), r, k, v, g, lp['r_k'], lp['ln_w'], lp['ln_b'],
                            ho.reshape(m, C_HEADS), ph, lp['hgrn_norm_g'], lh, plru)
            o_r, o_h, o_l = mix[:, :C_HEADS], mix[:, C_HEADS:2 * C_HEADS], mix[:, 2 * C_HEADS:]
        else:
            o_r, S_r = _rwkv_seq(pr, p0, wkv[l], prm[:8] + [lp['r_k'], lp['ln_w'], lp['ln_b']], B, T)
            o_h, S_h = _hgrn_seq(ph, hgrn[l], prm[8], lp['hgrn_norm_g'], B, T)
            o_l, h_l = _lru_seq(plru, conv[l], lru[l], prm[9:], B, T)
        x = _out_mlp(x, o_r, o_h, o_l, lp['w_out'], lp['norm2_g'], lp['mlp_w1'], lp['mlp_w2'])
        xb = plru.reshape(B, T, LRU_COLS)[:, :, :C_LRU]
        hist = jnp.concatenate([conv[l], xb[:, max(T - (CONV_WIDTH - 1), 0):]], axis=1)
        n_wkv.append(S_r)
        n_shift.append(xn.reshape(B, T, D_MODEL)[:, -1])
        n_hgrn.append(S_h)
        n_lru.append(h_l)
        n_conv.append(hist[:, -(CONV_WIDTH - 1):])
    y = _final_norm(x, final_g).reshape(B, T, D_MODEL)
    return (y, jnp.stack(n_wkv), jnp.stack(n_shift), jnp.stack(n_hgrn), jnp.stack(n_lru), jnp.stack(n_conv))


def kernel(x_prompt, x_sample, state_wkv, state_shift, state_hgrn, state_lru, state_conv, norm1_g, w_in, mu_shift, rwkv_w0, rwkv_w_up, rwkv_a0, rwkv_a_up, rwkv_g_up, rwkv_k_k, rwkv_k_a, rwkv_r_k, rwkv_ln_w, rwkv_ln_b, hgrn_lb, hgrn_norm_g, lru_conv_w, lru_conv_b, lru_wa, lru_ba, lru_wx, lru_bx, lru_lambda, w_out, norm2_g, mlp_w1, mlp_w2, final_g):
    prm = dict(norm1_g=norm1_g, w_in=w_in, mu_shift=mu_shift, rwkv_w0=rwkv_w0, rwkv_w_up=rwkv_w_up,
               rwkv_a0=rwkv_a0, rwkv_a_up=rwkv_a_up, rwkv_g_up=rwkv_g_up, rwkv_k_k=rwkv_k_k,
               rwkv_k_a=rwkv_k_a, rwkv_r_k=rwkv_r_k.reshape(DEPTH, C_HEADS), rwkv_ln_w=rwkv_ln_w,
               rwkv_ln_b=rwkv_ln_b, hgrn_norm_g=hgrn_norm_g, lru_conv_w=lru_conv_w,
               lru_conv_b=lru_conv_b, lru_wa=lru_wa, lru_ba=lru_ba, lru_wx=lru_wx, lru_bx=lru_bx,
               lru_lambda=lru_lambda, w_out=w_out, norm2_g=norm2_g, mlp_w1=mlp_w1, mlp_w2=mlp_w2)
    layers = _prepare_params(prm)
    lb_all = _hgrn_lower_bounds(hgrn_lb)
    fg = final_g.reshape(1, D_MODEL)
    Bp = x_prompt.shape[0]
    dt = x_prompt.dtype
    z_wkv = jnp.zeros((DEPTH, Bp, N_HEADS, HEAD_DIM, HEAD_DIM), dt)
    z_shift = jnp.zeros((DEPTH, Bp, D_MODEL), dt)
    z_hgrn = jnp.zeros((DEPTH, Bp, N_HEADS, HEAD_DIM, HEAD_DIM), dt)
    z_lru = jnp.zeros((DEPTH, Bp, C_LRU), dt)
    z_conv = jnp.zeros((DEPTH, Bp, CONV_WIDTH - 1, C_LRU), dt)
    y_p, p_wkv, p_shift, p_hgrn, p_lru, p_conv = _trunk(x_prompt, z_wkv, z_shift, z_hgrn, z_lru, z_conv,
                                                         layers, lb_all, fg)
    y_s, s_wkv, s_shift, s_hgrn, s_lru, s_conv = _trunk(x_sample, state_wkv, state_shift, state_hgrn,
                                                         state_lru, state_conv, layers, lb_all, fg)
    return (y_p, y_s, p_wkv, p_shift, p_hgrn, p_lru, p_conv, s_wkv, s_shift, s_hgrn, s_lru, s_conv)
```

```python
import functools

import jax
import jax.numpy as jnp
from jax import lax
from jax.experimental import pallas as pl
from jax.experimental.pallas import tpu as pltpu

F32 = jnp.float32
BF16 = jnp.bfloat16

D_MODEL = 1024
DEPTH = 4
HEAD_DIM = 64
N_HEADS = 6
C_HEADS = N_HEADS * HEAD_DIM
C_LRU = 256
LRU_BLOCKS = 4
CONV_WIDTH = 4
LRU_C = 8.0
W_LORA = 64
A_LORA = 64
G_LORA = 128
RWKV_COLS = 3 * C_HEADS + W_LORA + A_LORA + G_LORA
HGRN_COLS = 4 * C_HEADS
LRU_COLS = 2 * C_LRU
C_IN = RWKV_COLS + HGRN_COLS + LRU_COLS
D_FF = 4 * D_MODEL
NORM_EPS = 1e-6
GN_EPS = 64e-5

LANES = 128
VMEM_LIMIT_BYTES = 48 * 1024 * 1024


def _cparams(*sem):
    return pltpu.CompilerParams(dimension_semantics=sem, vmem_limit_bytes=VMEM_LIMIT_BYTES)


def _dot(a, b):
    return jnp.dot(a.astype(BF16), b.astype(BF16), preferred_element_type=F32)


def _rmsnorm(x, g):
    return x * lax.rsqrt(jnp.mean(x * x, axis=-1, keepdims=True) + NORM_EPS) * g


def _softplus(x):
    return jnp.maximum(x, 0.0) + jnp.log1p(jnp.exp(-jnp.abs(x)))


def _head_sum(x):
    m = x.shape[0]
    lo = lax.broadcasted_iota(jnp.int32, (m, LANES), 1) < HEAD_DIM
    outs = []
    for p in range(x.shape[1] // LANES):
        xp = x[:, p * LANES:(p + 1) * LANES]
        s_lo = jnp.sum(jnp.where(lo, xp, 0.0), axis=1, keepdims=True)
        s_hi = jnp.sum(jnp.where(lo, 0.0, xp), axis=1, keepdims=True)
        outs.append(jnp.where(lo, s_lo, s_hi))
    return jnp.concatenate(outs, axis=1)


def _norm_proj_kernel(x_ref, g_ref, w_ref, pr_ref, ph_ref, pl_ref):
    xn = _rmsnorm(x_ref[...], g_ref[...])
    p = jnp.dot(xn.astype(BF16), w_ref[...], preferred_element_type=F32)
    pr_ref[...] = p[:, :RWKV_COLS]
    ph_ref[...] = p[:, RWKV_COLS:RWKV_COLS + HGRN_COLS]
    pl_ref[...] = p[:, RWKV_COLS + HGRN_COLS:]


def _norm_proj(x, g, w):
    m = x.shape[0]
    tm = min(m, 256)
    row = lambda c: pl.BlockSpec((tm, c), lambda i: (i, 0))
    full = lambda a: pl.BlockSpec(a.shape, lambda i: (0,) * a.ndim)
    return pl.pallas_call(
        _norm_proj_kernel,
        grid=(m // tm,),
        in_specs=[row(D_MODEL), full(g), full(w)],
        out_specs=[row(RWKV_COLS), row(HGRN_COLS), row(LRU_COLS)],
        out_shape=[jax.ShapeDtypeStruct((m, c), F32) for c in (RWKV_COLS, HGRN_COLS, LRU_COLS)],
        compiler_params=_cparams("parallel"),
        name="norm_proj",
    )(x, g, w)


def _proj_kernel(x_ref, w_ref, o_ref):
    o_ref[...] = jnp.dot(x_ref[...].astype(BF16), w_ref[...], preferred_element_type=F32)


def _proj(x, w):
    return pl.pallas_call(
        _proj_kernel,
        out_shape=jax.ShapeDtypeStruct((x.shape[0], w.shape[1]), F32),
        compiler_params=pltpu.CompilerParams(vmem_limit_bytes=VMEM_LIMIT_BYTES),
        name="shift_proj",
    )(x, w)


def _out_mlp_kernel(x_ref, or_ref, oh_ref, ol_ref, wo_ref, g2_ref, w1_ref, w2_ref, o_ref, x1_s, xn_s, acc_s):
    j = pl.program_id(1)

    @pl.when(j == 0)
    def _():
        mix = jnp.concatenate([or_ref[...], oh_ref[...], ol_ref[...]], axis=1)
        x1 = x_ref[...] + jnp.dot(mix.astype(BF16), wo_ref[...], preferred_element_type=F32)
        x1_s[...] = x1
        xn_s[...] = _rmsnorm(x1, g2_ref[...]).astype(BF16)
        acc_s[...] = jnp.zeros_like(acc_s)

    h = jnp.dot(xn_s[...], w1_ref[...], preferred_element_type=F32)
    h = jnp.square(jnp.maximum(h, 0.0))
    acc_s[...] += jnp.dot(h.astype(BF16), w2_ref[...], preferred_element_type=F32)

    @pl.when(j == pl.num_programs(1) - 1)
    def _():
        o_ref[...] = x1_s[...] + acc_s[...]


def _out_mlp(x, o_r, o_h, o_l, wo, g2, w1, w2):
    m = x.shape[0]
    tm = min(m, 512)
    tf = 2048
    row = pl.BlockSpec((tm, D_MODEL), lambda i, j: (i, 0))
    part = lambda c: pl.BlockSpec((tm, c), lambda i, j: (i, 0))
    return pl.pallas_call(
        _out_mlp_kernel,
        grid=(m // tm, D_FF // tf),
        in_specs=[row, part(C_HEADS), part(C_HEADS), part(C_LRU),
                  pl.BlockSpec((D_MODEL, D_MODEL), lambda i, j: (0, 0)),
                  pl.BlockSpec((1, D_MODEL), lambda i, j: (0, 0)),
                  pl.BlockSpec((D_MODEL, tf), lambda i, j: (0, j)),
                  pl.BlockSpec((tf, D_MODEL), lambda i, j: (j, 0))],
        out_specs=row,
        out_shape=jax.ShapeDtypeStruct((m, D_MODEL), F32),
        scratch_shapes=[pltpu.VMEM((tm, D_MODEL), F32), pltpu.VMEM((tm, D_MODEL), BF16),
                        pltpu.VMEM((tm, D_MODEL), F32)],
        compiler_params=_cparams("parallel", "arbitrary"),
        name="out_mlp",
    )(x, o_r, o_h, o_l, wo, g2, w1, w2)


def _rmsnorm_rows_kernel(x_ref, g_ref, o_ref):
    o_ref[...] = _rmsnorm(x_ref[...], g_ref[...])


def _rmsnorm_rows(x, g):
    m = x.shape[0]
    tm = min(m, 512)
    row = pl.BlockSpec((tm, D_MODEL), lambda i: (i, 0))
    return pl.pallas_call(
        _rmsnorm_rows_kernel,
        grid=(m // tm,),
        in_specs=[row, pl.BlockSpec((1, D_MODEL), lambda i: (0, 0))],
        out_specs=row,
        out_shape=jax.ShapeDtypeStruct((m, D_MODEL), F32),
        compiler_params=_cparams("parallel"),
        name="rmsnorm_rows",
    )(x, g)


def _lb_kernel(p_ref, o_ref):
    p = p_ref[...]
    e = jnp.exp(p - jnp.max(p, axis=0, keepdims=True))
    s = e / jnp.sum(e, axis=0, keepdims=True)
    acc = jnp.zeros_like(s[0:1])
    for l in range(DEPTH):
        if l > 0:
            acc = acc + s[l:l + 1]
        o_ref[l:l + 1, :] = acc


def _hgrn_lower_bounds(hgrn_lb):
    return pl.pallas_call(_lb_kernel, out_shape=jax.ShapeDtypeStruct(hgrn_lb.shape, F32),
                          name="hgrn_lower_bounds")(hgrn_lb)


def _rwkv_pre(pr, prev, mu, w0, w_up, a0, a_up, g_up, k_k, k_a):
    c = C_HEADS
    pm = pr + mu * (prev - pr)
    r = pm[:, :c]
    k0 = pm[:, c:2 * c]
    v = pm[:, 2 * c:3 * c]
    xwa = pm[:, 3 * c:3 * c + W_LORA + A_LORA]
    xg = pm[:, 3 * c + W_LORA + A_LORA:]
    w_log = -_softplus(-(w0 + _dot(jnp.tanh(xwa), w_up))) - 0.5
    log_decay = -jnp.exp(w_log)
    a = jax.nn.sigmoid(a0 + _dot(xwa, a_up))
    g = _dot(jax.nn.sigmoid(xg), g_up)
    kk = k0 * k_k
    kk = kk * lax.rsqrt(jnp.maximum(_head_sum(kk * kk), 1e-24))
    k = k0 * (1.0 + (a - 1.0) * k_a)
    return r, log_decay, k, v, kk, kk * a, g


def _hgrn_pre(fpre, lb):
    sg = jax.nn.sigmoid(fpre)
    f = lb + (1.0 - lb) * sg
    kin = (1.0 - lb) * jax.nn.sigmoid(-fpre)
    return f, kin


def _lru_pre(xb, x1, x2, x3, conv_w, conv_b, wa, ba, wx, bx, lam):
    xc = conv_b + (((x3 * conv_w[0:1] + x2 * conv_w[1:2]) + x1 * conv_w[2:3]) + xb * conv_w[3:4])
    r = jax.nn.sigmoid(_dot(xc, wa) + ba)
    i = jax.nn.sigmoid(_dot(xc, wx) + bx)
    log_a = -LRU_C * r * _softplus(-lam)
    a = jnp.exp(log_a)
    one_minus_a2 = -jnp.tanh(log_a) * (jnp.exp(2.0 * log_a) + 1.0)
    u = xc * i * jnp.sqrt(jnp.maximum(one_minus_a2, 1e-12))
    return a, u


_N_PRE_PARAMS = 16


def _pre_compute(pr, prev, ph, xb, x1, x2, x3, prm, outs):
    (mu, w0, w_up, a0, a_up, g_up, k_k, k_a, lb, conv_w, conv_b, wa, ba, wx, bx, lam) = prm
    r_o, w_o, k_o, v_o, kk_o, ka_o, g_o, f_o, kin_o, a_o, u_o = outs
    r, lw, k, v, kk, ka, g = _rwkv_pre(pr, prev, mu[...], w0[...], w_up[...], a0[...], a_up[...],
                                       g_up[...], k_k[...], k_a[...])
    w = jnp.exp(lw)
    f, kin = _hgrn_pre(ph[:, C_HEADS:2 * C_HEADS], lb[...])
    a, u = _lru_pre(xb, x1, x2, x3, conv_w[...], conv_b[...], wa[...], ba[...], wx[...], bx[...], lam[...])
    return ((r_o, r), (w_o, w), (k_o, k), (v_o, v), (kk_o, kk), (ka_o, ka), (g_o, g),
            (f_o, f), (kin_o, kin), (a_o, a), (u_o, u))


def _pre_step_kernel(*refs):
    pr_ref, ph_ref, pl_ref, p0_ref, c0_ref, c1_ref, c2_ref = refs[:7]
    prm = refs[7:7 + _N_PRE_PARAMS]
    outs = refs[7 + _N_PRE_PARAMS:]
    xb = pl_ref[...][:, :C_LRU]
    res = _pre_compute(pr_ref[...], p0_ref[...], ph_ref[...], xb, c2_ref[...], c1_ref[...], c0_ref[...],
                       prm, outs)
    for o_ref, val in res:
        o_ref[...] = val


def _mixer_pre_step(pr, ph, plru, p0, conv0, prm):
    widths = (C_HEADS,) * 9 + (C_LRU,) * 2
    return pl.pallas_call(
        _pre_step_kernel,
        out_shape=[jax.ShapeDtypeStruct((pr.shape[0], c), F32) for c in widths],
        compiler_params=pltpu.CompilerParams(vmem_limit_bytes=VMEM_LIMIT_BYTES),
        name="mixer_pre_step",
    )(pr, ph, plru, p0, conv0[:, 0], conv0[:, 1], conv0[:, 2], *prm)


def _eye():
    return (lax.broadcasted_iota(jnp.int32, (HEAD_DIM, HEAD_DIM), 0)
            == lax.broadcasted_iota(jnp.int32, (HEAD_DIM, HEAD_DIM), 1))


def _to_col(row, eye):
    return jnp.sum(jnp.where(eye, row, 0.0), axis=1, keepdims=True)


def _to_row(col, eye):
    return jnp.sum(jnp.where(eye, col, 0.0), axis=0, keepdims=True)


def _rwkv_scan_kernel(r_ref, w_ref, k_ref, v_ref, kk_ref, ka_ref, s0_ref, o_ref, sT_ref, S_s, *, bb, tc):
    j = pl.program_id(1)

    @pl.when(j == 0)
    def _():
        S_s[...] = s0_ref[...]

    eye = _eye()

    def step(t, carry):
        units = [(b, h) for b in range(bb) for h in range(N_HEADS)]
        idx = [(b, t, pl.ds(h, 1), slice(None)) for b, h in units]
        S = [S_s[b, h] for b, h in units]
        sa = [-jnp.sum(Si * kk_ref[i], axis=1, keepdims=True) for Si, i in zip(S, idx)]
        vc = [_to_col(v_ref[i], eye) for i in idx]
        S = [Si * w_ref[i] + sai * ka_ref[i] + vci * k_ref[i] for Si, sai, vci, i in zip(S, sa, vc, idx)]
        for (b, h), Si in zip(units, S):
            S_s[b, h] = Si
        oc = [jnp.sum(Si * r_ref[i], axis=1, keepdims=True) for Si, i in zip(S, idx)]
        for i, oci in zip(idx, oc):
            o_ref[i] = _to_row(oci, eye)
        return carry

    lax.fori_loop(0, tc, step, 0)

    @pl.when(j == pl.num_programs(1) - 1)
    def _():
        sT_ref[...] = S_s[...]


def _hgrn_scan_kernel(q_ref, f_ref, kin_ref, iv_ref, s0_ref, o_ref, sT_ref, S_s, *, bb, tc):
    j = pl.program_id(1)

    @pl.when(j == 0)
    def _():
        S_s[...] = s0_ref[...]

    eye = _eye()

    def step(t, carry):
        units = [(b, h) for b in range(bb) for h in range(N_HEADS)]
        idx = [(b, t, pl.ds(h, 1), slice(None)) for b, h in units]
        cols = [(_to_col(f_ref[i], eye), _to_col(kin_ref[i], eye), _to_col(q_ref[i], eye)) for i in idx]
        for (b, h), i, (fc, kc, qc) in zip(units, idx, cols):
            S = fc * S_s[b, h] + kc * iv_ref[i]
            S_s[b, h] = S
            o_ref[i] = jnp.sum(S * qc, axis=0, keepdims=True)
        return carry

    lax.fori_loop(0, tc, step, 0)

    @pl.when(j == pl.num_programs(1) - 1)
    def _():
        sT_ref[...] = S_s[...]


def _lru_scan_kernel(a_ref, u_ref, h0_ref, o_ref, hT_ref, h_s, *, bb, tc):
    j = pl.program_id(1)

    @pl.when(j == 0)
    def _():
        h_s[...] = h0_ref[...]

    def step(t, carry):
        for b in range(bb):
            idx = (b, pl.ds(t, 1), slice(None))
            h = a_ref[idx] * h_s[b] + u_ref[idx]
            h_s[b] = h
            o_ref[idx] = h
        return carry

    lax.fori_loop(0, tc, step, 0)

    @pl.when(j == pl.num_programs(1) - 1)
    def _():
        hT_ref[...] = h_s[...]


def _scan_tiles(B, T):
    bb = 8 if T == 1 else 4
    tc = min(T, 32)
    return bb, tc


def _head_scan(kernel, name, seqs, s0, B, T):
    bb, tc = _scan_tiles(B, T)
    hblk = lambda g: pl.BlockSpec((bb, tc, None, N_HEADS, HEAD_DIM), lambda i, j: (i, j, g, 0, 0))
    sblk = pl.BlockSpec((bb, N_HEADS, HEAD_DIM, HEAD_DIM), lambda i, j: (i, 0, 0, 0))
    return pl.pallas_call(
        functools.partial(kernel, bb=bb, tc=tc),
        grid=(B // bb, T // tc),
        in_specs=[hblk(g) for _, g in seqs] + [sblk],
        out_specs=[hblk(0), sblk],
        out_shape=[jax.ShapeDtypeStruct((B, T, 1, N_HEADS, HEAD_DIM), F32),
                   jax.ShapeDtypeStruct(s0.shape, F32)],
        scratch_shapes=[pltpu.VMEM((bb, N_HEADS, HEAD_DIM, HEAD_DIM), F32)],
        compiler_params=_cparams("parallel", "arbitrary"),
        name=name,
    )(*[a for a, _ in seqs], s0)


def _lru_scan(a, u, h0, B, T):
    bb, tc = _scan_tiles(B, T)
    blk = pl.BlockSpec((bb, tc, C_LRU), lambda i, j: (i, j, 0))
    hblk = pl.BlockSpec((bb, 1, C_LRU), lambda i, j: (i, 0, 0))
    o, hT = pl.pallas_call(
        functools.partial(_lru_scan_kernel, bb=bb, tc=tc),
        grid=(B // bb, T // tc),
        in_specs=[blk, blk, hblk],
        out_specs=[blk, hblk],
        out_shape=[jax.ShapeDtypeStruct((B, T, C_LRU), F32), jax.ShapeDtypeStruct((B, 1, C_LRU), F32)],
        scratch_shapes=[pltpu.VMEM((bb, 1, C_LRU), F32)],
        compiler_params=_cparams("parallel", "arbitrary"),
        name="lru_scan",
    )(a.reshape(B, T, C_LRU), u.reshape(B, T, C_LRU), h0.reshape(B, 1, C_LRU))
    return o.reshape(B * T, C_LRU), hT.reshape(B, C_LRU)


def _rwkv_post(out, r, k, v, g, r_k, ln_w, ln_b):
    inv_n = 1.0 / HEAD_DIM
    mean = _head_sum(out) * inv_n
    d = out - mean
    var = _head_sum(d * d) * inv_n
    gn = d * lax.rsqrt(var + GN_EPS) * ln_w + ln_b
    bonus = _head_sum(r * k * r_k) * v
    return (gn + bonus) * g


def _hgrn_post(o, gate, norm_g):
    o = o * lax.rsqrt(_head_sum(o * o) * (1.0 / HEAD_DIM) + NORM_EPS)
    return o * norm_g * jax.nn.silu(gate)


def _mix_post_kernel(wo_ref, r_ref, k_ref, v_ref, g_ref, rk_ref, lnw_ref, lnb_ref,
                     ho_ref, hg_ref, hng_ref, lh_ref, lg_ref, mix_ref):
    mix_ref[:, :C_HEADS] = _rwkv_post(wo_ref[...], r_ref[...], k_ref[...], v_ref[...], g_ref[...],
                                      rk_ref[...], lnw_ref[...], lnb_ref[...])
    mix_ref[:, C_HEADS:2 * C_HEADS] = _hgrn_post(ho_ref[...], hg_ref[...], hng_ref[...])
    mix_ref[:, 2 * C_HEADS:] = lh_ref[...] * jax.nn.gelu(lg_ref[...])


def _mix_post(wo, r, k, v, g, r_k, ln_w, ln_b, ho, ph, hng, lh, plru):
    m = wo.shape[0]
    tm = min(m, 512)
    hrow = pl.BlockSpec((tm, C_HEADS), lambda i: (i, 0))
    prow = lambda c: pl.BlockSpec((1, c), lambda i: (0, 0))
    return pl.pallas_call(
        _mix_post_kernel,
        grid=(m // tm,),
        in_specs=[hrow, hrow, hrow, hrow, hrow, prow(C_HEADS), prow(C_HEADS), prow(C_HEADS),
                  hrow, pl.BlockSpec((tm, C_HEADS), lambda i: (i, 3)), prow(C_HEADS),
                  pl.BlockSpec((tm, C_LRU), lambda i: (i, 0)),
                  pl.BlockSpec((tm, C_LRU), lambda i: (i, 1))],
        out_specs=pl.BlockSpec((tm, D_MODEL), lambda i: (i, 0)),
        out_shape=jax.ShapeDtypeStruct((m, D_MODEL), F32),
        compiler_params=_cparams("parallel"),
        name="mix_post",
    )(wo, r, k, v, g, r_k, ln_w, ln_b, ho, ph, hng, lh, plru)


CHUNK = 64
CHUNK_LOG2 = 6
SEQS = 4
GROUP_HEADS = 4
GROUP = GROUP_HEADS * HEAD_DIM
N_GROUPS = SEQS * N_HEADS // GROUP_HEADS

NN = ((1,), (0,))
NT = ((1,), (1,))


def _mm(a, b, dims, passes):
    dn = (dims, ((), ()))
    if passes == 6:
        return lax.dot_general(a, b, dn, precision=lax.Precision.HIGHEST, preferred_element_type=F32)
    ah, bh = a.astype(BF16), b.astype(BF16)
    out = lax.dot_general(ah, bh, dn, preferred_element_type=F32)
    if passes == 3:
        al = (a - ah.astype(F32)).astype(BF16)
        bl = (b - bh.astype(F32)).astype(BF16)
        out = (out + lax.dot_general(ah, bl, dn, preferred_element_type=F32)
               + lax.dot_general(al, bh, dn, preferred_element_type=F32))
    return out


def _iota2(shape):
    return lax.broadcasted_iota(jnp.int32, shape, 0), lax.broadcasted_iota(jnp.int32, shape, 1)


def _head_of(group, i):
    hw = group * GROUP_HEADS + i
    return hw // N_HEADS, hw % N_HEADS


def _group_state_load(s_ref, S_s, transpose):
    z = jnp.zeros((HEAD_DIM, HEAD_DIM), F32)
    for gi in range(N_GROUPS):
        rows = []
        for i in range(GROUP_HEADS):
            b, h = _head_of(gi, i)
            blk = s_ref[b, h].T if transpose else s_ref[b, h]
            rows.append(jnp.concatenate([blk if i2 == i else z for i2 in range(GROUP_HEADS)], axis=1))
        S_s[gi] = jnp.concatenate(rows, axis=0)


def _group_state_store(S_s, s_ref, transpose):
    for gi in range(N_GROUPS):
        Sg = S_s[gi]
        for i in range(GROUP_HEADS):
            b, h = _head_of(gi, i)
            blk = Sg[i * HEAD_DIM:(i + 1) * HEAD_DIM, i * HEAD_DIM:(i + 1) * HEAD_DIM]
            s_ref[b, h] = blk.T if transpose else blk


def _wide(x):
    return jnp.concatenate([x[b * CHUNK:(b + 1) * CHUNK] for b in range(SEQS)], axis=1)


def _tall(x):
    return jnp.concatenate([x[:, b * C_HEADS:(b + 1) * C_HEADS] for b in range(SEQS)], axis=0)


def _seq_cumsum(x):
    tr, tc = _iota2((SEQS * CHUNK, SEQS * CHUNK))
    tri = ((tr >= tc) & ((tr >> CHUNK_LOG2) == (tc >> CHUNK_LOG2))).astype(F32)
    return _mm(tri, x, NN, 6)


def _lane_head():
    return lax.broadcasted_iota(jnp.int32, (CHUNK, GROUP), 1) >> 6


def _by_head(x, lane_head):
    return jnp.concatenate([jnp.where(lane_head == i, x, 0.0) for i in range(GROUP_HEADS)], axis=0)


def _rep(x):
    return jnp.concatenate([x] * GROUP_HEADS, axis=0)


def _pair_masks():
    lane = lax.broadcasted_iota(jnp.int32, (CHUNK, GROUP), 1)
    return lane < 2 * HEAD_DIM, (lane & HEAD_DIM) == 0


def _split_rows(x, mask):
    return jnp.concatenate([jnp.where(mask, x, 0.0), jnp.where(mask, 0.0, x)], axis=0)


def _pick(x, lane_head):
    out = x[:CHUNK]
    for i in range(1, GROUP_HEADS):
        out = jnp.where(lane_head == i, x[i * CHUNK:(i + 1) * CHUNK], out)
    return out


def _unit_lower_inverse(Ls):
    r, c = _iota2(Ls[0].shape)
    eye = (r == c).astype(F32)
    diag_blk = (r >> 4) == (c >> 4)
    mm = lambda xs, ys: [_mm(x, y, NN, 1) for x, y in zip(xs, ys)]
    Ld = [jnp.where(diag_blk, L, 0.0) for L in Ls]
    N = [jnp.where(diag_blk, 0.0, L) for L in Ls]
    L2 = mm(Ld, Ld)
    L4 = mm(L2, L2)
    P1 = mm([eye - x for x in Ld], [eye + x for x in L2])
    L8 = mm(L4, L4)
    P2 = mm([eye + x for x in L4], [eye + x for x in L8])
    Dinv = mm(P1, P2)
    M = mm(Dinv, N)
    M2 = mm(M, M)
    X = mm([eye - x for x in M], [eye + x for x in M2])
    return mm(X, Dinv)


def _rwkv_seq_kernel(pr_ref, p0_ref, s0_ref, mu, w0, w_up, a0, a_up, g_up, k_k, k_a, r_k, ln_w, ln_b,
                     o_ref, sT_ref, carry_s, S_s):
    j = pl.program_id(1)
    C = CHUNK

    @pl.when(j == 0)
    def _():
        carry_s[...] = p0_ref[...]
        _group_state_load(s0_ref, S_s, transpose=False)

    pr = pr_ref[...].reshape(SEQS * C, RWKV_COLS)
    prow = lax.broadcasted_iota(jnp.int32, pr.shape, 0)
    prev = pltpu.roll(pr, 1, 0)
    for b in range(SEQS):
        prev = jnp.where(prow == b * C, carry_s[b], prev)
        carry_s[b] = pr[(b + 1) * C - 1:(b + 1) * C, :]

    r, lw, k, v, kk, kb, g = _rwkv_pre(pr, prev, mu[...], w0[...], w_up[...], a0[...], a_up[...],
                                       g_up[...], k_k[...], k_a[...])
    cum_w, lw_w = _wide(_seq_cumsum(lw)), _wide(lw)
    r_w, k_w, v_w, kk_w, kb_w = _wide(r), _wide(k), _wide(v), _wide(kk), _wide(kb)
    cum_last = cum_w[C - 1:C, :]
    e_inv = jnp.exp(-cum_w)
    e_rel = jnp.exp(cum_last - cum_w)
    e_last = jnp.exp(cum_last)
    Kg = kk_w * jnp.exp(cum_w - lw_w)
    Rg = r_w * jnp.exp(cum_w)
    Ki, Bi = k_w * e_inv, kb_w * e_inv
    Kt, Bt = k_w * e_rel, kb_w * e_rel

    lane_head = _lane_head()
    low_pair, even_head = _pair_masks()
    by_pair = lambda x: _split_rows(x, low_pair)
    by_parity = lambda x: _split_rows(x, even_head)
    gr, gc = _iota2((GROUP, GROUP))
    same_head = (gr >> 6) == (gc >> 6)
    strict_bd = same_head & ((gr & (HEAD_DIM - 1)) > (gc & (HEAD_DIM - 1)))
    ar, ac = _iota2((2 * C, 2 * C))
    strict = (ar & (C - 1)) > (ac & (C - 1))
    incl = (ar & (C - 1)) >= (ac & (C - 1))

    groups = range(N_GROUPS)
    sl = [slice(gi * GROUP, (gi + 1) * GROUP) for gi in groups]
    Akb = [jnp.where(strict_bd, _mm(_by_head(Kg[:, s], lane_head), _rep(Bi[:, s]), NT, 1), 0.0) for s in sl]
    P = [_mm(jnp.concatenate([by_pair(Kg[:, s]), by_pair(Rg[:, s])], axis=0),
             jnp.concatenate([by_parity(Ki[:, s]), by_parity(Bi[:, s])], axis=0), NT, 1) for s in sl]
    Tinv = _unit_lower_inverse(Akb)
    S = [S_s[gi] for gi in groups]
    H0 = [_mm(jnp.concatenate([Kg[:, s], Rg[:, s]], axis=0), S[gi], NT, 1) for gi, s in enumerate(sl)]
    Vp = [by_parity(v_w[:, s]) for s in sl]
    AkkV = [_mm(jnp.where(strict, P[gi][:2 * C, :2 * C], 0.0), Vp[gi], NN, 1) for gi in groups]
    rhs_u = [jnp.concatenate([H0[gi][:C] + AkkV[gi][:C]] * 2 + [H0[gi][:C] + AkkV[gi][C:]] * 2, axis=0)
             for gi in groups]
    U = [_pick(_mm(Tinv[gi], rhs_u[gi], NN, 1), lane_head) for gi in groups]
    O = []
    for gi in groups:
        Y = _mm(jnp.concatenate([jnp.where(incl, P[gi][2 * C:, :2 * C], 0.0),
                                 jnp.where(incl, -P[gi][2 * C:, 2 * C:], 0.0)], axis=1),
                jnp.concatenate([Vp[gi], by_parity(U[gi])], axis=0), NN, 1)
        O.append(H0[gi][C:] + jnp.where(low_pair, Y[:C], Y[C:]))
    for gi, s in enumerate(sl):
        Z = _mm(jnp.concatenate([v_w[:, s], U[gi]], axis=0).T,
                jnp.concatenate([Kt[:, s], -Bt[:, s]], axis=0), NN, 1)
        S_s[gi] = S[gi] * e_last[:, s] + jnp.where(same_head, Z, 0.0)

    res = _rwkv_post(_tall(jnp.concatenate(O, axis=1)), r, k, v, g, r_k[...], ln_w[...], ln_b[...])
    o_ref[...] = res.reshape(SEQS, C, C_HEADS)

    @pl.when(j == pl.num_programs(1) - 1)
    def _():
        _group_state_store(S_s, sT_ref, transpose=False)


def _hgrn_seq_kernel(ph_ref, s0_ref, lb, norm_g, o_ref, sT_ref, S_s):
    j = pl.program_id(1)
    C = CHUNK

    @pl.when(j == 0)
    def _():
        _group_state_load(s0_ref, S_s, transpose=True)

    ph = ph_ref[...].reshape(SEQS * C, HGRN_COLS)
    f, kin = _hgrn_pre(ph[:, C_HEADS:2 * C_HEADS], lb[...])
    cum = _wide(_seq_cumsum(jnp.log(f)))
    q_w, kin_w, iv_w = _wide(ph[:, :C_HEADS]), _wide(kin), _wide(ph[:, 2 * C_HEADS:3 * C_HEADS])
    cum_last = cum[C - 1:C, :]
    e_last = jnp.exp(cum_last)
    qd = q_w * jnp.exp(cum)
    kd = kin_w * jnp.exp(cum_last - cum)

    low_pair, even_head = _pair_masks()
    by_pair = lambda x: _split_rows(x, low_pair)
    by_parity = lambda x: _split_rows(x, even_head)
    ar, ac = _iota2((2 * C, 2 * C))
    t_pos, s_pos = ar & (C - 1), ac & (C - 1)
    differ = jnp.where(t_pos > s_pos, t_pos ^ s_pos, 0)
    gr, gc = _iota2((GROUP, GROUP))
    same_head = (gr >> 6) == (gc >> 6)
    groups = range(N_GROUPS)
    sl = [slice(gi * GROUP, (gi + 1) * GROUP) for gi in groups]

    att = [jnp.where(t_pos == s_pos, _mm(by_pair(q_w[:, s]), by_parity(kin_w[:, s]), NT, 1), 0.0) for s in sl]
    row = lax.broadcasted_iota(jnp.int32, cum.shape, 0)
    last = cum
    for l in range(CHUNK_LOG2):
        m = 1 << l
        in_low = (row & (2 * m - 1)) < m
        d = cum - jnp.where(in_low, last, pltpu.roll(last, m, 0))
        last = jnp.where(in_low, pltpu.roll(last, C - m, 0), last)
        ql = q_w * jnp.exp(jnp.minimum(d, 0.0))
        kl = kin_w * jnp.exp(jnp.minimum(-d, 0.0))
        level = (differ >> l) == 1
        att = [jnp.where(level, _mm(by_pair(ql[:, s]), by_parity(kl[:, s]), NT, 1), att[gi])
               for gi, s in enumerate(sl)]

    outs = []
    for gi, s in enumerate(sl):
        S = S_s[gi]
        v = iv_w[:, s]
        intra = _mm(att[gi], by_parity(v), NN, 1)
        outs.append(jnp.where(low_pair, intra[:C], intra[C:]) + _mm(qd[:, s], S, NT, 1))
        Z = _mm(v.T, kd[:, s], NN, 1)
        S_s[gi] = S * e_last[:, s] + jnp.where(same_head, Z, 0.0)

    res = _hgrn_post(_tall(jnp.concatenate(outs, axis=1)), ph[:, 3 * C_HEADS:], norm_g[...])
    o_ref[...] = res.reshape(SEQS, C, C_HEADS)

    @pl.when(j == pl.num_programs(1) - 1)
    def _():
        _group_state_store(S_s, sT_ref, transpose=True)


def _lru_seq_kernel(pl_ref, c0_ref, h0_ref, conv_w, conv_b, wa, ba, wx, bx, lam,
                    o_ref, hT_ref, xbuf_s, h_s, a_s, u_s, *, tm):
    j = pl.program_id(1)

    @pl.when(j == 0)
    def _():
        xbuf_s[5:8, :] = c0_ref[0]
        h_s[...] = h0_ref[0]

    blk = pl_ref[0]
    xb = blk[:, :C_LRU]
    xbuf_s[8:8 + tm, :] = xb
    x1 = xbuf_s[7:7 + tm, :]
    x2 = xbuf_s[6:6 + tm, :]
    x3 = xbuf_s[5:5 + tm, :]
    a, u = _lru_pre(xb, x1, x2, x3, conv_w[...], conv_b[...], wa[...], ba[...], wx[...], bx[...], lam[...])
    xbuf_s[5:8, :] = xb[tm - 3:tm, :]

    row8 = lax.broadcasted_iota(jnp.int32, a.shape, 0) & 7
    for d in (1, 2, 4):
        ok = row8 >= d
        u = jnp.where(ok, a * pltpu.roll(u, d, 0) + u, u)
        a = jnp.where(ok, a * pltpu.roll(a, d, 0), a)
    a_s[...] = a
    u_s[...] = u

    def group(gi, h):
        i = pl.multiple_of(gi * 8, 8)
        ht = u_s[pl.ds(i, 8), :] + a_s[pl.ds(i, 8), :] * h
        o_ref[0, pl.ds(i, 8), :] = ht
        return ht[7:8, :]

    h = lax.fori_loop(0, tm // 8, group, h_s[...])
    h_s[...] = h
    o_ref[0] = o_ref[0] * jax.nn.gelu(blk[:, C_LRU:])

    @pl.when(j == pl.num_programs(1) - 1)
    def _():
        hT_ref[0] = h


def _full_spec(a):
    return pl.BlockSpec(a.shape, lambda b, j: (0,) * a.ndim)


def _seq_block(c):
    return pl.BlockSpec((SEQS, CHUNK, c), lambda b, j: (b, j, 0))


_STATE_BLOCK = pl.BlockSpec((SEQS, N_HEADS, HEAD_DIM, HEAD_DIM), lambda b, j: (b, 0, 0, 0))
_GROUP_STATE = pltpu.VMEM((N_GROUPS, GROUP, GROUP), F32)


def _rwkv_seq(pr, p0, s0, prm, B, T):
    o, sT = pl.pallas_call(
        _rwkv_seq_kernel,
        grid=(B // SEQS, T // CHUNK),
        in_specs=[_seq_block(RWKV_COLS), pl.BlockSpec((SEQS, 1, RWKV_COLS), lambda b, j: (b, 0, 0)),
                  _STATE_BLOCK] + [_full_spec(a) for a in prm],
        out_specs=[_seq_block(C_HEADS), _STATE_BLOCK],
        out_shape=[jax.ShapeDtypeStruct((B, T, C_HEADS), F32), jax.ShapeDtypeStruct(s0.shape, F32)],
        scratch_shapes=[pltpu.VMEM((SEQS, 1, RWKV_COLS), F32), _GROUP_STATE],
        compiler_params=_cparams("parallel", "arbitrary"),
        name="rwkv_seq",
    )(pr.reshape(B, T, RWKV_COLS), p0.reshape(B, 1, RWKV_COLS), s0, *prm)
    return o.reshape(B * T, C_HEADS), sT


def _hgrn_seq(ph, s0, lb, norm_g, B, T):
    o, sT = pl.pallas_call(
        _hgrn_seq_kernel,
        grid=(B // SEQS, T // CHUNK),
        in_specs=[_seq_block(HGRN_COLS), _STATE_BLOCK, _full_spec(lb), _full_spec(norm_g)],
        out_specs=[_seq_block(C_HEADS), _STATE_BLOCK],
        out_shape=[jax.ShapeDtypeStruct((B, T, C_HEADS), F32), jax.ShapeDtypeStruct(s0.shape, F32)],
        scratch_shapes=[_GROUP_STATE],
        compiler_params=_cparams("parallel", "arbitrary"),
        name="hgrn_seq",
    )(ph.reshape(B, T, HGRN_COLS), s0, lb, norm_g)
    return o.reshape(B * T, C_HEADS), sT


def _lru_seq(plru, conv0, h0, prm, B, T):
    tm = min(T, 256)
    seq = lambda c: pl.BlockSpec((1, tm, c), lambda b, j: (b, j, 0))
    hblk = pl.BlockSpec((1, 1, C_LRU), lambda b, j: (b, 0, 0))
    o, hT = pl.pallas_call(
        functools.partial(_lru_seq_kernel, tm=tm),
        grid=(B, T // tm),
        in_specs=[seq(LRU_COLS), pl.BlockSpec((1, CONV_WIDTH - 1, C_LRU), lambda b, j: (b, 0, 0)), hblk]
                 + [_full_spec(a) for a in prm],
        out_specs=[seq(C_LRU), hblk],
        out_shape=[jax.ShapeDtypeStruct((B, T, C_LRU), F32), jax.ShapeDtypeStruct((B, 1, C_LRU), F32)],
        scratch_shapes=[pltpu.VMEM((tm + 8, C_LRU), F32), pltpu.VMEM((1, C_LRU), F32),
                        pltpu.VMEM((tm, C_LRU), F32), pltpu.VMEM((tm, C_LRU), F32)],
        compiler_params=_cparams("parallel", "arbitrary"),
        name="lru_seq",
    )(plru.reshape(B, T, LRU_COLS), conv0, h0.reshape(B, 1, C_LRU), *prm)
    return o.reshape(B * T, C_LRU), hT.reshape(B, C_LRU)


def _block_diag(w):
    out = jnp.zeros((C_LRU, C_LRU), w.dtype)
    n = C_LRU // LRU_BLOCKS
    for i in range(LRU_BLOCKS):
        out = out.at[i * n:(i + 1) * n, i * n:(i + 1) * n].set(w[i])
    return out


def _prepare_params(p):
    row = lambda a: a.reshape(1, -1)
    zpad = jnp.zeros((W_LORA, C_HEADS), F32)
    layers = []
    for l in range(DEPTH):
        pre = (row(p['mu_shift'][l]), row(p['rwkv_w0'][l]),
               jnp.concatenate([p['rwkv_w_up'][l], zpad], 0).astype(BF16),
               row(p['rwkv_a0'][l]),
               jnp.concatenate([zpad, p['rwkv_a_up'][l]], 0).astype(BF16),
               p['rwkv_g_up'][l].astype(BF16), row(p['rwkv_k_k'][l]), row(p['rwkv_k_a'][l]),
               None,
               p['lru_conv_w'][l], row(p['lru_conv_b'][l]),
               _block_diag(p['lru_wa'][l]).astype(BF16), row(p['lru_ba'][l]),
               _block_diag(p['lru_wx'][l]).astype(BF16), row(p['lru_bx'][l]), row(p['lru_lambda'][l]))
        layers.append(dict(
            norm1_g=row(p['norm1_g'][l]), w_in=p['w_in'][l].astype(BF16), pre=pre,
            r_k=row(p['rwkv_r_k'][l]), ln_w=row(p['rwkv_ln_w'][l]), ln_b=row(p['rwkv_ln_b'][l]),
            hgrn_norm_g=row(p['hgrn_norm_g'][l]),
            w_out=p['w_out'][l].astype(BF16), norm2_g=row(p['norm2_g'][l]),
            mlp_w1=p['mlp_w1'][l].astype(BF16), mlp_w2=p['mlp_w2'][l].astype(BF16)))
    return layers


def _trunk(x, wkv, shift, hgrn, lru, conv, layers, lb_all, final_g):
    B, T, _ = x.shape
    m = B * T
    assert T == 1 or (T % CHUNK == 0 and B % SEQS == 0), "sequence kernels tile (B, T) by (SEQS, CHUNK)"
    x = x.reshape(m, D_MODEL)
    heads = lambda a: a.reshape(B, T, -1, N_HEADS, HEAD_DIM)
    n_wkv, n_shift, n_hgrn, n_lru, n_conv = [], [], [], [], []
    for l, lp in enumerate(layers):
        pr, ph, plru = _norm_proj(x, lp['norm1_g'], lp['w_in'])
        n_shift.append(_rmsnorm_rows(x.reshape(B, T, D_MODEL)[:, -1], lp['norm1_g']))
        p0 = _proj(shift[l], lp['w_in'][:, :RWKV_COLS])
        prm = list(lp['pre'])
        prm[8] = lb_all[l:l + 1]
        if T == 1:
            r, w, k, v, kk, ka, g, f, kin, a, u = _mixer_pre_step(pr, ph, plru, p0, conv[l], prm)
            wo, S_r = _head_scan(_rwkv_scan_kernel, "rwkv_scan",
                                 [(heads(t), 0) for t in (r, w, k, v, kk, ka)], wkv[l], B, T)
            ph4 = heads(ph)
            ho, S_h = _head_scan(_hgrn_scan_kernel, "hgrn_scan",
                                 [(ph4, 0), (heads(f), 0), (heads(kin), 0), (ph4, 2)], hgrn[l], B, T)
            lh, h_l = _lru_scan(a, u, lru[l], B, T)
            mix = _mix_post(wo.reshape(m, C_HEADS), r, k, v, g, lp['r_k'], lp['ln_w'], lp['ln_b'],
                            ho.reshape(m, C_HEADS), ph, lp['hgrn_norm_g'], lh, plru)
            o_r, o_h, o_l = mix[:, :C_HEADS], mix[:, C_HEADS:2 * C_HEADS], mix[:, 2 * C_HEADS:]
        else:
            o_r, S_r = _rwkv_seq(pr, p0, wkv[l], prm[:8] + [lp['r_k'], lp['ln_w'], lp['ln_b']], B, T)
            o_h, S_h = _hgrn_seq(ph, hgrn[l], prm[8], lp['hgrn_norm_g'], B, T)
            o_l, h_l = _lru_seq(plru, conv[l], lru[l], prm[9:], B, T)
        x = _out_mlp(x, o_r, o_h, o_l, lp['w_out'], lp['norm2_g'], lp['mlp_w1'], lp['mlp_w2'])
        xb = plru.reshape(B, T, LRU_COLS)[:, :, :C_LRU]
        hist = jnp.concatenate([conv[l], xb[:, max(T - (CONV_WIDTH - 1), 0):]], axis=1)
        n_wkv.append(S_r)
        n_hgrn.append(S_h)
        n_lru.append(h_l)
        n_conv.append(hist[:, -(CONV_WIDTH - 1):])
    y = _rmsnorm_rows(x, final_g).reshape(B, T, D_MODEL)
    return (y, jnp.stack(n_wkv), jnp.stack(n_shift), jnp.stack(n_hgrn), jnp.stack(n_lru), jnp.stack(n_conv))


def kernel(x_prompt, x_sample, state_wkv, state_shift, state_hgrn, state_lru, state_conv, norm1_g, w_in, mu_shift, rwkv_w0, rwkv_w_up, rwkv_a0, rwkv_a_up, rwkv_g_up, rwkv_k_k, rwkv_k_a, rwkv_r_k, rwkv_ln_w, rwkv_ln_b, hgrn_lb, hgrn_norm_g, lru_conv_w, lru_conv_b, lru_wa, lru_ba, lru_wx, lru_bx, lru_lambda, w_out, norm2_g, mlp_w1, mlp_w2, final_g):
    prm = dict(norm1_g=norm1_g, w_in=w_in, mu_shift=mu_shift, rwkv_w0=rwkv_w0, rwkv_w_up=rwkv_w_up,
               rwkv_a0=rwkv_a0, rwkv_a_up=rwkv_a_up, rwkv_g_up=rwkv_g_up, rwkv_k_k=rwkv_k_k,
               rwkv_k_a=rwkv_k_a, rwkv_r_k=rwkv_r_k.reshape(DEPTH, C_HEADS), rwkv_ln_w=rwkv_ln_w,
               rwkv_ln_b=rwkv_ln_b, hgrn_norm_g=hgrn_norm_g, lru_conv_w=lru_conv_w,
               lru_conv_b=lru_conv_b, lru_wa=lru_wa, lru_ba=lru_ba, lru_wx=lru_wx, lru_bx=lru_bx,
               lru_lambda=lru_lambda, w_out=w_out, norm2_g=norm2_g, mlp_w1=mlp_w1, mlp_w2=mlp_w2)
    layers = _prepare_params(prm)
    lb_all = _hgrn_lower_bounds(hgrn_lb)
    fg = final_g.reshape(1, D_MODEL)
    Bp = x_prompt.shape[0]
    dt = x_prompt.dtype
    z_wkv = jnp.zeros((DEPTH, Bp, N_HEADS, HEAD_DIM, HEAD_DIM), dt)
    z_shift = jnp.zeros((DEPTH, Bp, D_MODEL), dt)
    z_hgrn = jnp.zeros((DEPTH, Bp, N_HEADS, HEAD_DIM, HEAD_DIM), dt)
    z_lru = jnp.zeros((DEPTH, Bp, C_LRU), dt)
    z_conv = jnp.zeros((DEPTH, Bp, CONV_WIDTH - 1, C_LRU), dt)
    y_p, p_wkv, p_shift, p_hgrn, p_lru, p_conv = _trunk(x_prompt, z_wkv, z_shift, z_hgrn, z_lru, z_conv,
                                                         layers, lb_all, fg)
    y_s, s_wkv, s_shift, s_hgrn, s_lru, s_conv = _trunk(x_sample, state_wkv, state_shift, state_hgrn,
                                                         state_lru, state_conv, layers, lb_all, fg)
    return (y_p, y_s, p_wkv, p_shift, p_hgrn, p_lru, p_conv, s_wkv, s_shift, s_hgrn, s_lru, s_conv)
```

```python
import functools

import jax
import jax.numpy as jnp
from jax import lax
from jax.experimental import pallas as pl
from jax.experimental.pallas import tpu as pltpu

F32 = jnp.float32
BF16 = jnp.bfloat16

D_MODEL = 1024
DEPTH = 4
HEAD_DIM = 64
N_HEADS = 6
C_HEADS = N_HEADS * HEAD_DIM
C_LRU = 256
LRU_BLOCKS = 4
CONV_WIDTH = 4
LRU_C = 8.0
W_LORA = 64
A_LORA = 64
G_LORA = 128
RWKV_COLS = 3 * C_HEADS + W_LORA + A_LORA + G_LORA
HGRN_COLS = 4 * C_HEADS
LRU_COLS = 2 * C_LRU
C_IN = RWKV_COLS + HGRN_COLS + LRU_COLS
D_FF = 4 * D_MODEL
NORM_EPS = 1e-6
GN_EPS = 64e-5

LANES = 128
VMEM_LIMIT_BYTES = 48 * 1024 * 1024


def _cparams(*sem):
    return pltpu.CompilerParams(dimension_semantics=sem, vmem_limit_bytes=VMEM_LIMIT_BYTES)


def _dot(a, b):
    return jnp.dot(a.astype(BF16), b.astype(BF16), preferred_element_type=F32)


def _rmsnorm(x, g):
    return x * lax.rsqrt(jnp.mean(x * x, axis=-1, keepdims=True) + NORM_EPS) * g


def _softplus(x):
    return jnp.maximum(x, 0.0) + jnp.log1p(jnp.exp(-jnp.abs(x)))


def _head_sum(x):
    m = x.shape[0]
    lo = lax.broadcasted_iota(jnp.int32, (m, LANES), 1) < HEAD_DIM
    outs = []
    for p in range(x.shape[1] // LANES):
        xp = x[:, p * LANES:(p + 1) * LANES]
        s_lo = jnp.sum(jnp.where(lo, xp, 0.0), axis=1, keepdims=True)
        s_hi = jnp.sum(jnp.where(lo, 0.0, xp), axis=1, keepdims=True)
        outs.append(jnp.where(lo, s_lo, s_hi))
    return jnp.concatenate(outs, axis=1)


def _norm_proj_kernel(x_ref, g_ref, w_ref, pr_ref, ph_ref, pl_ref):
    xn = _rmsnorm(x_ref[...], g_ref[...])
    p = jnp.dot(xn.astype(BF16), w_ref[...], preferred_element_type=F32)
    pr_ref[...] = p[:, :RWKV_COLS]
    ph_ref[...] = p[:, RWKV_COLS:RWKV_COLS + HGRN_COLS]
    pl_ref[...] = p[:, RWKV_COLS + HGRN_COLS:]


def _norm_proj(x, g, w):
    m = x.shape[0]
    tm = min(m, 256)
    row = lambda c: pl.BlockSpec((tm, c), lambda i: (i, 0))
    full = lambda a: pl.BlockSpec(a.shape, lambda i: (0,) * a.ndim)
    return pl.pallas_call(
        _norm_proj_kernel,
        grid=(m // tm,),
        in_specs=[row(D_MODEL), full(g), full(w)],
        out_specs=[row(RWKV_COLS), row(HGRN_COLS), row(LRU_COLS)],
        out_shape=[jax.ShapeDtypeStruct((m, c), F32) for c in (RWKV_COLS, HGRN_COLS, LRU_COLS)],
        compiler_params=_cparams("parallel"),
        name="norm_proj",
    )(x, g, w)


def _proj_kernel(x_ref, w_ref, o_ref):
    o_ref[...] = jnp.dot(x_ref[...].astype(BF16), w_ref[...], preferred_element_type=F32)


def _proj(x, w):
    return pl.pallas_call(
        _proj_kernel,
        out_shape=jax.ShapeDtypeStruct((x.shape[0], w.shape[1]), F32),
        compiler_params=pltpu.CompilerParams(vmem_limit_bytes=VMEM_LIMIT_BYTES),
        name="shift_proj",
    )(x, w)


def _out_mlp_kernel(x_ref, or_ref, oh_ref, ol_ref, wo_ref, g2_ref, w1_ref, w2_ref, o_ref, x1_s, xn_s, acc_s):
    j = pl.program_id(1)

    @pl.when(j == 0)
    def _():
        mix = jnp.concatenate([or_ref[...], oh_ref[...], ol_ref[...]], axis=1)
        x1 = x_ref[...] + jnp.dot(mix.astype(BF16), wo_ref[...], preferred_element_type=F32)
        x1_s[...] = x1
        xn_s[...] = _rmsnorm(x1, g2_ref[...]).astype(BF16)
        acc_s[...] = jnp.zeros_like(acc_s)

    h = jnp.dot(xn_s[...], w1_ref[...], preferred_element_type=F32)
    h = jnp.square(jnp.maximum(h, 0.0))
    acc_s[...] += jnp.dot(h.astype(BF16), w2_ref[...], preferred_element_type=F32)

    @pl.when(j == pl.num_programs(1) - 1)
    def _():
        o_ref[...] = x1_s[...] + acc_s[...]


def _out_mlp(x, o_r, o_h, o_l, wo, g2, w1, w2):
    m = x.shape[0]
    tm = min(m, 512)
    tf = 2048
    row = pl.BlockSpec((tm, D_MODEL), lambda i, j: (i, 0))
    part = lambda c: pl.BlockSpec((tm, c), lambda i, j: (i, 0))
    return pl.pallas_call(
        _out_mlp_kernel,
        grid=(m // tm, D_FF // tf),
        in_specs=[row, part(C_HEADS), part(C_HEADS), part(C_LRU),
                  pl.BlockSpec((D_MODEL, D_MODEL), lambda i, j: (0, 0)),
                  pl.BlockSpec((1, D_MODEL), lambda i, j: (0, 0)),
                  pl.BlockSpec((D_MODEL, tf), lambda i, j: (0, j)),
                  pl.BlockSpec((tf, D_MODEL), lambda i, j: (j, 0))],
        out_specs=row,
        out_shape=jax.ShapeDtypeStruct((m, D_MODEL), F32),
        scratch_shapes=[pltpu.VMEM((tm, D_MODEL), F32), pltpu.VMEM((tm, D_MODEL), BF16),
                        pltpu.VMEM((tm, D_MODEL), F32)],
        compiler_params=_cparams("parallel", "arbitrary"),
        name="out_mlp",
    )(x, o_r, o_h, o_l, wo, g2, w1, w2)


def _rmsnorm_rows_kernel(x_ref, g_ref, o_ref):
    o_ref[...] = _rmsnorm(x_ref[...], g_ref[...])


def _rmsnorm_rows(x, g):
    m = x.shape[0]
    tm = min(m, 512)
    row = pl.BlockSpec((tm, D_MODEL), lambda i: (i, 0))
    return pl.pallas_call(
        _rmsnorm_rows_kernel,
        grid=(m // tm,),
        in_specs=[row, pl.BlockSpec((1, D_MODEL), lambda i: (0, 0))],
        out_specs=row,
        out_shape=jax.ShapeDtypeStruct((m, D_MODEL), F32),
        compiler_params=_cparams("parallel"),
        name="rmsnorm_rows",
    )(x, g)


def _lb_kernel(p_ref, o_ref):
    p = p_ref[...]
    e = jnp.exp(p - jnp.max(p, axis=0, keepdims=True))
    s = e / jnp.sum(e, axis=0, keepdims=True)
    acc = jnp.zeros_like(s[0:1])
    for l in range(DEPTH):
        if l > 0:
            acc = acc + s[l:l + 1]
        o_ref[l:l + 1, :] = acc


def _hgrn_lower_bounds(hgrn_lb):
    return pl.pallas_call(_lb_kernel, out_shape=jax.ShapeDtypeStruct(hgrn_lb.shape, F32),
                          name="hgrn_lower_bounds")(hgrn_lb)


def _rwkv_pre(pr, prev, mu, w0, w_up, a0, a_up, g_up, k_k, k_a):
    c = C_HEADS
    pm = pr + mu * (prev - pr)
    r = pm[:, :c]
    k0 = pm[:, c:2 * c]
    v = pm[:, 2 * c:3 * c]
    xwa = pm[:, 3 * c:3 * c + W_LORA + A_LORA]
    xg = pm[:, 3 * c + W_LORA + A_LORA:]
    w_log = -_softplus(-(w0 + _dot(jnp.tanh(xwa), w_up))) - 0.5
    log_decay = -jnp.exp(w_log)
    a = jax.nn.sigmoid(a0 + _dot(xwa, a_up))
    g = _dot(jax.nn.sigmoid(xg), g_up)
    kk = k0 * k_k
    kk = kk * lax.rsqrt(jnp.maximum(_head_sum(kk * kk), 1e-24))
    k = k0 * (1.0 + (a - 1.0) * k_a)
    return r, log_decay, k, v, kk, kk * a, g


def _hgrn_pre(fpre, lb):
    sg = jax.nn.sigmoid(fpre)
    f = lb + (1.0 - lb) * sg
    kin = (1.0 - lb) * jax.nn.sigmoid(-fpre)
    return f, kin


def _lru_pre(xb, x1, x2, x3, conv_w, conv_b, wa, ba, wx, bx, lam):
    xc = conv_b + (((x3 * conv_w[0:1] + x2 * conv_w[1:2]) + x1 * conv_w[2:3]) + xb * conv_w[3:4])
    r = jax.nn.sigmoid(_dot(xc, wa) + ba)
    i = jax.nn.sigmoid(_dot(xc, wx) + bx)
    log_a = -LRU_C * r * _softplus(-lam)
    a = jnp.exp(log_a)
    one_minus_a2 = -jnp.tanh(log_a) * (jnp.exp(2.0 * log_a) + 1.0)
    u = xc * i * jnp.sqrt(jnp.maximum(one_minus_a2, 1e-12))
    return a, u


_N_PRE_PARAMS = 16


def _pre_compute(pr, prev, ph, xb, x1, x2, x3, prm, outs):
    (mu, w0, w_up, a0, a_up, g_up, k_k, k_a, lb, conv_w, conv_b, wa, ba, wx, bx, lam) = prm
    r_o, w_o, k_o, v_o, kk_o, ka_o, g_o, f_o, kin_o, a_o, u_o = outs
    r, lw, k, v, kk, ka, g = _rwkv_pre(pr, prev, mu[...], w0[...], w_up[...], a0[...], a_up[...],
                                       g_up[...], k_k[...], k_a[...])
    w = jnp.exp(lw)
    f, kin = _hgrn_pre(ph[:, C_HEADS:2 * C_HEADS], lb[...])
    a, u = _lru_pre(xb, x1, x2, x3, conv_w[...], conv_b[...], wa[...], ba[...], wx[...], bx[...], lam[...])
    return ((r_o, r), (w_o, w), (k_o, k), (v_o, v), (kk_o, kk), (ka_o, ka), (g_o, g),
            (f_o, f), (kin_o, kin), (a_o, a), (u_o, u))


def _pre_step_kernel(*refs):
    pr_ref, ph_ref, pl_ref, p0_ref, c0_ref, c1_ref, c2_ref = refs[:7]
    prm = refs[7:7 + _N_PRE_PARAMS]
    outs = refs[7 + _N_PRE_PARAMS:]
    xb = pl_ref[...][:, :C_LRU]
    res = _pre_compute(pr_ref[...], p0_ref[...], ph_ref[...], xb, c2_ref[...], c1_ref[...], c0_ref[...],
                       prm, outs)
    for o_ref, val in res:
        o_ref[...] = val


def _mixer_pre_step(pr, ph, plru, p0, conv0, prm):
    widths = (C_HEADS,) * 9 + (C_LRU,) * 2
    return pl.pallas_call(
        _pre_step_kernel,
        out_shape=[jax.ShapeDtypeStruct((pr.shape[0], c), F32) for c in widths],
        compiler_params=pltpu.CompilerParams(vmem_limit_bytes=VMEM_LIMIT_BYTES),
        name="mixer_pre_step",
    )(pr, ph, plru, p0, conv0[:, 0], conv0[:, 1], conv0[:, 2], *prm)


def _eye():
    return (lax.broadcasted_iota(jnp.int32, (HEAD_DIM, HEAD_DIM), 0)
            == lax.broadcasted_iota(jnp.int32, (HEAD_DIM, HEAD_DIM), 1))


def _to_col(row, eye):
    return jnp.sum(jnp.where(eye, row, 0.0), axis=1, keepdims=True)


def _to_row(col, eye):
    return jnp.sum(jnp.where(eye, col, 0.0), axis=0, keepdims=True)


def _rwkv_scan_kernel(r_ref, w_ref, k_ref, v_ref, kk_ref, ka_ref, s0_ref, o_ref, sT_ref, S_s, *, bb, tc):
    j = pl.program_id(1)

    @pl.when(j == 0)
    def _():
        S_s[...] = s0_ref[...]

    eye = _eye()

    def step(t, carry):
        units = [(b, h) for b in range(bb) for h in range(N_HEADS)]
        idx = [(b, t, pl.ds(h, 1), slice(None)) for b, h in units]
        S = [S_s[b, h] for b, h in units]
        sa = [-jnp.sum(Si * kk_ref[i], axis=1, keepdims=True) for Si, i in zip(S, idx)]
        vc = [_to_col(v_ref[i], eye) for i in idx]
        S = [Si * w_ref[i] + sai * ka_ref[i] + vci * k_ref[i] for Si, sai, vci, i in zip(S, sa, vc, idx)]
        for (b, h), Si in zip(units, S):
            S_s[b, h] = Si
        oc = [jnp.sum(Si * r_ref[i], axis=1, keepdims=True) for Si, i in zip(S, idx)]
        for i, oci in zip(idx, oc):
            o_ref[i] = _to_row(oci, eye)
        return carry

    lax.fori_loop(0, tc, step, 0)

    @pl.when(j == pl.num_programs(1) - 1)
    def _():
        sT_ref[...] = S_s[...]


def _hgrn_scan_kernel(q_ref, f_ref, kin_ref, iv_ref, s0_ref, o_ref, sT_ref, S_s, *, bb, tc):
    j = pl.program_id(1)

    @pl.when(j == 0)
    def _():
        S_s[...] = s0_ref[...]

    eye = _eye()

    def step(t, carry):
        units = [(b, h) for b in range(bb) for h in range(N_HEADS)]
        idx = [(b, t, pl.ds(h, 1), slice(None)) for b, h in units]
        cols = [(_to_col(f_ref[i], eye), _to_col(kin_ref[i], eye), _to_col(q_ref[i], eye)) for i in idx]
        for (b, h), i, (fc, kc, qc) in zip(units, idx, cols):
            S = fc * S_s[b, h] + kc * iv_ref[i]
            S_s[b, h] = S
            o_ref[i] = jnp.sum(S * qc, axis=0, keepdims=True)
        return carry

    lax.fori_loop(0, tc, step, 0)

    @pl.when(j == pl.num_programs(1) - 1)
    def _():
        sT_ref[...] = S_s[...]


def _lru_scan_kernel(a_ref, u_ref, h0_ref, o_ref, hT_ref, h_s, *, bb, tc):
    j = pl.program_id(1)

    @pl.when(j == 0)
    def _():
        h_s[...] = h0_ref[...]

    def step(t, carry):
        for b in range(bb):
            idx = (b, pl.ds(t, 1), slice(None))
            h = a_ref[idx] * h_s[b] + u_ref[idx]
            h_s[b] = h
            o_ref[idx] = h
        return carry

    lax.fori_loop(0, tc, step, 0)

    @pl.when(j == pl.num_programs(1) - 1)
    def _():
        hT_ref[...] = h_s[...]


def _scan_tiles(B, T):
    bb = 8 if T == 1 else 4
    tc = min(T, 32)
    return bb, tc


def _head_scan(kernel, name, seqs, states, layer, new_states, B, T):
    bb, tc = _scan_tiles(B, T)
    n_seq = len(seqs)
    hblk = lambda g: pl.BlockSpec((bb, tc, None, N_HEADS, HEAD_DIM), lambda i, j: (i, j, g, 0, 0))
    sblk = pl.BlockSpec((None, bb, N_HEADS, HEAD_DIM, HEAD_DIM), lambda i, j: (layer, i, 0, 0, 0))
    in_specs = [hblk(g) for _, g in seqs] + [sblk]
    operands = [a for a, _ in seqs] + [states]
    aliases = {}
    if new_states is not None:
        in_specs.append(pl.BlockSpec(memory_space=pl.ANY))
        operands.append(new_states)
        aliases = {n_seq + 1: 1}

    def body(*refs):
        kernel(*refs[:n_seq + 1], *refs[len(operands):], bb=bb, tc=tc)

    return pl.pallas_call(
        body,
        grid=(B // bb, T // tc),
        in_specs=in_specs,
        out_specs=[hblk(0), sblk],
        out_shape=[jax.ShapeDtypeStruct((B, T, 1, N_HEADS, HEAD_DIM), F32),
                   jax.ShapeDtypeStruct(states.shape, F32)],
        scratch_shapes=[pltpu.VMEM((bb, N_HEADS, HEAD_DIM, HEAD_DIM), F32)],
        input_output_aliases=aliases,
        compiler_params=_cparams("parallel", "arbitrary"),
        name=name,
    )(*operands)


def _lru_scan(a, u, h0, B, T):
    bb, tc = _scan_tiles(B, T)
    blk = pl.BlockSpec((bb, tc, C_LRU), lambda i, j: (i, j, 0))
    hblk = pl.BlockSpec((bb, 1, C_LRU), lambda i, j: (i, 0, 0))
    o, hT = pl.pallas_call(
        functools.partial(_lru_scan_kernel, bb=bb, tc=tc),
        grid=(B // bb, T // tc),
        in_specs=[blk, blk, hblk],
        out_specs=[blk, hblk],
        out_shape=[jax.ShapeDtypeStruct((B, T, C_LRU), F32), jax.ShapeDtypeStruct((B, 1, C_LRU), F32)],
        scratch_shapes=[pltpu.VMEM((bb, 1, C_LRU), F32)],
        compiler_params=_cparams("parallel", "arbitrary"),
        name="lru_scan",
    )(a.reshape(B, T, C_LRU), u.reshape(B, T, C_LRU), h0.reshape(B, 1, C_LRU))
    return o.reshape(B * T, C_LRU), hT.reshape(B, C_LRU)


def _rwkv_post(out, r, k, v, g, r_k, ln_w, ln_b):
    inv_n = 1.0 / HEAD_DIM
    mean = _head_sum(out) * inv_n
    d = out - mean
    var = _head_sum(d * d) * inv_n
    gn = d * lax.rsqrt(var + GN_EPS) * ln_w + ln_b
    bonus = _head_sum(r * k * r_k) * v
    return (gn + bonus) * g


def _hgrn_post(o, gate, norm_g):
    o = o * lax.rsqrt(_head_sum(o * o) * (1.0 / HEAD_DIM) + NORM_EPS)
    return o * norm_g * jax.nn.silu(gate)


def _mix_post_kernel(wo_ref, r_ref, k_ref, v_ref, g_ref, rk_ref, lnw_ref, lnb_ref,
                     ho_ref, hg_ref, hng_ref, lh_ref, lg_ref, mix_ref):
    mix_ref[:, :C_HEADS] = _rwkv_post(wo_ref[...], r_ref[...], k_ref[...], v_ref[...], g_ref[...],
                                      rk_ref[...], lnw_ref[...], lnb_ref[...])
    mix_ref[:, C_HEADS:2 * C_HEADS] = _hgrn_post(ho_ref[...], hg_ref[...], hng_ref[...])
    mix_ref[:, 2 * C_HEADS:] = lh_ref[...] * jax.nn.gelu(lg_ref[...])


def _mix_post(wo, r, k, v, g, r_k, ln_w, ln_b, ho, ph, hng, lh, plru):
    m = wo.shape[0]
    tm = min(m, 512)
    hrow = pl.BlockSpec((tm, C_HEADS), lambda i: (i, 0))
    prow = lambda c: pl.BlockSpec((1, c), lambda i: (0, 0))
    return pl.pallas_call(
        _mix_post_kernel,
        grid=(m // tm,),
        in_specs=[hrow, hrow, hrow, hrow, hrow, prow(C_HEADS), prow(C_HEADS), prow(C_HEADS),
                  hrow, pl.BlockSpec((tm, C_HEADS), lambda i: (i, 3)), prow(C_HEADS),
                  pl.BlockSpec((tm, C_LRU), lambda i: (i, 0)),
                  pl.BlockSpec((tm, C_LRU), lambda i: (i, 1))],
        out_specs=pl.BlockSpec((tm, D_MODEL), lambda i: (i, 0)),
        out_shape=jax.ShapeDtypeStruct((m, D_MODEL), F32),
        compiler_params=_cparams("parallel"),
        name="mix_post",
    )(wo, r, k, v, g, r_k, ln_w, ln_b, ho, ph, hng, lh, plru)


CHUNK = 64
CHUNK_LOG2 = 6
SEQS = 4
GROUP_HEADS = 4
GROUP = GROUP_HEADS * HEAD_DIM
N_GROUPS = SEQS * N_HEADS // GROUP_HEADS

NN = ((1,), (0,))
NT = ((1,), (1,))


def _mm(a, b, dims, passes):
    dn = (dims, ((), ()))
    if passes == 6:
        return lax.dot_general(a, b, dn, precision=lax.Precision.HIGHEST, preferred_element_type=F32)
    ah, bh = a.astype(BF16), b.astype(BF16)
    out = lax.dot_general(ah, bh, dn, preferred_element_type=F32)
    if passes == 3:
        al = (a - ah.astype(F32)).astype(BF16)
        bl = (b - bh.astype(F32)).astype(BF16)
        out = (out + lax.dot_general(ah, bl, dn, preferred_element_type=F32)
               + lax.dot_general(al, bh, dn, preferred_element_type=F32))
    return out


def _iota2(shape):
    return lax.broadcasted_iota(jnp.int32, shape, 0), lax.broadcasted_iota(jnp.int32, shape, 1)


def _head_of(group, i):
    hw = group * GROUP_HEADS + i
    return hw // N_HEADS, hw % N_HEADS


def _group_state_load(s_ref, S_s, transpose):
    z = jnp.zeros((HEAD_DIM, HEAD_DIM), F32)
    for gi in range(N_GROUPS):
        rows = []
        for i in range(GROUP_HEADS):
            b, h = _head_of(gi, i)
            blk = s_ref[b, h].T if transpose else s_ref[b, h]
            rows.append(jnp.concatenate([blk if i2 == i else z for i2 in range(GROUP_HEADS)], axis=1))
        S_s[gi] = jnp.concatenate(rows, axis=0)


def _group_state_store(S_s, s_ref, transpose):
    for gi in range(N_GROUPS):
        Sg = S_s[gi]
        for i in range(GROUP_HEADS):
            b, h = _head_of(gi, i)
            blk = Sg[i * HEAD_DIM:(i + 1) * HEAD_DIM, i * HEAD_DIM:(i + 1) * HEAD_DIM]
            s_ref[b, h] = blk.T if transpose else blk


def _wide(x):
    return jnp.concatenate([x[b * CHUNK:(b + 1) * CHUNK] for b in range(SEQS)], axis=1)


def _tall(x):
    return jnp.concatenate([x[:, b * C_HEADS:(b + 1) * C_HEADS] for b in range(SEQS)], axis=0)


def _seq_cumsum(x):
    tr, tc = _iota2((SEQS * CHUNK, SEQS * CHUNK))
    tri = ((tr >= tc) & ((tr >> CHUNK_LOG2) == (tc >> CHUNK_LOG2))).astype(F32)
    return _mm(tri, x, NN, 6)


def _lane_head():
    return lax.broadcasted_iota(jnp.int32, (CHUNK, GROUP), 1) >> 6


def _by_head(x, lane_head):
    return jnp.concatenate([jnp.where(lane_head == i, x, 0.0) for i in range(GROUP_HEADS)], axis=0)


def _rep(x):
    return jnp.concatenate([x] * GROUP_HEADS, axis=0)


def _pair_masks():
    lane = lax.broadcasted_iota(jnp.int32, (CHUNK, GROUP), 1)
    return lane < 2 * HEAD_DIM, (lane & HEAD_DIM) == 0


def _split_rows(x, mask):
    return jnp.concatenate([jnp.where(mask, x, 0.0), jnp.where(mask, 0.0, x)], axis=0)


def _pick(x, lane_head):
    out = x[:CHUNK]
    for i in range(1, GROUP_HEADS):
        out = jnp.where(lane_head == i, x[i * CHUNK:(i + 1) * CHUNK], out)
    return out


def _unit_lower_inverse(Ls):
    r, c = _iota2(Ls[0].shape)
    eye = (r == c).astype(F32)
    diag_blk = (r >> 4) == (c >> 4)
    mm = lambda xs, ys: [_mm(x, y, NN, 1) for x, y in zip(xs, ys)]
    Ld = [jnp.where(diag_blk, L, 0.0) for L in Ls]
    N = [jnp.where(diag_blk, 0.0, L) for L in Ls]
    L2 = mm(Ld, Ld)
    L4 = mm(L2, L2)
    P1 = mm([eye - x for x in Ld], [eye + x for x in L2])
    L8 = mm(L4, L4)
    P2 = mm([eye + x for x in L4], [eye + x for x in L8])
    Dinv = mm(P1, P2)
    M = mm(Dinv, N)
    M2 = mm(M, M)
    X = mm([eye - x for x in M], [eye + x for x in M2])
    return mm(X, Dinv)


def _seq_project(x_ref, norm_g, w_ref):
    x = x_ref[...].reshape(-1, D_MODEL)
    return jnp.dot(_rmsnorm(x, norm_g[...]).astype(BF16), w_ref[...], preferred_element_type=F32)


def _rwkv_seq_kernel(x_ref, shift_ref, norm_g, w_ref, s0_ref,
                     mu, w0, w_up, a0, a_up, g_up, k_k, k_a, r_k, ln_w, ln_b,
                     o_ref, sT_ref, carry_s, S_s):
    j = pl.program_id(1)
    C = CHUNK

    @pl.when(j == 0)
    def _():
        p0 = jnp.dot(shift_ref[...].reshape(SEQS, D_MODEL).astype(BF16), w_ref[...], preferred_element_type=F32)
        carry_s[...] = p0.reshape(SEQS, 1, RWKV_COLS)
        _group_state_load(s0_ref, S_s, transpose=False)

    pr = _seq_project(x_ref, norm_g, w_ref)
    prow = lax.broadcasted_iota(jnp.int32, pr.shape, 0)
    prev = pltpu.roll(pr, 1, 0)
    for b in range(SEQS):
        prev = jnp.where(prow == b * C, carry_s[b], prev)
        carry_s[b] = pr[(b + 1) * C - 1:(b + 1) * C, :]

    r, lw, k, v, kk, kb, g = _rwkv_pre(pr, prev, mu[...], w0[...], w_up[...], a0[...], a_up[...],
                                       g_up[...], k_k[...], k_a[...])
    cum_w, lw_w = _wide(_seq_cumsum(lw)), _wide(lw)
    r_w, k_w, v_w, kk_w, kb_w = _wide(r), _wide(k), _wide(v), _wide(kk), _wide(kb)
    cum_last = cum_w[C - 1:C, :]
    e_inv = jnp.exp(-cum_w)
    e_rel = jnp.exp(cum_last - cum_w)
    e_last = jnp.exp(cum_last)
    Kg = kk_w * jnp.exp(cum_w - lw_w)
    Rg = r_w * jnp.exp(cum_w)
    Ki, Bi = k_w * e_inv, kb_w * e_inv
    Kt, Bt = k_w * e_rel, kb_w * e_rel

    lane_head = _lane_head()
    low_pair, even_head = _pair_masks()
    by_pair = lambda x: _split_rows(x, low_pair)
    by_parity = lambda x: _split_rows(x, even_head)
    gr, gc = _iota2((GROUP, GROUP))
    same_head = (gr >> 6) == (gc >> 6)
    strict_bd = same_head & ((gr & (HEAD_DIM - 1)) > (gc & (HEAD_DIM - 1)))
    ar, ac = _iota2((2 * C, 2 * C))
    strict = (ar & (C - 1)) > (ac & (C - 1))
    incl = (ar & (C - 1)) >= (ac & (C - 1))

    groups = range(N_GROUPS)
    sl = [slice(gi * GROUP, (gi + 1) * GROUP) for gi in groups]
    Akb = [jnp.where(strict_bd, _mm(_by_head(Kg[:, s], lane_head), _rep(Bi[:, s]), NT, 1), 0.0) for s in sl]
    P = [_mm(jnp.concatenate([by_pair(Kg[:, s]), by_pair(Rg[:, s])], axis=0),
             jnp.concatenate([by_parity(Ki[:, s]), by_parity(Bi[:, s])], axis=0), NT, 1) for s in sl]
    Tinv = _unit_lower_inverse(Akb)
    S = [S_s[gi] for gi in groups]
    H0 = [_mm(jnp.concatenate([Kg[:, s], Rg[:, s]], axis=0), S[gi], NT, 1) for gi, s in enumerate(sl)]
    Vp = [by_parity(v_w[:, s]) for s in sl]
    AkkV = [_mm(jnp.where(strict, P[gi][:2 * C, :2 * C], 0.0), Vp[gi], NN, 1) for gi in groups]
    rhs_u = [jnp.concatenate([H0[gi][:C] + AkkV[gi][:C]] * 2 + [H0[gi][:C] + AkkV[gi][C:]] * 2, axis=0)
             for gi in groups]
    U = [_pick(_mm(Tinv[gi], rhs_u[gi], NN, 1), lane_head) for gi in groups]
    O = []
    for gi in groups:
        Y = _mm(jnp.concatenate([jnp.where(incl, P[gi][2 * C:, :2 * C], 0.0),
                                 jnp.where(incl, -P[gi][2 * C:, 2 * C:], 0.0)], axis=1),
                jnp.concatenate([Vp[gi], by_parity(U[gi])], axis=0), NN, 1)
        O.append(H0[gi][C:] + jnp.where(low_pair, Y[:C], Y[C:]))
    for gi, s in enumerate(sl):
        Z = _mm(jnp.concatenate([v_w[:, s], U[gi]], axis=0).T,
                jnp.concatenate([Kt[:, s], -Bt[:, s]], axis=0), NN, 1)
        S_s[gi] = S[gi] * e_last[:, s] + jnp.where(same_head, Z, 0.0)

    res = _rwkv_post(_tall(jnp.concatenate(O, axis=1)), r, k, v, g, r_k[...], ln_w[...], ln_b[...])
    o_ref[...] = res.reshape(SEQS, C, C_HEADS)

    @pl.when(j == pl.num_programs(1) - 1)
    def _():
        _group_state_store(S_s, sT_ref, transpose=False)


def _hgrn_seq_kernel(x_ref, norm1_g, w_ref, s0_ref, lb, norm_g, o_ref, sT_ref, S_s):
    j = pl.program_id(1)
    C = CHUNK

    @pl.when(j == 0)
    def _():
        _group_state_load(s0_ref, S_s, transpose=True)

    ph = _seq_project(x_ref, norm1_g, w_ref)
    f, kin = _hgrn_pre(ph[:, C_HEADS:2 * C_HEADS], lb[...])
    cum = _wide(_seq_cumsum(jnp.log(f)))
    q_w, kin_w, iv_w = _wide(ph[:, :C_HEADS]), _wide(kin), _wide(ph[:, 2 * C_HEADS:3 * C_HEADS])
    cum_last = cum[C - 1:C, :]
    e_last = jnp.exp(cum_last)
    qd = q_w * jnp.exp(cum)
    kd = kin_w * jnp.exp(cum_last - cum)

    low_pair, even_head = _pair_masks()
    by_pair = lambda x: _split_rows(x, low_pair)
    by_parity = lambda x: _split_rows(x, even_head)
    ar, ac = _iota2((2 * C, 2 * C))
    t_pos, s_pos = ar & (C - 1), ac & (C - 1)
    differ = jnp.where(t_pos > s_pos, t_pos ^ s_pos, 0)
    gr, gc = _iota2((GROUP, GROUP))
    same_head = (gr >> 6) == (gc >> 6)
    groups = range(N_GROUPS)
    sl = [slice(gi * GROUP, (gi + 1) * GROUP) for gi in groups]

    att = [jnp.where(t_pos == s_pos, _mm(by_pair(q_w[:, s]), by_parity(kin_w[:, s]), NT, 1), 0.0) for s in sl]
    row = lax.broadcasted_iota(jnp.int32, cum.shape, 0)
    last = cum
    for l in range(CHUNK_LOG2):
        m = 1 << l
        in_low = (row & (2 * m - 1)) < m
        d = cum - jnp.where(in_low, last, pltpu.roll(last, m, 0))
        last = jnp.where(in_low, pltpu.roll(last, C - m, 0), last)
        ql = q_w * jnp.exp(jnp.minimum(d, 0.0))
        kl = kin_w * jnp.exp(jnp.minimum(-d, 0.0))
        level = (differ >> l) == 1
        att = [jnp.where(level, _mm(by_pair(ql[:, s]), by_parity(kl[:, s]), NT, 1), att[gi])
               for gi, s in enumerate(sl)]

    outs = []
    for gi, s in enumerate(sl):
        S = S_s[gi]
        v = iv_w[:, s]
        intra = _mm(att[gi], by_parity(v), NN, 1)
        outs.append(jnp.where(low_pair, intra[:C], intra[C:]) + _mm(qd[:, s], S, NT, 1))
        Z = _mm(v.T, kd[:, s], NN, 1)
        S_s[gi] = S * e_last[:, s] + jnp.where(same_head, Z, 0.0)

    res = _hgrn_post(_tall(jnp.concatenate(outs, axis=1)), ph[:, 3 * C_HEADS:], norm_g[...])
    o_ref[...] = res.reshape(SEQS, C, C_HEADS)

    @pl.when(j == pl.num_programs(1) - 1)
    def _():
        _group_state_store(S_s, sT_ref, transpose=True)


def _lru_seq_kernel(x_ref, norm_g, w_ref, c0_ref, h0_ref, conv_w, conv_b, wa, ba, wx, bx, lam,
                    o_ref, hT_ref, cT_ref, xbuf_s, h_s, a_s, u_s, *, tm):
    j = pl.program_id(1)

    @pl.when(j == 0)
    def _():
        xbuf_s[5:8, :] = c0_ref[0]
        h_s[...] = h0_ref[0]

    blk = _seq_project(x_ref, norm_g, w_ref)
    xb = blk[:, :C_LRU]
    xbuf_s[8:8 + tm, :] = xb
    x1 = xbuf_s[7:7 + tm, :]
    x2 = xbuf_s[6:6 + tm, :]
    x3 = xbuf_s[5:5 + tm, :]
    a, u = _lru_pre(xb, x1, x2, x3, conv_w[...], conv_b[...], wa[...], ba[...], wx[...], bx[...], lam[...])
    xbuf_s[5:8, :] = xb[tm - 3:tm, :]

    row8 = lax.broadcasted_iota(jnp.int32, a.shape, 0) & 7
    for d in (1, 2, 4):
        ok = row8 >= d
        u = jnp.where(ok, a * pltpu.roll(u, d, 0) + u, u)
        a = jnp.where(ok, a * pltpu.roll(a, d, 0), a)
    a_s[...] = a
    u_s[...] = u

    def group(gi, h):
        i = pl.multiple_of(gi * 8, 8)
        ht = u_s[pl.ds(i, 8), :] + a_s[pl.ds(i, 8), :] * h
        o_ref[0, pl.ds(i, 8), :] = ht
        return ht[7:8, :]

    h = lax.fori_loop(0, tm // 8, group, h_s[...])
    h_s[...] = h
    o_ref[0] = o_ref[0] * jax.nn.gelu(blk[:, C_LRU:])

    @pl.when(j == pl.num_programs(1) - 1)
    def _():
        hT_ref[0] = h
        cT_ref[0] = xb[tm - (CONV_WIDTH - 1):tm, :]


def _full_spec(a):
    return pl.BlockSpec(a.shape, lambda b, j: (0,) * a.ndim)


def _seq_block(c):
    return pl.BlockSpec((SEQS, CHUNK, c), lambda b, j: (b, j, 0))


_STATE_BLOCK = pl.BlockSpec((SEQS, N_HEADS, HEAD_DIM, HEAD_DIM), lambda b, j: (b, 0, 0, 0))
_GROUP_STATE = pltpu.VMEM((N_GROUPS, GROUP, GROUP), F32)


def _rwkv_seq(x, shift, norm_g, w, s0, prm, B, T):
    o, sT = pl.pallas_call(
        _rwkv_seq_kernel,
        grid=(B // SEQS, T // CHUNK),
        in_specs=[_seq_block(D_MODEL), pl.BlockSpec((SEQS, 1, D_MODEL), lambda b, j: (b, 0, 0)),
                  _full_spec(norm_g), _full_spec(w), _STATE_BLOCK] + [_full_spec(a) for a in prm],
        out_specs=[_seq_block(C_HEADS), _STATE_BLOCK],
        out_shape=[jax.ShapeDtypeStruct((B, T, C_HEADS), F32), jax.ShapeDtypeStruct(s0.shape, F32)],
        scratch_shapes=[pltpu.VMEM((SEQS, 1, RWKV_COLS), F32), _GROUP_STATE],
        compiler_params=_cparams("parallel", "arbitrary"),
        name="rwkv_seq",
    )(x, shift.reshape(B, 1, D_MODEL), norm_g, w, s0, *prm)
    return o.reshape(B * T, C_HEADS), sT


def _hgrn_seq(x, norm1_g, w, s0, lb, norm_g, B, T):
    o, sT = pl.pallas_call(
        _hgrn_seq_kernel,
        grid=(B // SEQS, T // CHUNK),
        in_specs=[_seq_block(D_MODEL), _full_spec(norm1_g), _full_spec(w), _STATE_BLOCK,
                  _full_spec(lb), _full_spec(norm_g)],
        out_specs=[_seq_block(C_HEADS), _STATE_BLOCK],
        out_shape=[jax.ShapeDtypeStruct((B, T, C_HEADS), F32), jax.ShapeDtypeStruct(s0.shape, F32)],
        scratch_shapes=[_GROUP_STATE],
        compiler_params=_cparams("parallel", "arbitrary"),
        name="hgrn_seq",
    )(x, norm1_g, w, s0, lb, norm_g)
    return o.reshape(B * T, C_HEADS), sT


def _lru_seq(x, norm_g, w, conv0, h0, prm, B, T):
    tm = min(T, 256)
    seq = lambda c: pl.BlockSpec((1, tm, c), lambda b, j: (b, j, 0))
    hblk = pl.BlockSpec((1, 1, C_LRU), lambda b, j: (b, 0, 0))
    cblk = pl.BlockSpec((1, CONV_WIDTH - 1, C_LRU), lambda b, j: (b, 0, 0))
    o, hT, cT = pl.pallas_call(
        functools.partial(_lru_seq_kernel, tm=tm),
        grid=(B, T // tm),
        in_specs=[seq(D_MODEL), _full_spec(norm_g), _full_spec(w), cblk, hblk] + [_full_spec(a) for a in prm],
        out_specs=[seq(C_LRU), hblk, cblk],
        out_shape=[jax.ShapeDtypeStruct((B, T, C_LRU), F32), jax.ShapeDtypeStruct((B, 1, C_LRU), F32),
                   jax.ShapeDtypeStruct((B, CONV_WIDTH - 1, C_LRU), F32)],
        scratch_shapes=[pltpu.VMEM((tm + 8, C_LRU), F32), pltpu.VMEM((1, C_LRU), F32),
                        pltpu.VMEM((tm, C_LRU), F32), pltpu.VMEM((tm, C_LRU), F32)],
        compiler_params=_cparams("parallel", "arbitrary"),
        name="lru_seq",
    )(x, norm_g, w, conv0, h0.reshape(B, 1, C_LRU), *prm)
    return o.reshape(B * T, C_LRU), hT.reshape(B, C_LRU), cT


def _block_diag(w):
    out = jnp.zeros((C_LRU, C_LRU), w.dtype)
    n = C_LRU // LRU_BLOCKS
    for i in range(LRU_BLOCKS):
        out = out.at[i * n:(i + 1) * n, i * n:(i + 1) * n].set(w[i])
    return out


def _prepare_params(p):
    row = lambda a: a.reshape(1, -1)
    zpad = jnp.zeros((W_LORA, C_HEADS), F32)
    layers = []
    for l in range(DEPTH):
        pre = (row(p['mu_shift'][l]), row(p['rwkv_w0'][l]),
               jnp.concatenate([p['rwkv_w_up'][l], zpad], 0).astype(BF16),
               row(p['rwkv_a0'][l]),
               jnp.concatenate([zpad, p['rwkv_a_up'][l]], 0).astype(BF16),
               p['rwkv_g_up'][l].astype(BF16), row(p['rwkv_k_k'][l]), row(p['rwkv_k_a'][l]),
               None,
               p['lru_conv_w'][l], row(p['lru_conv_b'][l]),
               _block_diag(p['lru_wa'][l]).astype(BF16), row(p['lru_ba'][l]),
               _block_diag(p['lru_wx'][l]).astype(BF16), row(p['lru_bx'][l]), row(p['lru_lambda'][l]))
        layers.append(dict(
            norm1_g=row(p['norm1_g'][l]), w_in=p['w_in'][l].astype(BF16), pre=pre,
            r_k=row(p['rwkv_r_k'][l]), ln_w=row(p['rwkv_ln_w'][l]), ln_b=row(p['rwkv_ln_b'][l]),
            hgrn_norm_g=row(p['hgrn_norm_g'][l]),
            w_out=p['w_out'][l].astype(BF16), norm2_g=row(p['norm2_g'][l]),
            mlp_w1=p['mlp_w1'][l].astype(BF16), mlp_w2=p['mlp_w2'][l].astype(BF16)))
    return layers


def _trunk(x, wkv, shift, hgrn, lru, conv, layers, lb_all, final_g):
    B, T, _ = x.shape
    m = B * T
    assert T == 1 or (T % CHUNK == 0 and B % SEQS == 0), "sequence kernels tile (B, T) by (SEQS, CHUNK)"
    x = x.reshape(m, D_MODEL)
    heads = lambda a: a.reshape(B, T, -1, N_HEADS, HEAD_DIM)
    n_wkv, n_shift, n_hgrn, n_lru, n_conv = [], [], [], [], []
    s_wkv = s_hgrn = None
    for l, lp in enumerate(layers):
        n_shift.append(_rmsnorm_rows(x.reshape(B, T, D_MODEL)[:, -1], lp['norm1_g']))
        prm = list(lp['pre'])
        prm[8] = lb_all[l:l + 1]
        w_r, w_h, w_l = (lp['w_in'][:, :RWKV_COLS], lp['w_in'][:, RWKV_COLS:RWKV_COLS + HGRN_COLS],
                         lp['w_in'][:, RWKV_COLS + HGRN_COLS:])
        if T == 1:
            pr, ph, plru = _norm_proj(x, lp['norm1_g'], lp['w_in'])
            p0 = _proj(shift[l], w_r)
            r, w, k, v, kk, ka, g, f, kin, a, u = _mixer_pre_step(pr, ph, plru, p0, conv[l], prm)
            wo, s_wkv = _head_scan(_rwkv_scan_kernel, "rwkv_scan",
                                   [(heads(t), 0) for t in (r, w, k, v, kk, ka)], wkv, l, s_wkv, B, T)
            ph4 = heads(ph)
            ho, s_hgrn = _head_scan(_hgrn_scan_kernel, "hgrn_scan",
                                    [(ph4, 0), (heads(f), 0), (heads(kin), 0), (ph4, 2)], hgrn, l, s_hgrn, B, T)
            lh, h_l = _lru_scan(a, u, lru[l], B, T)
            mix = _mix_post(wo.reshape(m, C_HEADS), r, k, v, g, lp['r_k'], lp['ln_w'], lp['ln_b'],
                            ho.reshape(m, C_HEADS), ph, lp['hgrn_norm_g'], lh, plru)
            o_r, o_h, o_l = mix[:, :C_HEADS], mix[:, C_HEADS:2 * C_HEADS], mix[:, 2 * C_HEADS:]
            c_l = jnp.concatenate([conv[l][:, 1:], plru[:, None, :C_LRU]], axis=1)
        else:
            x3 = x.reshape(B, T, D_MODEL)
            o_r, S_r = _rwkv_seq(x3, shift[l], lp['norm1_g'], w_r, wkv[l],
                                 prm[:8] + [lp['r_k'], lp['ln_w'], lp['ln_b']], B, T)
            o_h, S_h = _hgrn_seq(x3, lp['norm1_g'], w_h, hgrn[l], prm[8], lp['hgrn_norm_g'], B, T)
            o_l, h_l, c_l = _lru_seq(x3, lp['norm1_g'], w_l, conv[l], lru[l], prm[9:], B, T)
            n_wkv.append(S_r)
            n_hgrn.append(S_h)
        x = _out_mlp(x, o_r, o_h, o_l, lp['w_out'], lp['norm2_g'], lp['mlp_w1'], lp['mlp_w2'])
        n_lru.append(h_l)
        n_conv.append(c_l)
    y = _rmsnorm_rows(x, final_g).reshape(B, T, D_MODEL)
    if T != 1:
        s_wkv, s_hgrn = jnp.stack(n_wkv), jnp.stack(n_hgrn)
    return (y, s_wkv, jnp.stack(n_shift), s_hgrn, jnp.stack(n_lru), jnp.stack(n_conv))


def kernel(x_prompt, x_sample, state_wkv, state_shift, state_hgrn, state_lru, state_conv, norm1_g, w_in, mu_shift, rwkv_w0, rwkv_w_up, rwkv_a0, rwkv_a_up, rwkv_g_up, rwkv_k_k, rwkv_k_a, rwkv_r_k, rwkv_ln_w, rwkv_ln_b, hgrn_lb, hgrn_norm_g, lru_conv_w, lru_conv_b, lru_wa, lru_ba, lru_wx, lru_bx, lru_lambda, w_out, norm2_g, mlp_w1, mlp_w2, final_g):
    prm = dict(norm1_g=norm1_g, w_in=w_in, mu_shift=mu_shift, rwkv_w0=rwkv_w0, rwkv_w_up=rwkv_w_up,
               rwkv_a0=rwkv_a0, rwkv_a_up=rwkv_a_up, rwkv_g_up=rwkv_g_up, rwkv_k_k=rwkv_k_k,
               rwkv_k_a=rwkv_k_a, rwkv_r_k=rwkv_r_k.reshape(DEPTH, C_HEADS), rwkv_ln_w=rwkv_ln_w,
               rwkv_ln_b=rwkv_ln_b, hgrn_norm_g=hgrn_norm_g, lru_conv_w=lru_conv_w,
               lru_conv_b=lru_conv_b, lru_wa=lru_wa, lru_ba=lru_ba, lru_wx=lru_wx, lru_bx=lru_bx,
               lru_lambda=lru_lambda, w_out=w_out, norm2_g=norm2_g, mlp_w1=mlp_w1, mlp_w2=mlp_w2)
    layers = _prepare_params(prm)
    lb_all = _hgrn_lower_bounds(hgrn_lb)
    fg = final_g.reshape(1, D_MODEL)
    Bp = x_prompt.shape[0]
    dt = x_prompt.dtype
    z_wkv = jnp.zeros((DEPTH, Bp, N_HEADS, HEAD_DIM, HEAD_DIM), dt)
    z_shift = jnp.zeros((DEPTH, Bp, D_MODEL), dt)
    z_hgrn = jnp.zeros((DEPTH, Bp, N_HEADS, HEAD_DIM, HEAD_DIM), dt)
    z_lru = jnp.zeros((DEPTH, Bp, C_LRU), dt)
    z_conv = jnp.zeros((DEPTH, Bp, CONV_WIDTH - 1, C_LRU), dt)
    y_p, p_wkv, p_shift, p_hgrn, p_lru, p_conv = _trunk(x_prompt, z_wkv, z_shift, z_hgrn, z_lru, z_conv,
                                                         layers, lb_all, fg)
    y_s, s_wkv, s_shift, s_hgrn, s_lru, s_conv = _trunk(x_sample, state_wkv, state_shift, state_hgrn,
                                                         state_lru, state_conv, layers, lb_all, fg)
    return (y_p, y_s, p_wkv, p_shift, p_hgrn, p_lru, p_conv, s_wkv, s_shift, s_hgrn, s_lru, s_conv)
```

```python
import functools

import jax
import jax.numpy as jnp
from jax import lax
from jax.experimental import pallas as pl
from jax.experimental.pallas import tpu as pltpu

F32 = jnp.float32
BF16 = jnp.bfloat16

D_MODEL = 1024
DEPTH = 4
HEAD_DIM = 64
N_HEADS = 6
C_HEADS = N_HEADS * HEAD_DIM
C_LRU = 256
LRU_BLOCKS = 4
CONV_WIDTH = 4
LRU_C = 8.0
W_LORA = 64
A_LORA = 64
G_LORA = 128
RWKV_COLS = 3 * C_HEADS + W_LORA + A_LORA + G_LORA
HGRN_COLS = 4 * C_HEADS
LRU_COLS = 2 * C_LRU
C_IN = RWKV_COLS + HGRN_COLS + LRU_COLS
D_FF = 4 * D_MODEL
NORM_EPS = 1e-6
GN_EPS = 64e-5

LANES = 128
VMEM_LIMIT_BYTES = 48 * 1024 * 1024


def _cparams(*sem):
    return pltpu.CompilerParams(dimension_semantics=sem, vmem_limit_bytes=VMEM_LIMIT_BYTES)


def _dot(a, b):
    return jnp.dot(a.astype(BF16), b.astype(BF16), preferred_element_type=F32)


def _rmsnorm(x, g):
    return x * lax.rsqrt(jnp.mean(x * x, axis=-1, keepdims=True) + NORM_EPS) * g


def _softplus(x):
    return jnp.maximum(x, 0.0) + jnp.log1p(jnp.exp(-jnp.abs(x)))


def _head_sum(x):
    m = x.shape[0]
    lo = lax.broadcasted_iota(jnp.int32, (m, LANES), 1) < HEAD_DIM
    outs = []
    for p in range(x.shape[1] // LANES):
        xp = x[:, p * LANES:(p + 1) * LANES]
        s_lo = jnp.sum(jnp.where(lo, xp, 0.0), axis=1, keepdims=True)
        s_hi = jnp.sum(jnp.where(lo, 0.0, xp), axis=1, keepdims=True)
        outs.append(jnp.where(lo, s_lo, s_hi))
    return jnp.concatenate(outs, axis=1)


def _norm_proj_kernel(x_ref, shift_ref, g_ref, wr_ref, wh_ref, wl_ref, pr_ref, ph_ref, pl_ref, p0_ref):
    xn = _rmsnorm(x_ref[...], g_ref[...]).astype(BF16)
    pr_ref[...] = jnp.dot(xn, wr_ref[...], preferred_element_type=F32)
    ph_ref[...] = jnp.dot(xn, wh_ref[...], preferred_element_type=F32)
    pl_ref[...] = jnp.dot(xn, wl_ref[...], preferred_element_type=F32)
    p0_ref[...] = jnp.dot(shift_ref[...].astype(BF16), wr_ref[...], preferred_element_type=F32)


def _norm_proj(x, shift, g, w_r, w_h, w_l):
    m = x.shape[0]
    return pl.pallas_call(
        _norm_proj_kernel,
        out_shape=[jax.ShapeDtypeStruct((m, c), F32) for c in (RWKV_COLS, HGRN_COLS, LRU_COLS, RWKV_COLS)],
        compiler_params=pltpu.CompilerParams(vmem_limit_bytes=VMEM_LIMIT_BYTES),
        name="norm_proj",
    )(x, shift, g, w_r, w_h, w_l)


def _out_mlp_kernel(x_ref, or_ref, oh_ref, ol_ref, wo_ref, g2_ref, w1_ref, w2_ref, o_ref, x1_s, xn_s, acc_s):
    j = pl.program_id(1)

    @pl.when(j == 0)
    def _():
        mix = jnp.concatenate([or_ref[...], oh_ref[...], ol_ref[...]], axis=1)
        x1 = x_ref[...] + jnp.dot(mix.astype(BF16), wo_ref[...], preferred_element_type=F32)
        x1_s[...] = x1
        xn_s[...] = _rmsnorm(x1, g2_ref[...]).astype(BF16)
        acc_s[...] = jnp.zeros_like(acc_s)

    h = jnp.dot(xn_s[...], w1_ref[...], preferred_element_type=F32)
    h = jnp.square(jnp.maximum(h, 0.0))
    acc_s[...] += jnp.dot(h.astype(BF16), w2_ref[...], preferred_element_type=F32)

    @pl.when(j == pl.num_programs(1) - 1)
    def _():
        o_ref[...] = x1_s[...] + acc_s[...]


def _out_mlp(x, o_r, o_h, o_l, wo, g2, w1, w2):
    m = x.shape[0]
    tm = min(m, 512)
    tf = 2048
    row = pl.BlockSpec((tm, D_MODEL), lambda i, j: (i, 0))
    part = lambda c: pl.BlockSpec((tm, c), lambda i, j: (i, 0))
    return pl.pallas_call(
        _out_mlp_kernel,
        grid=(m // tm, D_FF // tf),
        in_specs=[row, part(C_HEADS), part(C_HEADS), part(C_LRU),
                  pl.BlockSpec((D_MODEL, D_MODEL), lambda i, j: (0, 0)),
                  pl.BlockSpec((1, D_MODEL), lambda i, j: (0, 0)),
                  pl.BlockSpec((D_MODEL, tf), lambda i, j: (0, j)),
                  pl.BlockSpec((tf, D_MODEL), lambda i, j: (j, 0))],
        out_specs=row,
        out_shape=jax.ShapeDtypeStruct((m, D_MODEL), F32),
        scratch_shapes=[pltpu.VMEM((tm, D_MODEL), F32), pltpu.VMEM((tm, D_MODEL), BF16),
                        pltpu.VMEM((tm, D_MODEL), F32)],
        compiler_params=_cparams("parallel", "arbitrary"),
        name="out_mlp",
    )(x, o_r, o_h, o_l, wo, g2, w1, w2)


def _rmsnorm_rows_kernel(x_ref, g_ref, o_ref):
    o_ref[...] = _rmsnorm(x_ref[...], g_ref[...])


def _rmsnorm_rows(x, g):
    m = x.shape[0]
    tm = min(m, 512)
    row = pl.BlockSpec((tm, D_MODEL), lambda i: (i, 0))
    return pl.pallas_call(
        _rmsnorm_rows_kernel,
        grid=(m // tm,),
        in_specs=[row, pl.BlockSpec((1, D_MODEL), lambda i: (0, 0))],
        out_specs=row,
        out_shape=jax.ShapeDtypeStruct((m, D_MODEL), F32),
        compiler_params=_cparams("parallel"),
        name="rmsnorm_rows",
    )(x, g)


def _lb_kernel(p_ref, o_ref):
    p = p_ref[...]
    e = jnp.exp(p - jnp.max(p, axis=0, keepdims=True))
    s = e / jnp.sum(e, axis=0, keepdims=True)
    acc = jnp.zeros_like(s[0:1])
    for l in range(DEPTH):
        if l > 0:
            acc = acc + s[l:l + 1]
        o_ref[l:l + 1, :] = acc


def _hgrn_lower_bounds(hgrn_lb):
    return pl.pallas_call(_lb_kernel, out_shape=jax.ShapeDtypeStruct(hgrn_lb.shape, F32),
                          name="hgrn_lower_bounds")(hgrn_lb)


def _rwkv_pre(pr, prev, mu, w0, w_up, a0, a_up, g_up, k_k, k_a):
    c = C_HEADS
    pm = pr + mu * (prev - pr)
    r = pm[:, :c]
    k0 = pm[:, c:2 * c]
    v = pm[:, 2 * c:3 * c]
    xwa = pm[:, 3 * c:3 * c + W_LORA + A_LORA]
    xg = pm[:, 3 * c + W_LORA + A_LORA:]
    w_log = -_softplus(-(w0 + _dot(jnp.tanh(xwa), w_up))) - 0.5
    log_decay = -jnp.exp(w_log)
    a = jax.nn.sigmoid(a0 + _dot(xwa, a_up))
    g = _dot(jax.nn.sigmoid(xg), g_up)
    kk = k0 * k_k
    kk = kk * lax.rsqrt(jnp.maximum(_head_sum(kk * kk), 1e-24))
    k = k0 * (1.0 + (a - 1.0) * k_a)
    return r, log_decay, k, v, kk, kk * a, g


def _hgrn_pre(fpre, lb):
    sg = jax.nn.sigmoid(fpre)
    f = lb + (1.0 - lb) * sg
    kin = (1.0 - lb) * jax.nn.sigmoid(-fpre)
    return f, kin


def _lru_pre(xb, x1, x2, x3, conv_w, conv_b, wa, ba, wx, bx, lam):
    xc = conv_b + (((x3 * conv_w[0:1] + x2 * conv_w[1:2]) + x1 * conv_w[2:3]) + xb * conv_w[3:4])
    r = jax.nn.sigmoid(_dot(xc, wa) + ba)
    i = jax.nn.sigmoid(_dot(xc, wx) + bx)
    log_a = -LRU_C * r * _softplus(-lam)
    a = jnp.exp(log_a)
    one_minus_a2 = -jnp.tanh(log_a) * (jnp.exp(2.0 * log_a) + 1.0)
    u = xc * i * jnp.sqrt(jnp.maximum(one_minus_a2, 1e-12))
    return a, u


_N_PRE_PARAMS = 16


def _step_pre_kernel(*refs):
    pr_ref, ph_ref, pl_ref, p0_ref, c0_ref, c1_ref, c2_ref, h0_ref = refs[:8]
    (mu, w0, w_up, a0, a_up, g_up, k_k, k_a, lb, conv_w, conv_b, wa, ba, wx, bx, lam) = refs[8:8 + _N_PRE_PARAMS]
    (r_o, k_o, v_o, g_o, rT_o, wT_o, kT_o, vT_o, kkT_o, kaT_o,
     qT_o, fT_o, kinT_o, ivT_o, h_o) = refs[8 + _N_PRE_PARAMS:]
    r, lw, k, v, kk, ka, g = _rwkv_pre(pr_ref[...], p0_ref[...], mu[...], w0[...], w_up[...], a0[...],
                                       a_up[...], g_up[...], k_k[...], k_a[...])
    r_o[...], k_o[...], v_o[...], g_o[...] = r, k, v, g
    for o_ref, val in ((rT_o, r), (wT_o, jnp.exp(lw)), (kT_o, k), (vT_o, v), (kkT_o, kk), (kaT_o, ka)):
        o_ref[...] = val.T
    ph = ph_ref[...]
    f, kin = _hgrn_pre(ph[:, C_HEADS:2 * C_HEADS], lb[...])
    for o_ref, val in ((qT_o, ph[:, :C_HEADS]), (fT_o, f), (kinT_o, kin), (ivT_o, ph[:, 2 * C_HEADS:3 * C_HEADS])):
        o_ref[...] = val.T
    a, u = _lru_pre(pl_ref[...][:, :C_LRU], c2_ref[...], c1_ref[...], c0_ref[...], conv_w[...], conv_b[...],
                    wa[...], ba[...], wx[...], bx[...], lam[...])
    h_o[...] = a * h0_ref[...] + u


def _step_pre(pr, ph, plru, p0, conv0, h0, prm):
    B = pr.shape[0]
    shapes = [(B, C_HEADS)] * 4 + [(C_HEADS, B)] * 10 + [(B, C_LRU)]
    return pl.pallas_call(
        _step_pre_kernel,
        out_shape=[jax.ShapeDtypeStruct(s, F32) for s in shapes],
        compiler_params=pltpu.CompilerParams(vmem_limit_bytes=VMEM_LIMIT_BYTES),
        name="step_pre",
    )(pr, ph, plru, p0, conv0[:, 0], conv0[:, 1], conv0[:, 2], h0, *prm)


def _rwkv_step_kernel(r_ref, w_ref, k_ref, v_ref, kk_ref, ka_ref, s_ref, o_ref, sT_ref):
    kk, w, ka, k, r = kk_ref[...], w_ref[...], ka_ref[...], k_ref[...], r_ref[...]

    def value_row(vi, carry):
        S = s_ref[vi]
        sa = -jnp.sum(S * kk, axis=0, keepdims=True)
        S = S * w + sa * ka + v_ref[pl.ds(vi, 1), :] * k
        sT_ref[vi] = S
        o_ref[pl.ds(vi, 1), :] = jnp.sum(S * r, axis=0, keepdims=True)
        return carry

    lax.fori_loop(0, HEAD_DIM, value_row, 0, unroll=8)


def _hgrn_step_kernel(q_ref, f_ref, kin_ref, iv_ref, s_ref, o_ref, sT_ref):
    iv = iv_ref[...]

    def key_row(ki, acc):
        S = f_ref[pl.ds(ki, 1), :] * s_ref[ki] + kin_ref[pl.ds(ki, 1), :] * iv
        sT_ref[ki] = S
        return acc + S * q_ref[pl.ds(ki, 1), :]

    o_ref[...] = lax.fori_loop(0, HEAD_DIM, key_row, jnp.zeros_like(iv), unroll=8)


def _state_step(kernel, name, vecs, states, layer, new_states):
    B = states.shape[-1]
    n_vec = len(vecs)
    vblk = pl.BlockSpec((HEAD_DIM, B), lambda h: (h, 0))
    sblk = pl.BlockSpec((None, None, HEAD_DIM, HEAD_DIM, B), lambda h: (layer, h, 0, 0, 0))
    in_specs = [vblk] * n_vec + [sblk]
    operands = list(vecs) + [states]
    aliases = {}
    if new_states is not None:
        in_specs.append(pl.BlockSpec(memory_space=pl.ANY))
        operands.append(new_states)
        aliases = {n_vec + 1: 1}

    def body(*refs):
        kernel(*refs[:n_vec + 1], *refs[len(operands):])

    return pl.pallas_call(
        body,
        grid=(N_HEADS,),
        in_specs=in_specs,
        out_specs=[vblk, sblk],
        out_shape=[jax.ShapeDtypeStruct((C_HEADS, B), F32), jax.ShapeDtypeStruct(states.shape, F32)],
        input_output_aliases=aliases,
        compiler_params=_cparams("parallel"),
        name=name,
    )(*operands)


def _rwkv_post(out, r, k, v, g, r_k, ln_w, ln_b):
    inv_n = 1.0 / HEAD_DIM
    mean = _head_sum(out) * inv_n
    d = out - mean
    var = _head_sum(d * d) * inv_n
    gn = d * lax.rsqrt(var + GN_EPS) * ln_w + ln_b
    bonus = _head_sum(r * k * r_k) * v
    return (gn + bonus) * g


def _hgrn_post(o, gate, norm_g):
    o = o * lax.rsqrt(_head_sum(o * o) * (1.0 / HEAD_DIM) + NORM_EPS)
    return o * norm_g * jax.nn.silu(gate)


def _step_post_kernel(woT_ref, r_ref, k_ref, v_ref, g_ref, rk_ref, lnw_ref, lnb_ref,
                      hoT_ref, hg_ref, hng_ref, lh_ref, lg_ref, or_ref, oh_ref, ol_ref):
    or_ref[...] = _rwkv_post(woT_ref[...].T, r_ref[...], k_ref[...], v_ref[...], g_ref[...],
                             rk_ref[...], lnw_ref[...], lnb_ref[...])
    oh_ref[...] = _hgrn_post(hoT_ref[...].T, hg_ref[...], hng_ref[...])
    ol_ref[...] = lh_ref[...] * jax.nn.gelu(lg_ref[...])


def _step_post(woT, r, k, v, g, r_k, ln_w, ln_b, hoT, ph, hng, lh, plru):
    B = r.shape[0]
    full = lambda a: pl.BlockSpec(a.shape, lambda i: (0,) * a.ndim)
    return pl.pallas_call(
        _step_post_kernel,
        grid=(1,),
        in_specs=[full(woT), full(r), full(k), full(v), full(g), full(r_k), full(ln_w), full(ln_b),
                  full(hoT), pl.BlockSpec((B, C_HEADS), lambda i: (0, 3)), full(hng), full(lh),
                  pl.BlockSpec((B, C_LRU), lambda i: (0, 1))],
        out_specs=[pl.BlockSpec((B, c), lambda i: (0, 0)) for c in (C_HEADS, C_HEADS, C_LRU)],
        out_shape=[jax.ShapeDtypeStruct((B, c), F32) for c in (C_HEADS, C_HEADS, C_LRU)],
        compiler_params=_cparams("arbitrary"),
        name="step_post",
    )(woT, r, k, v, g, r_k, ln_w, ln_b, hoT, ph, hng, lh, plru)


CHUNK = 64
CHUNK_LOG2 = 6
SEQS = 4
GROUP_HEADS = 4
GROUP = GROUP_HEADS * HEAD_DIM
N_GROUPS = SEQS * N_HEADS // GROUP_HEADS

NN = ((1,), (0,))
NT = ((1,), (1,))


def _mm(a, b, dims, passes):
    dn = (dims, ((), ()))
    if passes == 6:
        return lax.dot_general(a, b, dn, precision=lax.Precision.HIGHEST, preferred_element_type=F32)
    ah, bh = a.astype(BF16), b.astype(BF16)
    out = lax.dot_general(ah, bh, dn, preferred_element_type=F32)
    if passes == 3:
        al = (a - ah.astype(F32)).astype(BF16)
        bl = (b - bh.astype(F32)).astype(BF16)
        out = (out + lax.dot_general(ah, bl, dn, preferred_element_type=F32)
               + lax.dot_general(al, bh, dn, preferred_element_type=F32))
    return out


def _iota2(shape):
    return lax.broadcasted_iota(jnp.int32, shape, 0), lax.broadcasted_iota(jnp.int32, shape, 1)


def _head_of(group, i):
    hw = group * GROUP_HEADS + i
    return hw // N_HEADS, hw % N_HEADS


def _group_state_load(s_ref, S_s, transpose):
    z = jnp.zeros((HEAD_DIM, HEAD_DIM), F32)
    for gi in range(N_GROUPS):
        rows = []
        for i in range(GROUP_HEADS):
            b, h = _head_of(gi, i)
            blk = s_ref[b, h].T if transpose else s_ref[b, h]
            rows.append(jnp.concatenate([blk if i2 == i else z for i2 in range(GROUP_HEADS)], axis=1))
        S_s[gi] = jnp.concatenate(rows, axis=0)


def _group_state_store(S_s, s_ref, transpose):
    for gi in range(N_GROUPS):
        Sg = S_s[gi]
        for i in range(GROUP_HEADS):
            b, h = _head_of(gi, i)
            blk = Sg[i * HEAD_DIM:(i + 1) * HEAD_DIM, i * HEAD_DIM:(i + 1) * HEAD_DIM]
            s_ref[b, h] = blk.T if transpose else blk


def _wide(x):
    return jnp.concatenate([x[b * CHUNK:(b + 1) * CHUNK] for b in range(SEQS)], axis=1)


def _tall(x):
    return jnp.concatenate([x[:, b * C_HEADS:(b + 1) * C_HEADS] for b in range(SEQS)], axis=0)


def _seq_cumsum(x):
    tr, tc = _iota2((SEQS * CHUNK, SEQS * CHUNK))
    tri = ((tr >= tc) & ((tr >> CHUNK_LOG2) == (tc >> CHUNK_LOG2))).astype(F32)
    return _mm(tri, x, NN, 6)


def _lane_head():
    return lax.broadcasted_iota(jnp.int32, (CHUNK, GROUP), 1) >> 6


def _by_head(x, lane_head):
    return jnp.concatenate([jnp.where(lane_head == i, x, 0.0) for i in range(GROUP_HEADS)], axis=0)


def _rep(x):
    return jnp.concatenate([x] * GROUP_HEADS, axis=0)


def _pair_masks():
    lane = lax.broadcasted_iota(jnp.int32, (CHUNK, GROUP), 1)
    return lane < 2 * HEAD_DIM, (lane & HEAD_DIM) == 0


def _split_rows(x, mask):
    return jnp.concatenate([jnp.where(mask, x, 0.0), jnp.where(mask, 0.0, x)], axis=0)


def _pick(x, lane_head):
    out = x[:CHUNK]
    for i in range(1, GROUP_HEADS):
        out = jnp.where(lane_head == i, x[i * CHUNK:(i + 1) * CHUNK], out)
    return out


def _unit_lower_inverse(Ls):
    r, c = _iota2(Ls[0].shape)
    eye = (r == c).astype(F32)
    diag_blk = (r >> 4) == (c >> 4)
    mm = lambda xs, ys: [_mm(x, y, NN, 1) for x, y in zip(xs, ys)]
    Ld = [jnp.where(diag_blk, L, 0.0) for L in Ls]
    N = [jnp.where(diag_blk, 0.0, L) for L in Ls]
    L2 = mm(Ld, Ld)
    L4 = mm(L2, L2)
    P1 = mm([eye - x for x in Ld], [eye + x for x in L2])
    L8 = mm(L4, L4)
    P2 = mm([eye + x for x in L4], [eye + x for x in L8])
    Dinv = mm(P1, P2)
    M = mm(Dinv, N)
    M2 = mm(M, M)
    X = mm([eye - x for x in M], [eye + x for x in M2])
    return mm(X, Dinv)


def _seq_project(x_ref, norm_g, w_ref):
    x = x_ref[...].reshape(-1, D_MODEL)
    return jnp.dot(_rmsnorm(x, norm_g[...]).astype(BF16), w_ref[...], preferred_element_type=F32)


def _rwkv_seq_kernel(x_ref, shift_ref, norm_g, w_ref, s0_ref,
                     mu, w0, w_up, a0, a_up, g_up, k_k, k_a, r_k, ln_w, ln_b,
                     o_ref, sT_ref, carry_s, S_s):
    j = pl.program_id(1)
    C = CHUNK

    @pl.when(j == 0)
    def _():
        p0 = jnp.dot(shift_ref[...].reshape(SEQS, D_MODEL).astype(BF16), w_ref[...], preferred_element_type=F32)
        carry_s[...] = p0.reshape(SEQS, 1, RWKV_COLS)
        _group_state_load(s0_ref, S_s, transpose=False)

    pr = _seq_project(x_ref, norm_g, w_ref)
    prow = lax.broadcasted_iota(jnp.int32, pr.shape, 0)
    prev = pltpu.roll(pr, 1, 0)
    for b in range(SEQS):
        prev = jnp.where(prow == b * C, carry_s[b], prev)
        carry_s[b] = pr[(b + 1) * C - 1:(b + 1) * C, :]

    r, lw, k, v, kk, kb, g = _rwkv_pre(pr, prev, mu[...], w0[...], w_up[...], a0[...], a_up[...],
                                       g_up[...], k_k[...], k_a[...])
    cum_w, lw_w = _wide(_seq_cumsum(lw)), _wide(lw)
    r_w, k_w, v_w, kk_w, kb_w = _wide(r), _wide(k), _wide(v), _wide(kk), _wide(kb)
    cum_last = cum_w[C - 1:C, :]
    e_inv = jnp.exp(-cum_w)
    e_rel = jnp.exp(cum_last - cum_w)
    e_last = jnp.exp(cum_last)
    Kg = kk_w * jnp.exp(cum_w - lw_w)
    Rg = r_w * jnp.exp(cum_w)
    Ki, Bi = k_w * e_inv, kb_w * e_inv
    Kt, Bt = k_w * e_rel, kb_w * e_rel

    lane_head = _lane_head()
    low_pair, even_head = _pair_masks()
    by_pair = lambda x: _split_rows(x, low_pair)
    by_parity = lambda x: _split_rows(x, even_head)
    gr, gc = _iota2((GROUP, GROUP))
    same_head = (gr >> 6) == (gc >> 6)
    strict_bd = same_head & ((gr & (HEAD_DIM - 1)) > (gc & (HEAD_DIM - 1)))
    ar, ac = _iota2((2 * C, 2 * C))
    strict = (ar & (C - 1)) > (ac & (C - 1))
    incl = (ar & (C - 1)) >= (ac & (C - 1))

    groups = range(N_GROUPS)
    sl = [slice(gi * GROUP, (gi + 1) * GROUP) for gi in groups]
    Akb = [jnp.where(strict_bd, _mm(_by_head(Kg[:, s], lane_head), _rep(Bi[:, s]), NT, 1), 0.0) for s in sl]
    P = [_mm(jnp.concatenate([by_pair(Kg[:, s]), by_pair(Rg[:, s])], axis=0),
             jnp.concatenate([by_parity(Ki[:, s]), by_parity(Bi[:, s])], axis=0), NT, 1) for s in sl]
    Tinv = _unit_lower_inverse(Akb)
    S = [S_s[gi] for gi in groups]
    H0 = [_mm(jnp.concatenate([Kg[:, s], Rg[:, s]], axis=0), S[gi], NT, 1) for gi, s in enumerate(sl)]
    Vp = [by_parity(v_w[:, s]) for s in sl]
    AkkV = [_mm(jnp.where(strict, P[gi][:2 * C, :2 * C], 0.0), Vp[gi], NN, 1) for gi in groups]
    rhs_u = [jnp.concatenate([H0[gi][:C] + AkkV[gi][:C]] * 2 + [H0[gi][:C] + AkkV[gi][C:]] * 2, axis=0)
             for gi in groups]
    U = [_pick(_mm(Tinv[gi], rhs_u[gi], NN, 1), lane_head) for gi in groups]
    O = []
    for gi in groups:
        Y = _mm(jnp.concatenate([jnp.where(incl, P[gi][2 * C:, :2 * C], 0.0),
                                 jnp.where(incl, -P[gi][2 * C:, 2 * C:], 0.0)], axis=1),
                jnp.concatenate([Vp[gi], by_parity(U[gi])], axis=0), NN, 1)
        O.append(H0[gi][C:] + jnp.where(low_pair, Y[:C], Y[C:]))
    for gi, s in enumerate(sl):
        Z = _mm(jnp.concatenate([v_w[:, s], U[gi]], axis=0).T,
                jnp.concatenate([Kt[:, s], -Bt[:, s]], axis=0), NN, 1)
        S_s[gi] = S[gi] * e_last[:, s] + jnp.where(same_head, Z, 0.0)

    res = _rwkv_post(_tall(jnp.concatenate(O, axis=1)), r, k, v, g, r_k[...], ln_w[...], ln_b[...])
    o_ref[...] = res.reshape(SEQS, C, C_HEADS)

    @pl.when(j == pl.num_programs(1) - 1)
    def _():
        _group_state_store(S_s, sT_ref, transpose=False)


def _hgrn_seq_kernel(x_ref, norm1_g, w_ref, s0_ref, lb, norm_g, o_ref, sT_ref, S_s):
    j = pl.program_id(1)
    C = CHUNK

    @pl.when(j == 0)
    def _():
        _group_state_load(s0_ref, S_s, transpose=True)

    ph = _seq_project(x_ref, norm1_g, w_ref)
    f, kin = _hgrn_pre(ph[:, C_HEADS:2 * C_HEADS], lb[...])
    cum = _wide(_seq_cumsum(jnp.log(f)))
    q_w, kin_w, iv_w = _wide(ph[:, :C_HEADS]), _wide(kin), _wide(ph[:, 2 * C_HEADS:3 * C_HEADS])
    cum_last = cum[C - 1:C, :]
    e_last = jnp.exp(cum_last)
    qd = q_w * jnp.exp(cum)
    kd = kin_w * jnp.exp(cum_last - cum)

    low_pair, even_head = _pair_masks()
    by_pair = lambda x: _split_rows(x, low_pair)
    by_parity = lambda x: _split_rows(x, even_head)
    ar, ac = _iota2((2 * C, 2 * C))
    t_pos, s_pos = ar & (C - 1), ac & (C - 1)
    differ = jnp.where(t_pos > s_pos, t_pos ^ s_pos, 0)
    gr, gc = _iota2((GROUP, GROUP))
    same_head = (gr >> 6) == (gc >> 6)
    groups = range(N_GROUPS)
    sl = [slice(gi * GROUP, (gi + 1) * GROUP) for gi in groups]

    att = [jnp.where(t_pos == s_pos, _mm(by_pair(q_w[:, s]), by_parity(kin_w[:, s]), NT, 1), 0.0) for s in sl]
    row = lax.broadcasted_iota(jnp.int32, cum.shape, 0)
    last = cum
    for l in range(CHUNK_LOG2):
        m = 1 << l
        in_low = (row & (2 * m - 1)) < m
        d = cum - jnp.where(in_low, last, pltpu.roll(last, m, 0))
        last = jnp.where(in_low, pltpu.roll(last, C - m, 0), last)
        ql = q_w * jnp.exp(jnp.minimum(d, 0.0))
        kl = kin_w * jnp.exp(jnp.minimum(-d, 0.0))
        level = (differ >> l) == 1
        att = [jnp.where(level, _mm(by_pair(ql[:, s]), by_parity(kl[:, s]), NT, 1), att[gi])
               for gi, s in enumerate(sl)]

    outs = []
    for gi, s in enumerate(sl):
        S = S_s[gi]
        v = iv_w[:, s]
        intra = _mm(att[gi], by_parity(v), NN, 1)
        outs.append(jnp.where(low_pair, intra[:C], intra[C:]) + _mm(qd[:, s], S, NT, 1))
        Z = _mm(v.T, kd[:, s], NN, 1)
        S_s[gi] = S * e_last[:, s] + jnp.where(same_head, Z, 0.0)

    res = _hgrn_post(_tall(jnp.concatenate(outs, axis=1)), ph[:, 3 * C_HEADS:], norm_g[...])
    o_ref[...] = res.reshape(SEQS, C, C_HEADS)

    @pl.when(j == pl.num_programs(1) - 1)
    def _():
        _group_state_store(S_s, sT_ref, transpose=True)


def _lru_seq_kernel(x_ref, norm_g, w_ref, c0_ref, h0_ref, conv_w, conv_b, wa, ba, wx, bx, lam,
                    o_ref, hT_ref, cT_ref, xbuf_s, h_s, a_s, u_s, *, tm):
    j = pl.program_id(1)

    @pl.when(j == 0)
    def _():
        xbuf_s[5:8, :] = c0_ref[0]
        h_s[...] = h0_ref[0]

    blk = _seq_project(x_ref, norm_g, w_ref)
    xb = blk[:, :C_LRU]
    xbuf_s[8:8 + tm, :] = xb
    x1 = xbuf_s[7:7 + tm, :]
    x2 = xbuf_s[6:6 + tm, :]
    x3 = xbuf_s[5:5 + tm, :]
    a, u = _lru_pre(xb, x1, x2, x3, conv_w[...], conv_b[...], wa[...], ba[...], wx[...], bx[...], lam[...])
    xbuf_s[5:8, :] = xb[tm - 3:tm, :]

    row8 = lax.broadcasted_iota(jnp.int32, a.shape, 0) & 7
    for d in (1, 2, 4):
        ok = row8 >= d
        u = jnp.where(ok, a * pltpu.roll(u, d, 0) + u, u)
        a = jnp.where(ok, a * pltpu.roll(a, d, 0), a)
    a_s[...] = a
    u_s[...] = u

    def group(gi, h):
        i = pl.multiple_of(gi * 8, 8)
        ht = u_s[pl.ds(i, 8), :] + a_s[pl.ds(i, 8), :] * h
        o_ref[0, pl.ds(i, 8), :] = ht
        return ht[7:8, :]

    h = lax.fori_loop(0, tm // 8, group, h_s[...])
    h_s[...] = h
    o_ref[0] = o_ref[0] * jax.nn.gelu(blk[:, C_LRU:])

    @pl.when(j == pl.num_programs(1) - 1)
    def _():
        hT_ref[0] = h
        cT_ref[0] = xb[tm - (CONV_WIDTH - 1):tm, :]


def _full_spec(a):
    return pl.BlockSpec(a.shape, lambda b, j: (0,) * a.ndim)


def _seq_block(c):
    return pl.BlockSpec((SEQS, CHUNK, c), lambda b, j: (b, j, 0))


_STATE_BLOCK = pl.BlockSpec((SEQS, N_HEADS, HEAD_DIM, HEAD_DIM), lambda b, j: (b, 0, 0, 0))
_GROUP_STATE = pltpu.VMEM((N_GROUPS, GROUP, GROUP), F32)


def _rwkv_seq(x, shift, norm_g, w, s0, prm, B, T):
    o, sT = pl.pallas_call(
        _rwkv_seq_kernel,
        grid=(B // SEQS, T // CHUNK),
        in_specs=[_seq_block(D_MODEL), pl.BlockSpec((SEQS, 1, D_MODEL), lambda b, j: (b, 0, 0)),
                  _full_spec(norm_g), _full_spec(w), _STATE_BLOCK] + [_full_spec(a) for a in prm],
        out_specs=[_seq_block(C_HEADS), _STATE_BLOCK],
        out_shape=[jax.ShapeDtypeStruct((B, T, C_HEADS), F32), jax.ShapeDtypeStruct(s0.shape, F32)],
        scratch_shapes=[pltpu.VMEM((SEQS, 1, RWKV_COLS), F32), _GROUP_STATE],
        compiler_params=_cparams("parallel", "arbitrary"),
        name="rwkv_seq",
    )(x, shift.reshape(B, 1, D_MODEL), norm_g, w, s0, *prm)
    return o.reshape(B * T, C_HEADS), sT


def _hgrn_seq(x, norm1_g, w, s0, lb, norm_g, B, T):
    o, sT = pl.pallas_call(
        _hgrn_seq_kernel,
        grid=(B // SEQS, T // CHUNK),
        in_specs=[_seq_block(D_MODEL), _full_spec(norm1_g), _full_spec(w), _STATE_BLOCK,
                  _full_spec(lb), _full_spec(norm_g)],
        out_specs=[_seq_block(C_HEADS), _STATE_BLOCK],
        out_shape=[jax.ShapeDtypeStruct((B, T, C_HEADS), F32), jax.ShapeDtypeStruct(s0.shape, F32)],
        scratch_shapes=[_GROUP_STATE],
        compiler_params=_cparams("parallel", "arbitrary"),
        name="hgrn_seq",
    )(x, norm1_g, w, s0, lb, norm_g)
    return o.reshape(B * T, C_HEADS), sT


def _lru_seq(x, norm_g, w, conv0, h0, prm, B, T):
    tm = min(T, 256)
    seq = lambda c: pl.BlockSpec((1, tm, c), lambda b, j: (b, j, 0))
    hblk = pl.BlockSpec((1, 1, C_LRU), lambda b, j: (b, 0, 0))
    cblk = pl.BlockSpec((1, CONV_WIDTH - 1, C_LRU), lambda b, j: (b, 0, 0))
    o, hT, cT = pl.pallas_call(
        functools.partial(_lru_seq_kernel, tm=tm),
        grid=(B, T // tm),
        in_specs=[seq(D_MODEL), _full_spec(norm_g), _full_spec(w), cblk, hblk] + [_full_spec(a) for a in prm],
        out_specs=[seq(C_LRU), hblk, cblk],
        out_shape=[jax.ShapeDtypeStruct((B, T, C_LRU), F32), jax.ShapeDtypeStruct((B, 1, C_LRU), F32),
                   jax.ShapeDtypeStruct((B, CONV_WIDTH - 1, C_LRU), F32)],
        scratch_shapes=[pltpu.VMEM((tm + 8, C_LRU), F32), pltpu.VMEM((1, C_LRU), F32),
                        pltpu.VMEM((tm, C_LRU), F32), pltpu.VMEM((tm, C_LRU), F32)],
        compiler_params=_cparams("parallel", "arbitrary"),
        name="lru_seq",
    )(x, norm_g, w, conv0, h0.reshape(B, 1, C_LRU), *prm)
    return o.reshape(B * T, C_LRU), hT.reshape(B, C_LRU), cT


def _block_diag(w):
    out = jnp.zeros((C_LRU, C_LRU), w.dtype)
    n = C_LRU // LRU_BLOCKS
    for i in range(LRU_BLOCKS):
        out = out.at[i * n:(i + 1) * n, i * n:(i + 1) * n].set(w[i])
    return out


def _prepare_params(p):
    row = lambda a: a.reshape(1, -1)
    zpad = jnp.zeros((W_LORA, C_HEADS), F32)
    layers = []
    for l in range(DEPTH):
        pre = (row(p['mu_shift'][l]), row(p['rwkv_w0'][l]),
               jnp.concatenate([p['rwkv_w_up'][l], zpad], 0).astype(BF16),
               row(p['rwkv_a0'][l]),
               jnp.concatenate([zpad, p['rwkv_a_up'][l]], 0).astype(BF16),
               p['rwkv_g_up'][l].astype(BF16), row(p['rwkv_k_k'][l]), row(p['rwkv_k_a'][l]),
               None,
               p['lru_conv_w'][l], row(p['lru_conv_b'][l]),
               _block_diag(p['lru_wa'][l]).astype(BF16), row(p['lru_ba'][l]),
               _block_diag(p['lru_wx'][l]).astype(BF16), row(p['lru_bx'][l]), row(p['lru_lambda'][l]))
        w_in = p['w_in'][l]
        layers.append(dict(
            norm1_g=row(p['norm1_g'][l]), pre=pre,
            w_r=w_in[:, :RWKV_COLS].astype(BF16),
            w_h=w_in[:, RWKV_COLS:RWKV_COLS + HGRN_COLS].astype(BF16),
            w_l=w_in[:, RWKV_COLS + HGRN_COLS:].astype(BF16),
            r_k=row(p['rwkv_r_k'][l]), ln_w=row(p['rwkv_ln_w'][l]), ln_b=row(p['rwkv_ln_b'][l]),
            hgrn_norm_g=row(p['hgrn_norm_g'][l]),
            w_out=p['w_out'][l].astype(BF16), norm2_g=row(p['norm2_g'][l]),
            mlp_w1=p['mlp_w1'][l].astype(BF16), mlp_w2=p['mlp_w2'][l].astype(BF16)))
    return layers


def _trunk(x, wkv, shift, hgrn, lru, conv, layers, lb_all, final_g):
    B, T, _ = x.shape
    m = B * T
    assert T == 1 or (T % CHUNK == 0 and B % SEQS == 0), "sequence kernels tile (B, T) by (SEQS, CHUNK)"
    x = x.reshape(m, D_MODEL)
    n_wkv, n_shift, n_hgrn, n_lru, n_conv = [], [], [], [], []
    if T == 1:
        wkv_t, hgrn_t = jnp.transpose(wkv, (0, 2, 3, 4, 1)), jnp.transpose(hgrn, (0, 2, 3, 4, 1))
        s_wkv_t = s_hgrn_t = None
    for l, lp in enumerate(layers):
        n_shift.append(_rmsnorm_rows(x.reshape(B, T, D_MODEL)[:, -1], lp['norm1_g']))
        prm = list(lp['pre'])
        prm[8] = lb_all[l:l + 1]
        if T == 1:
            pr, ph, plru, p0 = _norm_proj(x, shift[l], lp['norm1_g'], lp['w_r'], lp['w_h'], lp['w_l'])
            (r, k, v, g, rT, wT, kT, vT, kkT, kaT, qT, fT, kinT, ivT, h_l) = _step_pre(
                pr, ph, plru, p0, conv[l], lru[l], prm)
            woT, s_wkv_t = _state_step(_rwkv_step_kernel, "rwkv_step", (rT, wT, kT, vT, kkT, kaT),
                                       wkv_t, l, s_wkv_t)
            hoT, s_hgrn_t = _state_step(_hgrn_step_kernel, "hgrn_step", (qT, fT, kinT, ivT),
                                        hgrn_t, l, s_hgrn_t)
            o_r, o_h, o_l = _step_post(woT, r, k, v, g, lp['r_k'], lp['ln_w'], lp['ln_b'],
                                       hoT, ph, lp['hgrn_norm_g'], h_l, plru)
            c_l = jnp.concatenate([conv[l][:, 1:], plru[:, None, :C_LRU]], axis=1)
        else:
            x3 = x.reshape(B, T, D_MODEL)
            o_r, S_r = _rwkv_seq(x3, shift[l], lp['norm1_g'], lp['w_r'], wkv[l],
                                 prm[:8] + [lp['r_k'], lp['ln_w'], lp['ln_b']], B, T)
            o_h, S_h = _hgrn_seq(x3, lp['norm1_g'], lp['w_h'], hgrn[l], prm[8], lp['hgrn_norm_g'], B, T)
            o_l, h_l, c_l = _lru_seq(x3, lp['norm1_g'], lp['w_l'], conv[l], lru[l], prm[9:], B, T)
            n_wkv.append(S_r)
            n_hgrn.append(S_h)
        x = _out_mlp(x, o_r, o_h, o_l, lp['w_out'], lp['norm2_g'], lp['mlp_w1'], lp['mlp_w2'])
        n_lru.append(h_l)
        n_conv.append(c_l)
    y = _rmsnorm_rows(x, final_g).reshape(B, T, D_MODEL)
    if T == 1:
        s_wkv, s_hgrn = jnp.transpose(s_wkv_t, (0, 4, 1, 2, 3)), jnp.transpose(s_hgrn_t, (0, 4, 1, 2, 3))
    else:
        s_wkv, s_hgrn = jnp.stack(n_wkv), jnp.stack(n_hgrn)
    return (y, s_wkv, jnp.stack(n_shift), s_hgrn, jnp.stack(n_lru), jnp.stack(n_conv))


def kernel(x_prompt, x_sample, state_wkv, state_shift, state_hgrn, state_lru, state_conv, norm1_g, w_in, mu_shift, rwkv_w0, rwkv_w_up, rwkv_a0, rwkv_a_up, rwkv_g_up, rwkv_k_k, rwkv_k_a, rwkv_r_k, rwkv_ln_w, rwkv_ln_b, hgrn_lb, hgrn_norm_g, lru_conv_w, lru_conv_b, lru_wa, lru_ba, lru_wx, lru_bx, lru_lambda, w_out, norm2_g, mlp_w1, mlp_w2, final_g):
    prm = dict(norm1_g=norm1_g, w_in=w_in, mu_shift=mu_shift, rwkv_w0=rwkv_w0, rwkv_w_up=rwkv_w_up,
               rwkv_a0=rwkv_a0, rwkv_a_up=rwkv_a_up, rwkv_g_up=rwkv_g_up, rwkv_k_k=rwkv_k_k,
               rwkv_k_a=rwkv_k_a, rwkv_r_k=rwkv_r_k.reshape(DEPTH, C_HEADS), rwkv_ln_w=rwkv_ln_w,
               rwkv_ln_b=rwkv_ln_b, hgrn_norm_g=hgrn_norm_g, lru_conv_w=lru_conv_w,
               lru_conv_b=lru_conv_b, lru_wa=lru_wa, lru_ba=lru_ba, lru_wx=lru_wx, lru_bx=lru_bx,
               lru_lambda=lru_lambda, w_out=w_out, norm2_g=norm2_g, mlp_w1=mlp_w1, mlp_w2=mlp_w2)
    layers = _prepare_params(prm)
    lb_all = _hgrn_lower_bounds(hgrn_lb)
    fg = final_g.reshape(1, D_MODEL)
    Bp = x_prompt.shape[0]
    dt = x_prompt.dtype
    z_wkv = jnp.zeros((DEPTH, Bp, N_HEADS, HEAD_DIM, HEAD_DIM), dt)
    z_shift = jnp.zeros((DEPTH, Bp, D_MODEL), dt)
    z_hgrn = jnp.zeros((DEPTH, Bp, N_HEADS, HEAD_DIM, HEAD_DIM), dt)
    z_lru = jnp.zeros((DEPTH, Bp, C_LRU), dt)
    z_conv = jnp.zeros((DEPTH, Bp, CONV_WIDTH - 1, C_LRU), dt)
    y_p, p_wkv, p_shift, p_hgrn, p_lru, p_conv = _trunk(x_prompt, z_wkv, z_shift, z_hgrn, z_lru, z_conv,
                                                         layers, lb_all, fg)
    y_s, s_wkv, s_shift, s_hgrn, s_lru, s_conv = _trunk(x_sample, state_wkv, state_shift, state_hgrn,
                                                         state_lru, state_conv, layers, lb_all, fg)
    return (y_p, y_s, p_wkv, p_shift, p_hgrn, p_lru, p_conv, s_wkv, s_shift, s_hgrn, s_lru, s_conv)
```

```python
import functools
import math

import jax
import jax.numpy as jnp
from jax import lax
from jax.experimental import pallas as pl
from jax.experimental.pallas import tpu as pltpu

F32 = jnp.float32
BF16 = jnp.bfloat16

D_MODEL = 1024
DEPTH = 4
HEAD_DIM = 64
N_HEADS = 6
C_HEADS = N_HEADS * HEAD_DIM
C_LRU = 256
LRU_BLOCKS = 4
CONV_WIDTH = 4
LRU_C = 8.0
W_LORA = 64
A_LORA = 64
G_LORA = 128
RWKV_COLS = 3 * C_HEADS + W_LORA + A_LORA + G_LORA
HGRN_COLS = 4 * C_HEADS
LRU_COLS = 2 * C_LRU
C_IN = RWKV_COLS + HGRN_COLS + LRU_COLS
D_FF = 4 * D_MODEL
NORM_EPS = 1e-6
GN_EPS = 64e-5
DECAY_SCALE = math.exp(-0.5)

LANES = 128
VMEM_LIMIT_BYTES = 48 * 1024 * 1024


def _cparams(*sem):
    return pltpu.CompilerParams(dimension_semantics=sem, vmem_limit_bytes=VMEM_LIMIT_BYTES)


def _dot(a, b):
    return jnp.dot(a.astype(BF16), b.astype(BF16), preferred_element_type=F32)


def _rmsnorm(x, g):
    return x * lax.rsqrt(jnp.mean(x * x, axis=-1, keepdims=True) + NORM_EPS) * g


def _softplus(x):
    return jnp.maximum(x, 0.0) + jnp.log1p(jnp.exp(-jnp.abs(x)))


def _head_sum(x):
    m = x.shape[0]
    lo = lax.broadcasted_iota(jnp.int32, (m, LANES), 1) < HEAD_DIM
    outs = []
    for p in range(x.shape[1] // LANES):
        xp = x[:, p * LANES:(p + 1) * LANES]
        s_lo = jnp.sum(jnp.where(lo, xp, 0.0), axis=1, keepdims=True)
        s_hi = jnp.sum(jnp.where(lo, 0.0, xp), axis=1, keepdims=True)
        outs.append(jnp.where(lo, s_lo, s_hi))
    return jnp.concatenate(outs, axis=1)


def _norm_proj_kernel(x_ref, shift_ref, g_ref, wr_ref, wh_ref, wl_ref, pr_ref, ph_ref, pl_ref, p0_ref):
    xn = _rmsnorm(x_ref[...], g_ref[...]).astype(BF16)
    pr_ref[...] = jnp.dot(xn, wr_ref[...], preferred_element_type=F32)
    ph_ref[...] = jnp.dot(xn, wh_ref[...], preferred_element_type=F32)
    pl_ref[...] = jnp.dot(xn, wl_ref[...], preferred_element_type=F32)
    p0_ref[...] = jnp.dot(shift_ref[...].astype(BF16), wr_ref[...], preferred_element_type=F32)


def _norm_proj(x, shift, g, w_r, w_h, w_l):
    m = x.shape[0]
    return pl.pallas_call(
        _norm_proj_kernel,
        out_shape=[jax.ShapeDtypeStruct((m, c), F32) for c in (RWKV_COLS, HGRN_COLS, LRU_COLS, RWKV_COLS)],
        compiler_params=pltpu.CompilerParams(vmem_limit_bytes=VMEM_LIMIT_BYTES),
        name="norm_proj",
    )(x, shift, g, w_r, w_h, w_l)


def _out_mlp_kernel(x_ref, or_ref, oh_ref, ol_ref, wo_ref, g2_ref, w1_ref, w2_ref, o_ref, x1_s, xn_s, acc_s):
    j = pl.program_id(1)

    @pl.when(j == 0)
    def _():
        mix = jnp.concatenate([or_ref[...], oh_ref[...], ol_ref[...]], axis=1)
        x1 = x_ref[...] + jnp.dot(mix.astype(BF16), wo_ref[...], preferred_element_type=F32)
        x1_s[...] = x1
        xn_s[...] = _rmsnorm(x1, g2_ref[...]).astype(BF16)
        acc_s[...] = jnp.zeros_like(acc_s)

    h = jnp.dot(xn_s[...], w1_ref[...], preferred_element_type=F32)
    h = jnp.square(jnp.maximum(h, 0.0))
    acc_s[...] += jnp.dot(h.astype(BF16), w2_ref[...], preferred_element_type=F32)

    @pl.when(j == pl.num_programs(1) - 1)
    def _():
        o_ref[...] = x1_s[...] + acc_s[...]


def _out_mlp(x, o_r, o_h, o_l, wo, g2, w1, w2):
    m = x.shape[0]
    tm = min(m, 512)
    tf = 2048
    row = pl.BlockSpec((tm, D_MODEL), lambda i, j: (i, 0))
    part = lambda c: pl.BlockSpec((tm, c), lambda i, j: (i, 0))
    return pl.pallas_call(
        _out_mlp_kernel,
        grid=(m // tm, D_FF // tf),
        in_specs=[row, part(C_HEADS), part(C_HEADS), part(C_LRU),
                  pl.BlockSpec((D_MODEL, D_MODEL), lambda i, j: (0, 0)),
                  pl.BlockSpec((1, D_MODEL), lambda i, j: (0, 0)),
                  pl.BlockSpec((D_MODEL, tf), lambda i, j: (0, j)),
                  pl.BlockSpec((tf, D_MODEL), lambda i, j: (j, 0))],
        out_specs=row,
        out_shape=jax.ShapeDtypeStruct((m, D_MODEL), F32),
        scratch_shapes=[pltpu.VMEM((tm, D_MODEL), F32), pltpu.VMEM((tm, D_MODEL), BF16),
                        pltpu.VMEM((tm, D_MODEL), F32)],
        compiler_params=_cparams("parallel", "arbitrary"),
        name="out_mlp",
    )(x, o_r, o_h, o_l, wo, g2, w1, w2)


def _rmsnorm_rows_kernel(x_ref, g_ref, o_ref):
    o_ref[...] = _rmsnorm(x_ref[...], g_ref[...])


def _rmsnorm_rows(x, g):
    m = x.shape[0]
    tm = min(m, 512)
    row = pl.BlockSpec((tm, D_MODEL), lambda i: (i, 0))
    return pl.pallas_call(
        _rmsnorm_rows_kernel,
        grid=(m // tm,),
        in_specs=[row, pl.BlockSpec((1, D_MODEL), lambda i: (0, 0))],
        out_specs=row,
        out_shape=jax.ShapeDtypeStruct((m, D_MODEL), F32),
        compiler_params=_cparams("parallel"),
        name="rmsnorm_rows",
    )(x, g)


def _lb_kernel(p_ref, o_ref):
    p = p_ref[...]
    e = jnp.exp(p - jnp.max(p, axis=0, keepdims=True))
    s = e / jnp.sum(e, axis=0, keepdims=True)
    acc = jnp.zeros_like(s[0:1])
    for l in range(DEPTH):
        if l > 0:
            acc = acc + s[l:l + 1]
        o_ref[l:l + 1, :] = acc


def _hgrn_lower_bounds(hgrn_lb):
    return pl.pallas_call(_lb_kernel, out_shape=jax.ShapeDtypeStruct(hgrn_lb.shape, F32),
                          name="hgrn_lower_bounds")(hgrn_lb)


def _rwkv_pre(pr, prev, mu, w0, w_up, a0, a_up, g_up, k_k, k_a):
    c = C_HEADS
    pm = pr + mu * (prev - pr)
    r = pm[:, :c]
    k0 = pm[:, c:2 * c]
    v = pm[:, 2 * c:3 * c]
    xwa = pm[:, 3 * c:3 * c + W_LORA + A_LORA]
    xg = pm[:, 3 * c + W_LORA + A_LORA:]
    log_decay = -DECAY_SCALE * jax.nn.sigmoid(w0 + _dot(jnp.tanh(xwa), w_up))
    a = jax.nn.sigmoid(a0 + _dot(xwa, a_up))
    g = _dot(jax.nn.sigmoid(xg), g_up)
    kk = k0 * k_k
    kk = kk * lax.rsqrt(jnp.maximum(_head_sum(kk * kk), 1e-24))
    k = k0 * (1.0 + (a - 1.0) * k_a)
    return r, log_decay, k, v, kk, kk * a, g


def _hgrn_pre(fpre, lb):
    sg = jax.nn.sigmoid(fpre)
    f = lb + (1.0 - lb) * sg
    kin = (1.0 - lb) * jax.nn.sigmoid(-fpre)
    return f, kin


def _lru_pre(xb, x1, x2, x3, conv_w, conv_b, wa, ba, wx, bx, lam):
    xc = conv_b + (((x3 * conv_w[0:1] + x2 * conv_w[1:2]) + x1 * conv_w[2:3]) + xb * conv_w[3:4])
    r = jax.nn.sigmoid(_dot(xc, wa) + ba)
    i = jax.nn.sigmoid(_dot(xc, wx) + bx)
    log_a = -LRU_C * r * _softplus(-lam)
    a = jnp.exp(log_a)
    one_minus_a2 = -jnp.tanh(log_a) * (jnp.exp(2.0 * log_a) + 1.0)
    u = xc * i * jnp.sqrt(jnp.maximum(one_minus_a2, 1e-12))
    return a, u


_N_PRE_PARAMS = 16


def _step_pre_kernel(*refs):
    pr_ref, ph_ref, pl_ref, p0_ref, c0_ref, c1_ref, c2_ref, h0_ref = refs[:8]
    (mu, w0, w_up, a0, a_up, g_up, k_k, k_a, lb, conv_w, conv_b, wa, ba, wx, bx, lam) = refs[8:8 + _N_PRE_PARAMS]
    (r_o, k_o, v_o, g_o, rT_o, wT_o, kT_o, vT_o, kkT_o, kaT_o,
     qT_o, fT_o, kinT_o, ivT_o, h_o) = refs[8 + _N_PRE_PARAMS:]
    r, lw, k, v, kk, ka, g = _rwkv_pre(pr_ref[...], p0_ref[...], mu[...], w0[...], w_up[...], a0[...],
                                       a_up[...], g_up[...], k_k[...], k_a[...])
    r_o[...], k_o[...], v_o[...], g_o[...] = r, k, v, g
    for o_ref, val in ((rT_o, r), (wT_o, jnp.exp(lw)), (kT_o, k), (vT_o, v), (kkT_o, kk), (kaT_o, ka)):
        o_ref[...] = val.T
    ph = ph_ref[...]
    f, kin = _hgrn_pre(ph[:, C_HEADS:2 * C_HEADS], lb[...])
    for o_ref, val in ((qT_o, ph[:, :C_HEADS]), (fT_o, f), (kinT_o, kin), (ivT_o, ph[:, 2 * C_HEADS:3 * C_HEADS])):
        o_ref[...] = val.T
    a, u = _lru_pre(pl_ref[...][:, :C_LRU], c2_ref[...], c1_ref[...], c0_ref[...], conv_w[...], conv_b[...],
                    wa[...], ba[...], wx[...], bx[...], lam[...])
    h_o[...] = a * h0_ref[...] + u


def _step_pre(pr, ph, plru, p0, conv0, h0, prm):
    B = pr.shape[0]
    shapes = [(B, C_HEADS)] * 4 + [(C_HEADS, B)] * 10 + [(B, C_LRU)]
    return pl.pallas_call(
        _step_pre_kernel,
        out_shape=[jax.ShapeDtypeStruct(s, F32) for s in shapes],
        compiler_params=pltpu.CompilerParams(vmem_limit_bytes=VMEM_LIMIT_BYTES),
        name="step_pre",
    )(pr, ph, plru, p0, conv0[:, 0], conv0[:, 1], conv0[:, 2], h0, *prm)


def _rwkv_step_kernel(r_ref, w_ref, k_ref, v_ref, kk_ref, ka_ref, s_ref, o_ref, sT_ref):
    kk, w, ka, k, r = kk_ref[...], w_ref[...], ka_ref[...], k_ref[...], r_ref[...]

    def value_row(vi, carry):
        S = s_ref[vi]
        sa = -jnp.sum(S * kk, axis=0, keepdims=True)
        S = S * w + sa * ka + v_ref[pl.ds(vi, 1), :] * k
        sT_ref[vi] = S
        o_ref[pl.ds(vi, 1), :] = jnp.sum(S * r, axis=0, keepdims=True)
        return carry

    lax.fori_loop(0, HEAD_DIM, value_row, 0, unroll=8)


def _hgrn_step_kernel(q_ref, f_ref, kin_ref, iv_ref, s_ref, o_ref, sT_ref):
    iv = iv_ref[...]

    def key_row(ki, acc):
        S = f_ref[pl.ds(ki, 1), :] * s_ref[ki] + kin_ref[pl.ds(ki, 1), :] * iv
        sT_ref[ki] = S
        return acc + S * q_ref[pl.ds(ki, 1), :]

    o_ref[...] = lax.fori_loop(0, HEAD_DIM, key_row, jnp.zeros_like(iv), unroll=8)


def _state_step(kernel, name, vecs, states, layer, new_states):
    B = states.shape[-1]
    n_vec = len(vecs)
    vblk = pl.BlockSpec((HEAD_DIM, B), lambda h: (h, 0))
    sblk = pl.BlockSpec((None, None, HEAD_DIM, HEAD_DIM, B), lambda h: (layer, h, 0, 0, 0))
    in_specs = [vblk] * n_vec + [sblk]
    operands = list(vecs) + [states]
    aliases = {}
    if new_states is not None:
        in_specs.append(pl.BlockSpec(memory_space=pl.ANY))
        operands.append(new_states)
        aliases = {n_vec + 1: 1}

    def body(*refs):
        kernel(*refs[:n_vec + 1], *refs[len(operands):])

    return pl.pallas_call(
        body,
        grid=(N_HEADS,),
        in_specs=in_specs,
        out_specs=[vblk, sblk],
        out_shape=[jax.ShapeDtypeStruct((C_HEADS, B), F32), jax.ShapeDtypeStruct(states.shape, F32)],
        input_output_aliases=aliases,
        compiler_params=_cparams("parallel"),
        name=name,
    )(*operands)


def _rwkv_post(out, r, k, v, g, r_k, ln_w, ln_b):
    inv_n = 1.0 / HEAD_DIM
    mean = _head_sum(out) * inv_n
    d = out - mean
    var = _head_sum(d * d) * inv_n
    gn = d * lax.rsqrt(var + GN_EPS) * ln_w + ln_b
    bonus = _head_sum(r * k * r_k) * v
    return (gn + bonus) * g


def _hgrn_post(o, gate, norm_g):
    o = o * lax.rsqrt(_head_sum(o * o) * (1.0 / HEAD_DIM) + NORM_EPS)
    return o * norm_g * jax.nn.silu(gate)


def _step_post_kernel(woT_ref, r_ref, k_ref, v_ref, g_ref, rk_ref, lnw_ref, lnb_ref,
                      hoT_ref, hg_ref, hng_ref, lh_ref, lg_ref, or_ref, oh_ref, ol_ref):
    or_ref[...] = _rwkv_post(woT_ref[...].T, r_ref[...], k_ref[...], v_ref[...], g_ref[...],
                             rk_ref[...], lnw_ref[...], lnb_ref[...])
    oh_ref[...] = _hgrn_post(hoT_ref[...].T, hg_ref[...], hng_ref[...])
    ol_ref[...] = lh_ref[...] * jax.nn.gelu(lg_ref[...])


def _step_post(woT, r, k, v, g, r_k, ln_w, ln_b, hoT, ph, hng, lh, plru):
    B = r.shape[0]
    full = lambda a: pl.BlockSpec(a.shape, lambda i: (0,) * a.ndim)
    return pl.pallas_call(
        _step_post_kernel,
        grid=(1,),
        in_specs=[full(woT), full(r), full(k), full(v), full(g), full(r_k), full(ln_w), full(ln_b),
                  full(hoT), pl.BlockSpec((B, C_HEADS), lambda i: (0, 3)), full(hng), full(lh),
                  pl.BlockSpec((B, C_LRU), lambda i: (0, 1))],
        out_specs=[pl.BlockSpec((B, c), lambda i: (0, 0)) for c in (C_HEADS, C_HEADS, C_LRU)],
        out_shape=[jax.ShapeDtypeStruct((B, c), F32) for c in (C_HEADS, C_HEADS, C_LRU)],
        compiler_params=_cparams("arbitrary"),
        name="step_post",
    )(woT, r, k, v, g, r_k, ln_w, ln_b, hoT, ph, hng, lh, plru)


CHUNK = 64
CHUNK_LOG2 = 6
SEQS = 4
GROUP_HEADS = 4
GROUP = GROUP_HEADS * HEAD_DIM
N_GROUPS = SEQS * N_HEADS // GROUP_HEADS

NN = ((1,), (0,))
NT = ((1,), (1,))


def _mm(a, b, dims, passes=1):
    assert passes == 1
    return lax.dot_general(a.astype(BF16), b.astype(BF16), (dims, ((), ())), preferred_element_type=F32)


def _iota2(shape):
    return lax.broadcasted_iota(jnp.int32, shape, 0), lax.broadcasted_iota(jnp.int32, shape, 1)


def _head_of(group, i):
    hw = group * GROUP_HEADS + i
    return hw // N_HEADS, hw % N_HEADS


def _group_state_load(s_ref, S_s, transpose):
    z = jnp.zeros((HEAD_DIM, HEAD_DIM), F32)
    for gi in range(N_GROUPS):
        rows = []
        for i in range(GROUP_HEADS):
            b, h = _head_of(gi, i)
            blk = s_ref[b, h].T if transpose else s_ref[b, h]
            rows.append(jnp.concatenate([blk if i2 == i else z for i2 in range(GROUP_HEADS)], axis=1))
        S_s[gi] = jnp.concatenate(rows, axis=0)


def _group_state_store(S_s, s_ref, transpose):
    for gi in range(N_GROUPS):
        Sg = S_s[gi]
        for i in range(GROUP_HEADS):
            b, h = _head_of(gi, i)
            blk = Sg[i * HEAD_DIM:(i + 1) * HEAD_DIM, i * HEAD_DIM:(i + 1) * HEAD_DIM]
            s_ref[b, h] = blk.T if transpose else blk


def _wide(x):
    return jnp.concatenate([x[b * CHUNK:(b + 1) * CHUNK] for b in range(SEQS)], axis=1)


def _tall(x):
    return jnp.concatenate([x[:, b * C_HEADS:(b + 1) * C_HEADS] for b in range(SEQS)], axis=0)


def _seq_cumsum(x):
    tr, tc = _iota2((SEQS * CHUNK, SEQS * CHUNK))
    tri = ((tr >= tc) & ((tr >> CHUNK_LOG2) == (tc >> CHUNK_LOG2))).astype(BF16)
    hi = x.astype(BF16)
    rest = x - hi.astype(F32)
    mid = rest.astype(BF16)
    lo = (rest - mid.astype(F32)).astype(BF16)
    dot = lambda part: jnp.dot(tri, part, preferred_element_type=F32)
    return dot(hi) + dot(mid) + dot(lo)


def _lane_head():
    return lax.broadcasted_iota(jnp.int32, (CHUNK, GROUP), 1) >> 6


def _by_head(x, lane_head):
    return jnp.concatenate([jnp.where(lane_head == i, x, 0.0) for i in range(GROUP_HEADS)], axis=0)


def _rep(x):
    return jnp.concatenate([x] * GROUP_HEADS, axis=0)


def _pair_masks():
    lane = lax.broadcasted_iota(jnp.int32, (CHUNK, GROUP), 1)
    return lane < 2 * HEAD_DIM, (lane & HEAD_DIM) == 0


def _split_rows(x, mask):
    return jnp.concatenate([jnp.where(mask, x, 0.0), jnp.where(mask, 0.0, x)], axis=0)


def _pick(x, lane_head):
    out = x[:CHUNK]
    for i in range(1, GROUP_HEADS):
        out = jnp.where(lane_head == i, x[i * CHUNK:(i + 1) * CHUNK], out)
    return out


def _unit_lower_inverse(Ls):
    r, c = _iota2(Ls[0].shape)
    eye = (r == c).astype(F32)
    diag_blk = (r >> 4) == (c >> 4)
    mm = lambda xs, ys: [_mm(x, y, NN, 1) for x, y in zip(xs, ys)]
    Ld = [jnp.where(diag_blk, L, 0.0) for L in Ls]
    N = [jnp.where(diag_blk, 0.0, L) for L in Ls]
    L2 = mm(Ld, Ld)
    L4 = mm(L2, L2)
    P1 = mm([eye - x for x in Ld], [eye + x for x in L2])
    L8 = mm(L4, L4)
    P2 = mm([eye + x for x in L4], [eye + x for x in L8])
    Dinv = mm(P1, P2)
    M = mm(Dinv, N)
    M2 = mm(M, M)
    X = mm([eye - x for x in M], [eye + x for x in M2])
    return mm(X, Dinv)


def _seq_project(x_ref, norm_g, w_ref):
    x = x_ref[...].reshape(-1, D_MODEL)
    return jnp.dot(_rmsnorm(x, norm_g[...]).astype(BF16), w_ref[...], preferred_element_type=F32)


def _rwkv_seq_kernel(x_ref, shift_ref, norm_g, w_ref, s0_ref,
                     mu, w0, w_up, a0, a_up, g_up, k_k, k_a, r_k, ln_w, ln_b,
                     o_ref, sT_ref, carry_s, S_s):
    j = pl.program_id(1)
    C = CHUNK

    @pl.when(j == 0)
    def _():
        p0 = jnp.dot(shift_ref[...].reshape(SEQS, D_MODEL).astype(BF16), w_ref[...], preferred_element_type=F32)
        carry_s[...] = p0.reshape(SEQS, 1, RWKV_COLS)
        _group_state_load(s0_ref, S_s, transpose=False)

    pr = _seq_project(x_ref, norm_g, w_ref)
    prow = lax.broadcasted_iota(jnp.int32, pr.shape, 0)
    prev = pltpu.roll(pr, 1, 0)
    for b in range(SEQS):
        prev = jnp.where(prow == b * C, carry_s[b], prev)
        carry_s[b] = pr[(b + 1) * C - 1:(b + 1) * C, :]

    r, lw, k, v, kk, kb, g = _rwkv_pre(pr, prev, mu[...], w0[...], w_up[...], a0[...], a_up[...],
                                       g_up[...], k_k[...], k_a[...])
    cum_w, lw_w = _wide(_seq_cumsum(lw)), _wide(lw)
    r_w, k_w, v_w, kk_w, kb_w = _wide(r), _wide(k), _wide(v), _wide(kk), _wide(kb)
    cum_last = cum_w[C - 1:C, :]
    e_inv = jnp.exp(-cum_w)
    e_rel = jnp.exp(cum_last - cum_w)
    e_last = jnp.exp(cum_last)
    Kg = kk_w * jnp.exp(cum_w - lw_w)
    Rg = r_w * jnp.exp(cum_w)
    Ki, Bi = k_w * e_inv, kb_w * e_inv
    Kt, Bt = k_w * e_rel, kb_w * e_rel

    lane_head = _lane_head()
    low_pair, even_head = _pair_masks()
    by_pair = lambda x: _split_rows(x, low_pair)
    by_parity = lambda x: _split_rows(x, even_head)
    gr, gc = _iota2((GROUP, GROUP))
    same_head = (gr >> 6) == (gc >> 6)
    strict_bd = same_head & ((gr & (HEAD_DIM - 1)) > (gc & (HEAD_DIM - 1)))
    ar, ac = _iota2((2 * C, 2 * C))
    strict = (ar & (C - 1)) > (ac & (C - 1))
    incl = (ar & (C - 1)) >= (ac & (C - 1))

    groups = range(N_GROUPS)
    sl = [slice(gi * GROUP, (gi + 1) * GROUP) for gi in groups]
    Akb = [jnp.where(strict_bd, _mm(_by_head(Kg[:, s], lane_head), _rep(Bi[:, s]), NT, 1), 0.0) for s in sl]
    P = [_mm(jnp.concatenate([by_pair(Kg[:, s]), by_pair(Rg[:, s])], axis=0),
             jnp.concatenate([by_parity(Ki[:, s]), by_parity(Bi[:, s])], axis=0), NT, 1) for s in sl]
    Tinv = _unit_lower_inverse(Akb)
    S = [S_s[gi] for gi in groups]
    H0 = [_mm(jnp.concatenate([Kg[:, s], Rg[:, s]], axis=0), S[gi], NT, 1) for gi, s in enumerate(sl)]
    Vp = [by_parity(v_w[:, s]) for s in sl]
    AkkV = [_mm(jnp.where(strict, P[gi][:2 * C, :2 * C], 0.0), Vp[gi], NN, 1) for gi in groups]
    rhs_u = [jnp.concatenate([H0[gi][:C] + AkkV[gi][:C]] * 2 + [H0[gi][:C] + AkkV[gi][C:]] * 2, axis=0)
             for gi in groups]
    U = [_pick(_mm(Tinv[gi], rhs_u[gi], NN, 1), lane_head) for gi in groups]
    O = []
    for gi in groups:
        Y = _mm(jnp.concatenate([jnp.where(incl, P[gi][2 * C:, :2 * C], 0.0),
                                 jnp.where(incl, -P[gi][2 * C:, 2 * C:], 0.0)], axis=1),
                jnp.concatenate([Vp[gi], by_parity(U[gi])], axis=0), NN, 1)
        O.append(H0[gi][C:] + jnp.where(low_pair, Y[:C], Y[C:]))
    for gi, s in enumerate(sl):
        Z = _mm(jnp.concatenate([v_w[:, s], U[gi]], axis=0).T,
                jnp.concatenate([Kt[:, s], -Bt[:, s]], axis=0), NN, 1)
        S_s[gi] = S[gi] * e_last[:, s] + jnp.where(same_head, Z, 0.0)

    res = _rwkv_post(_tall(jnp.concatenate(O, axis=1)), r, k, v, g, r_k[...], ln_w[...], ln_b[...])
    o_ref[...] = res.reshape(SEQS, C, C_HEADS)

    @pl.when(j == pl.num_programs(1) - 1)
    def _():
        _group_state_store(S_s, sT_ref, transpose=False)


def _hgrn_seq_kernel(x_ref, norm1_g, w_ref, s0_ref, lb, norm_g, o_ref, sT_ref, S_s):
    j = pl.program_id(1)
    C = CHUNK

    @pl.when(j == 0)
    def _():
        _group_state_load(s0_ref, S_s, transpose=True)

    ph = _seq_project(x_ref, norm1_g, w_ref)
    f, kin = _hgrn_pre(ph[:, C_HEADS:2 * C_HEADS], lb[...])
    cum = _wide(_seq_cumsum(jnp.log(f)))
    q_w, kin_w, iv_w = _wide(ph[:, :C_HEADS]), _wide(kin), _wide(ph[:, 2 * C_HEADS:3 * C_HEADS])
    cum_last = cum[C - 1:C, :]
    e_last = jnp.exp(cum_last)
    qd = q_w * jnp.exp(cum)
    kd = kin_w * jnp.exp(cum_last - cum)

    low_pair, even_head = _pair_masks()
    by_pair = lambda x: _split_rows(x, low_pair)
    by_parity = lambda x: _split_rows(x, even_head)
    ar, ac = _iota2((2 * C, 2 * C))
    t_pos, s_pos = ar & (C - 1), ac & (C - 1)
    differ = jnp.where(t_pos > s_pos, t_pos ^ s_pos, 0)
    gr, gc = _iota2((GROUP, GROUP))
    same_head = (gr >> 6) == (gc >> 6)
    groups = range(N_GROUPS)
    sl = [slice(gi * GROUP, (gi + 1) * GROUP) for gi in groups]

    att = [jnp.where(t_pos == s_pos, _mm(by_pair(q_w[:, s]), by_parity(kin_w[:, s]), NT, 1), 0.0) for s in sl]
    row = lax.broadcasted_iota(jnp.int32, cum.shape, 0)
    last = cum
    for l in range(CHUNK_LOG2):
        m = 1 << l
        in_low = (row & (2 * m - 1)) < m
        d = cum - jnp.where(in_low, last, pltpu.roll(last, m, 0))
        last = jnp.where(in_low, pltpu.roll(last, C - m, 0), last)
        e = jnp.exp(-jnp.abs(d))
        ql, kl = q_w * e, kin_w * e
        level = (differ >> l) == 1
        att = [jnp.where(level, _mm(by_pair(ql[:, s]), by_parity(kl[:, s]), NT, 1), att[gi])
               for gi, s in enumerate(sl)]

    outs = []
    for gi, s in enumerate(sl):
        S = S_s[gi]
        v = iv_w[:, s]
        intra = _mm(att[gi], by_parity(v), NN, 1)
        outs.append(jnp.where(low_pair, intra[:C], intra[C:]) + _mm(qd[:, s], S, NT, 1))
        Z = _mm(v.T, kd[:, s], NN, 1)
        S_s[gi] = S * e_last[:, s] + jnp.where(same_head, Z, 0.0)

    res = _hgrn_post(_tall(jnp.concatenate(outs, axis=1)), ph[:, 3 * C_HEADS:], norm_g[...])
    o_ref[...] = res.reshape(SEQS, C, C_HEADS)

    @pl.when(j == pl.num_programs(1) - 1)
    def _():
        _group_state_store(S_s, sT_ref, transpose=True)


def _lru_seq_kernel(x_ref, norm_g, w_ref, c0_ref, h0_ref, conv_w, conv_b, wa, ba, wx, bx, lam,
                    o_ref, hT_ref, cT_ref, xbuf_s, h_s, a_s, u_s, *, tm):
    j = pl.program_id(1)

    @pl.when(j == 0)
    def _():
        xbuf_s[5:8, :] = c0_ref[0]
        h_s[...] = h0_ref[0]

    blk = _seq_project(x_ref, norm_g, w_ref)
    xb = blk[:, :C_LRU]
    xbuf_s[8:8 + tm, :] = xb
    x1 = xbuf_s[7:7 + tm, :]
    x2 = xbuf_s[6:6 + tm, :]
    x3 = xbuf_s[5:5 + tm, :]
    a, u = _lru_pre(xb, x1, x2, x3, conv_w[...], conv_b[...], wa[...], ba[...], wx[...], bx[...], lam[...])
    xbuf_s[5:8, :] = xb[tm - 3:tm, :]

    row8 = lax.broadcasted_iota(jnp.int32, a.shape, 0) & 7
    for d in (1, 2, 4):
        ok = row8 >= d
        u = jnp.where(ok, a * pltpu.roll(u, d, 0) + u, u)
        a = jnp.where(ok, a * pltpu.roll(a, d, 0), a)
    a_s[...] = a
    u_s[...] = u

    def group(gi, h):
        i = pl.multiple_of(gi * 8, 8)
        ht = u_s[pl.ds(i, 8), :] + a_s[pl.ds(i, 8), :] * h
        o_ref[0, pl.ds(i, 8), :] = ht
        return ht[7:8, :]

    h = lax.fori_loop(0, tm // 8, group, h_s[...])
    h_s[...] = h
    o_ref[0] = o_ref[0] * jax.nn.gelu(blk[:, C_LRU:])

    @pl.when(j == pl.num_programs(1) - 1)
    def _():
        hT_ref[0] = h
        cT_ref[0] = xb[tm - (CONV_WIDTH - 1):tm, :]


def _full_spec(a):
    return pl.BlockSpec(a.shape, lambda b, j: (0,) * a.ndim)


def _seq_block(c):
    return pl.BlockSpec((SEQS, CHUNK, c), lambda b, j: (b, j, 0))


_STATE_BLOCK = pl.BlockSpec((SEQS, N_HEADS, HEAD_DIM, HEAD_DIM), lambda b, j: (b, 0, 0, 0))
_GROUP_STATE = pltpu.VMEM((N_GROUPS, GROUP, GROUP), F32)


def _rwkv_seq(x, shift, norm_g, w, s0, prm, B, T):
    o, sT = pl.pallas_call(
        _rwkv_seq_kernel,
        grid=(B // SEQS, T // CHUNK),
        in_specs=[_seq_block(D_MODEL), pl.BlockSpec((SEQS, 1, D_MODEL), lambda b, j: (b, 0, 0)),
                  _full_spec(norm_g), _full_spec(w), _STATE_BLOCK] + [_full_spec(a) for a in prm],
        out_specs=[_seq_block(C_HEADS), _STATE_BLOCK],
        out_shape=[jax.ShapeDtypeStruct((B, T, C_HEADS), F32), jax.ShapeDtypeStruct(s0.shape, F32)],
        scratch_shapes=[pltpu.VMEM((SEQS, 1, RWKV_COLS), F32), _GROUP_STATE],
        compiler_params=_cparams("parallel", "arbitrary"),
        name="rwkv_seq",
    )(x, shift.reshape(B, 1, D_MODEL), norm_g, w, s0, *prm)
    return o.reshape(B * T, C_HEADS), sT


def _hgrn_seq(x, norm1_g, w, s0, lb, norm_g, B, T):
    o, sT = pl.pallas_call(
        _hgrn_seq_kernel,
        grid=(B // SEQS, T // CHUNK),
        in_specs=[_seq_block(D_MODEL), _full_spec(norm1_g), _full_spec(w), _STATE_BLOCK,
                  _full_spec(lb), _full_spec(norm_g)],
        out_specs=[_seq_block(C_HEADS), _STATE_BLOCK],
        out_shape=[jax.ShapeDtypeStruct((B, T, C_HEADS), F32), jax.ShapeDtypeStruct(s0.shape, F32)],
        scratch_shapes=[_GROUP_STATE],
        compiler_params=_cparams("parallel", "arbitrary"),
        name="hgrn_seq",
    )(x, norm1_g, w, s0, lb, norm_g)
    return o.reshape(B * T, C_HEADS), sT


def _lru_seq(x, norm_g, w, conv0, h0, prm, B, T):
    tm = min(T, 256)
    seq = lambda c: pl.BlockSpec((1, tm, c), lambda b, j: (b, j, 0))
    hblk = pl.BlockSpec((1, 1, C_LRU), lambda b, j: (b, 0, 0))
    cblk = pl.BlockSpec((1, CONV_WIDTH - 1, C_LRU), lambda b, j: (b, 0, 0))
    o, hT, cT = pl.pallas_call(
        functools.partial(_lru_seq_kernel, tm=tm),
        grid=(B, T // tm),
        in_specs=[seq(D_MODEL), _full_spec(norm_g), _full_spec(w), cblk, hblk] + [_full_spec(a) for a in prm],
        out_specs=[seq(C_LRU), hblk, cblk],
        out_shape=[jax.ShapeDtypeStruct((B, T, C_LRU), F32), jax.ShapeDtypeStruct((B, 1, C_LRU), F32),
                   jax.ShapeDtypeStruct((B, CONV_WIDTH - 1, C_LRU), F32)],
        scratch_shapes=[pltpu.VMEM((tm + 8, C_LRU), F32), pltpu.VMEM((1, C_LRU), F32),
                        pltpu.VMEM((tm, C_LRU), F32), pltpu.VMEM((tm, C_LRU), F32)],
        compiler_params=_cparams("parallel", "arbitrary"),
        name="lru_seq",
    )(x, norm_g, w, conv0, h0.reshape(B, 1, C_LRU), *prm)
    return o.reshape(B * T, C_LRU), hT.reshape(B, C_LRU), cT


def _block_diag(w):
    out = jnp.zeros((C_LRU, C_LRU), w.dtype)
    n = C_LRU // LRU_BLOCKS
    for i in range(LRU_BLOCKS):
        out = out.at[i * n:(i + 1) * n, i * n:(i + 1) * n].set(w[i])
    return out


def _prepare_params(p):
    row = lambda a: a.reshape(1, -1)
    zpad = jnp.zeros((W_LORA, C_HEADS), F32)
    layers = []
    for l in range(DEPTH):
        pre = (row(p['mu_shift'][l]), row(p['rwkv_w0'][l]),
               jnp.concatenate([p['rwkv_w_up'][l], zpad], 0).astype(BF16),
               row(p['rwkv_a0'][l]),
               jnp.concatenate([zpad, p['rwkv_a_up'][l]], 0).astype(BF16),
               p['rwkv_g_up'][l].astype(BF16), row(p['rwkv_k_k'][l]), row(p['rwkv_k_a'][l]),
               None,
               p['lru_conv_w'][l], row(p['lru_conv_b'][l]),
               _block_diag(p['lru_wa'][l]).astype(BF16), row(p['lru_ba'][l]),
               _block_diag(p['lru_wx'][l]).astype(BF16), row(p['lru_bx'][l]), row(p['lru_lambda'][l]))
        w_in = p['w_in'][l]
        layers.append(dict(
            norm1_g=row(p['norm1_g'][l]), pre=pre,
            w_r=w_in[:, :RWKV_COLS].astype(BF16),
            w_h=w_in[:, RWKV_COLS:RWKV_COLS + HGRN_COLS].astype(BF16),
            w_l=w_in[:, RWKV_COLS + HGRN_COLS:].astype(BF16),
            r_k=row(p['rwkv_r_k'][l]), ln_w=row(p['rwkv_ln_w'][l]), ln_b=row(p['rwkv_ln_b'][l]),
            hgrn_norm_g=row(p['hgrn_norm_g'][l]),
            w_out=p['w_out'][l].astype(BF16), norm2_g=row(p['norm2_g'][l]),
            mlp_w1=p['mlp_w1'][l].astype(BF16), mlp_w2=p['mlp_w2'][l].astype(BF16)))
    return layers


def _trunk(x, wkv, shift, hgrn, lru, conv, layers, lb_all, final_g):
    B, T, _ = x.shape
    m = B * T
    assert T == 1 or (T % CHUNK == 0 and B % SEQS == 0), "sequence kernels tile (B, T) by (SEQS, CHUNK)"
    x = x.reshape(m, D_MODEL)
    n_wkv, n_shift, n_hgrn, n_lru, n_conv = [], [], [], [], []
    if T == 1:
        wkv_t, hgrn_t = jnp.transpose(wkv, (0, 2, 3, 4, 1)), jnp.transpose(hgrn, (0, 2, 3, 4, 1))
        s_wkv_t = s_hgrn_t = None
    for l, lp in enumerate(layers):
        n_shift.append(_rmsnorm_rows(x.reshape(B, T, D_MODEL)[:, -1], lp['norm1_g']))
        prm = list(lp['pre'])
        prm[8] = lb_all[l:l + 1]
        if T == 1:
            pr, ph, plru, p0 = _norm_proj(x, shift[l], lp['norm1_g'], lp['w_r'], lp['w_h'], lp['w_l'])
            (r, k, v, g, rT, wT, kT, vT, kkT, kaT, qT, fT, kinT, ivT, h_l) = _step_pre(
                pr, ph, plru, p0, conv[l], lru[l], prm)
            woT, s_wkv_t = _state_step(_rwkv_step_kernel, "rwkv_step", (rT, wT, kT, vT, kkT, kaT),
                                       wkv_t, l, s_wkv_t)
            hoT, s_hgrn_t = _state_step(_hgrn_step_kernel, "hgrn_step", (qT, fT, kinT, ivT),
                                        hgrn_t, l, s_hgrn_t)
            o_r, o_h, o_l = _step_post(woT, r, k, v, g, lp['r_k'], lp['ln_w'], lp['ln_b'],
                                       hoT, ph, lp['hgrn_norm_g'], h_l, plru)
            c_l = jnp.concatenate([conv[l][:, 1:], plru[:, None, :C_LRU]], axis=1)
        else:
            x3 = x.reshape(B, T, D_MODEL)
            o_r, S_r = _rwkv_seq(x3, shift[l], lp['norm1_g'], lp['w_r'], wkv[l],
                                 prm[:8] + [lp['r_k'], lp['ln_w'], lp['ln_b']], B, T)
            o_h, S_h = _hgrn_seq(x3, lp['norm1_g'], lp['w_h'], hgrn[l], prm[8], lp['hgrn_norm_g'], B, T)
            o_l, h_l, c_l = _lru_seq(x3, lp['norm1_g'], lp['w_l'], conv[l], lru[l], prm[9:], B, T)
            n_wkv.append(S_r)
            n_hgrn.append(S_h)
        x = _out_mlp(x, o_r, o_h, o_l, lp['w_out'], lp['norm2_g'], lp['mlp_w1'], lp['mlp_w2'])
        n_lru.append(h_l)
        n_conv.append(c_l)
    y = _rmsnorm_rows(x, final_g).reshape(B, T, D_MODEL)
    if T == 1:
        s_wkv, s_hgrn = jnp.transpose(s_wkv_t, (0, 4, 1, 2, 3)), jnp.transpose(s_hgrn_t, (0, 4, 1, 2, 3))
    else:
        s_wkv, s_hgrn = jnp.stack(n_wkv), jnp.stack(n_hgrn)
    return (y, s_wkv, jnp.stack(n_shift), s_hgrn, jnp.stack(n_lru), jnp.stack(n_conv))


def kernel(x_prompt, x_sample, state_wkv, state_shift, state_hgrn, state_lru, state_conv, norm1_g, w_in, mu_shift, rwkv_w0, rwkv_w_up, rwkv_a0, rwkv_a_up, rwkv_g_up, rwkv_k_k, rwkv_k_a, rwkv_r_k, rwkv_ln_w, rwkv_ln_b, hgrn_lb, hgrn_norm_g, lru_conv_w, lru_conv_b, lru_wa, lru_ba, lru_wx, lru_bx, lru_lambda, w_out, norm2_g, mlp_w1, mlp_w2, final_g):
    prm = dict(norm1_g=norm1_g, w_in=w_in, mu_shift=mu_shift, rwkv_w0=rwkv_w0, rwkv_w_up=rwkv_w_up,
               rwkv_a0=rwkv_a0, rwkv_a_up=rwkv_a_up, rwkv_g_up=rwkv_g_up, rwkv_k_k=rwkv_k_k,
               rwkv_k_a=rwkv_k_a, rwkv_r_k=rwkv_r_k.reshape(DEPTH, C_HEADS), rwkv_ln_w=rwkv_ln_w,
               rwkv_ln_b=rwkv_ln_b, hgrn_norm_g=hgrn_norm_g, lru_conv_w=lru_conv_w,
               lru_conv_b=lru_conv_b, lru_wa=lru_wa, lru_ba=lru_ba, lru_wx=lru_wx, lru_bx=lru_bx,
               lru_lambda=lru_lambda, w_out=w_out, norm2_g=norm2_g, mlp_w1=mlp_w1, mlp_w2=mlp_w2)
    layers = _prepare_params(prm)
    lb_all = _hgrn_lower_bounds(hgrn_lb)
    fg = final_g.reshape(1, D_MODEL)
    Bp = x_prompt.shape[0]
    dt = x_prompt.dtype
    z_wkv = jnp.zeros((DEPTH, Bp, N_HEADS, HEAD_DIM, HEAD_DIM), dt)
    z_shift = jnp.zeros((DEPTH, Bp, D_MODEL), dt)
    z_hgrn = jnp.zeros((DEPTH, Bp, N_HEADS, HEAD_DIM, HEAD_DIM), dt)
    z_lru = jnp.zeros((DEPTH, Bp, C_LRU), dt)
    z_conv = jnp.zeros((DEPTH, Bp, CONV_WIDTH - 1, C_LRU), dt)
    y_p, p_wkv, p_shift, p_hgrn, p_lru, p_conv = _trunk(x_prompt, z_wkv, z_shift, z_hgrn, z_lru, z_conv,
                                                         layers, lb_all, fg)
    y_s, s_wkv, s_shift, s_hgrn, s_lru, s_conv = _trunk(x_sample, state_wkv, state_shift, state_hgrn,
                                                         state_lru, state_conv, layers, lb_all, fg)
    return (y_p, y_s, p_wkv, p_shift, p_hgrn, p_lru, p_conv, s_wkv, s_shift, s_hgrn, s_lru, s_conv)
```

```python
import functools
import math

import jax
import jax.numpy as jnp
from jax import lax
from jax.experimental import pallas as pl
from jax.experimental.pallas import tpu as pltpu

F32 = jnp.float32
BF16 = jnp.bfloat16

D_MODEL = 1024
DEPTH = 4
HEAD_DIM = 64
N_HEADS = 6
C_HEADS = N_HEADS * HEAD_DIM
C_LRU = 256
LRU_BLOCKS = 4
CONV_WIDTH = 4
LRU_C = 8.0
W_LORA = 64
A_LORA = 64
G_LORA = 128
RWKV_COLS = 3 * C_HEADS + W_LORA + A_LORA + G_LORA
HGRN_COLS = 4 * C_HEADS
LRU_COLS = 2 * C_LRU
C_IN = RWKV_COLS + HGRN_COLS + LRU_COLS
D_FF = 4 * D_MODEL
NORM_EPS = 1e-6
GN_EPS = 64e-5
DECAY_SCALE = math.exp(-0.5)

LANES = 128
VMEM_LIMIT_BYTES = 48 * 1024 * 1024


def _cparams(*sem):
    return pltpu.CompilerParams(dimension_semantics=sem, vmem_limit_bytes=VMEM_LIMIT_BYTES)


def _dot(a, b):
    return jnp.dot(a.astype(BF16), b.astype(BF16), preferred_element_type=F32)


def _rmsnorm(x, g):
    return x * lax.rsqrt(jnp.mean(x * x, axis=-1, keepdims=True) + NORM_EPS) * g


def _softplus(x):
    return jnp.maximum(x, 0.0) + jnp.log1p(jnp.exp(-jnp.abs(x)))


def _head_sum(x):
    m = x.shape[0]
    lo = lax.broadcasted_iota(jnp.int32, (m, LANES), 1) < HEAD_DIM
    outs = []
    for p in range(x.shape[1] // LANES):
        xp = x[:, p * LANES:(p + 1) * LANES]
        s_lo = jnp.sum(jnp.where(lo, xp, 0.0), axis=1, keepdims=True)
        s_hi = jnp.sum(jnp.where(lo, 0.0, xp), axis=1, keepdims=True)
        outs.append(jnp.where(lo, s_lo, s_hi))
    return jnp.concatenate(outs, axis=1)


def _norm_proj_kernel(x_ref, shift_ref, g_ref, wr_ref, wh_ref, wl_ref, pr_ref, ph_ref, pl_ref, p0_ref):
    xn = _rmsnorm(x_ref[...], g_ref[...]).astype(BF16)
    pr_ref[...] = jnp.dot(xn, wr_ref[...], preferred_element_type=F32)
    ph_ref[...] = jnp.dot(xn, wh_ref[...], preferred_element_type=F32)
    pl_ref[...] = jnp.dot(xn, wl_ref[...], preferred_element_type=F32)
    p0_ref[...] = jnp.dot(shift_ref[...].astype(BF16), wr_ref[...], preferred_element_type=F32)


def _norm_proj(x, shift, g, w_r, w_h, w_l):
    m = x.shape[0]
    return pl.pallas_call(
        _norm_proj_kernel,
        out_shape=[jax.ShapeDtypeStruct((m, c), F32) for c in (RWKV_COLS, HGRN_COLS, LRU_COLS, RWKV_COLS)],
        compiler_params=pltpu.CompilerParams(vmem_limit_bytes=VMEM_LIMIT_BYTES),
        name="norm_proj",
    )(x, shift, g, w_r, w_h, w_l)


def _out_mlp_kernel(x_ref, or_ref, oh_ref, ol_ref, wo_ref, g2_ref, w1_ref, w2_ref, o_ref, x1_s, xn_s, acc_s):
    j = pl.program_id(1)

    @pl.when(j == 0)
    def _():
        mix = jnp.concatenate([or_ref[...], oh_ref[...], ol_ref[...]], axis=1)
        x1 = x_ref[...] + jnp.dot(mix.astype(BF16), wo_ref[...], preferred_element_type=F32)
        x1_s[...] = x1
        xn_s[...] = _rmsnorm(x1, g2_ref[...]).astype(BF16)
        acc_s[...] = jnp.zeros_like(acc_s)

    h = jnp.dot(xn_s[...], w1_ref[...], preferred_element_type=F32)
    h = jnp.square(jnp.maximum(h, 0.0))
    acc_s[...] += jnp.dot(h.astype(BF16), w2_ref[...], preferred_element_type=F32)

    @pl.when(j == pl.num_programs(1) - 1)
    def _():
        o_ref[...] = x1_s[...] + acc_s[...]


def _out_mlp(x, o_r, o_h, o_l, wo, g2, w1, w2):
    m = x.shape[0]
    tm = min(m, 512)
    tf = 2048
    row = pl.BlockSpec((tm, D_MODEL), lambda i, j: (i, 0))
    part = lambda c: pl.BlockSpec((tm, c), lambda i, j: (i, 0))
    return pl.pallas_call(
        _out_mlp_kernel,
        grid=(m // tm, D_FF // tf),
        in_specs=[row, part(C_HEADS), part(C_HEADS), part(C_LRU),
                  pl.BlockSpec((D_MODEL, D_MODEL), lambda i, j: (0, 0)),
                  pl.BlockSpec((1, D_MODEL), lambda i, j: (0, 0)),
                  pl.BlockSpec((D_MODEL, tf), lambda i, j: (0, j)),
                  pl.BlockSpec((tf, D_MODEL), lambda i, j: (j, 0))],
        out_specs=row,
        out_shape=jax.ShapeDtypeStruct((m, D_MODEL), F32),
        scratch_shapes=[pltpu.VMEM((tm, D_MODEL), F32), pltpu.VMEM((tm, D_MODEL), BF16),
                        pltpu.VMEM((tm, D_MODEL), F32)],
        compiler_params=_cparams("parallel", "arbitrary"),
        name="out_mlp",
    )(x, o_r, o_h, o_l, wo, g2, w1, w2)


def _rmsnorm_rows_kernel(x_ref, g_ref, o_ref):
    o_ref[...] = _rmsnorm(x_ref[...], g_ref[...])


def _rmsnorm_rows(x, g):
    m = x.shape[0]
    tm = min(m, 512)
    row = pl.BlockSpec((tm, D_MODEL), lambda i: (i, 0))
    return pl.pallas_call(
        _rmsnorm_rows_kernel,
        grid=(m // tm,),
        in_specs=[row, pl.BlockSpec((1, D_MODEL), lambda i: (0, 0))],
        out_specs=row,
        out_shape=jax.ShapeDtypeStruct((m, D_MODEL), F32),
        compiler_params=_cparams("parallel"),
        name="rmsnorm_rows",
    )(x, g)


def _lb_kernel(p_ref, o_ref):
    p = p_ref[...]
    e = jnp.exp(p - jnp.max(p, axis=0, keepdims=True))
    s = e / jnp.sum(e, axis=0, keepdims=True)
    acc = jnp.zeros_like(s[0:1])
    for l in range(DEPTH):
        if l > 0:
            acc = acc + s[l:l + 1]
        o_ref[l:l + 1, :] = acc


def _hgrn_lower_bounds(hgrn_lb):
    return pl.pallas_call(_lb_kernel, out_shape=jax.ShapeDtypeStruct(hgrn_lb.shape, F32),
                          name="hgrn_lower_bounds")(hgrn_lb)


def _rwkv_pre(pr, prev, mu, w0, w_up, a0, a_up, g_up, k_k, k_a):
    c = C_HEADS
    pm = pr + mu * (prev - pr)
    r = pm[:, :c]
    k0 = pm[:, c:2 * c]
    v = pm[:, 2 * c:3 * c]
    xwa = pm[:, 3 * c:3 * c + W_LORA + A_LORA]
    xg = pm[:, 3 * c + W_LORA + A_LORA:]
    log_decay = -DECAY_SCALE * jax.nn.sigmoid(w0 + _dot(jnp.tanh(xwa), w_up))
    a = jax.nn.sigmoid(a0 + _dot(xwa, a_up))
    g = _dot(jax.nn.sigmoid(xg), g_up)
    kk = k0 * k_k
    kk = kk * lax.rsqrt(jnp.maximum(_head_sum(kk * kk), 1e-24))
    k = k0 * (1.0 + (a - 1.0) * k_a)
    return r, log_decay, k, v, kk, kk * a, g


def _hgrn_pre(fpre, lb):
    sg = jax.nn.sigmoid(fpre)
    f = lb + (1.0 - lb) * sg
    kin = (1.0 - lb) * jax.nn.sigmoid(-fpre)
    return f, kin


def _lru_pre(xb, x1, x2, x3, conv_w, conv_b, wa, ba, wx, bx, lam):
    xc = conv_b + (((x3 * conv_w[0:1] + x2 * conv_w[1:2]) + x1 * conv_w[2:3]) + xb * conv_w[3:4])
    r = jax.nn.sigmoid(_dot(xc, wa) + ba)
    i = jax.nn.sigmoid(_dot(xc, wx) + bx)
    log_a = -LRU_C * r * _softplus(-lam)
    a = jnp.exp(log_a)
    one_minus_a2 = -jnp.tanh(log_a) * (jnp.exp(2.0 * log_a) + 1.0)
    u = xc * i * jnp.sqrt(jnp.maximum(one_minus_a2, 1e-12))
    return a, u


_N_PRE_PARAMS = 16


def _step_pre_kernel(*refs):
    pr_ref, ph_ref, pl_ref, p0_ref, c0_ref, c1_ref, c2_ref, h0_ref = refs[:8]
    (mu, w0, w_up, a0, a_up, g_up, k_k, k_a, lb, conv_w, conv_b, wa, ba, wx, bx, lam) = refs[8:8 + _N_PRE_PARAMS]
    (r_o, k_o, v_o, g_o, rT_o, wT_o, kT_o, vT_o, kkT_o, kaT_o,
     qT_o, fT_o, kinT_o, ivT_o, h_o) = refs[8 + _N_PRE_PARAMS:]
    r, lw, k, v, kk, ka, g = _rwkv_pre(pr_ref[...], p0_ref[...], mu[...], w0[...], w_up[...], a0[...],
                                       a_up[...], g_up[...], k_k[...], k_a[...])
    r_o[...], k_o[...], v_o[...], g_o[...] = r, k, v, g
    for o_ref, val in ((rT_o, r), (wT_o, jnp.exp(lw)), (kT_o, k), (vT_o, v), (kkT_o, kk), (kaT_o, ka)):
        o_ref[...] = val.T
    ph = ph_ref[...]
    f, kin = _hgrn_pre(ph[:, C_HEADS:2 * C_HEADS], lb[...])
    for o_ref, val in ((qT_o, ph[:, :C_HEADS]), (fT_o, f), (kinT_o, kin), (ivT_o, ph[:, 2 * C_HEADS:3 * C_HEADS])):
        o_ref[...] = val.T
    a, u = _lru_pre(pl_ref[...][:, :C_LRU], c2_ref[...], c1_ref[...], c0_ref[...], conv_w[...], conv_b[...],
                    wa[...], ba[...], wx[...], bx[...], lam[...])
    h_o[...] = a * h0_ref[...] + u


def _step_pre(pr, ph, plru, p0, conv0, h0, prm):
    B = pr.shape[0]
    shapes = [(B, C_HEADS)] * 4 + [(C_HEADS, B)] * 10 + [(B, C_LRU)]
    return pl.pallas_call(
        _step_pre_kernel,
        out_shape=[jax.ShapeDtypeStruct(s, F32) for s in shapes],
        compiler_params=pltpu.CompilerParams(vmem_limit_bytes=VMEM_LIMIT_BYTES),
        name="step_pre",
    )(pr, ph, plru, p0, conv0[:, 0], conv0[:, 1], conv0[:, 2], h0, *prm)


def _rwkv_step_kernel(r_ref, w_ref, k_ref, v_ref, kk_ref, ka_ref, s_ref, o_ref, sT_ref):
    kk, w, ka, k, r = kk_ref[...], w_ref[...], ka_ref[...], k_ref[...], r_ref[...]

    def value_row(vi, carry):
        S = s_ref[vi]
        sa = -jnp.sum(S * kk, axis=0, keepdims=True)
        S = S * w + sa * ka + v_ref[pl.ds(vi, 1), :] * k
        sT_ref[vi] = S
        o_ref[pl.ds(vi, 1), :] = jnp.sum(S * r, axis=0, keepdims=True)
        return carry

    lax.fori_loop(0, HEAD_DIM, value_row, 0, unroll=8)


def _hgrn_step_kernel(q_ref, f_ref, kin_ref, iv_ref, s_ref, o_ref, sT_ref):
    iv = iv_ref[...]

    def key_row(ki, acc):
        S = f_ref[pl.ds(ki, 1), :] * s_ref[ki] + kin_ref[pl.ds(ki, 1), :] * iv
        sT_ref[ki] = S
        return acc + S * q_ref[pl.ds(ki, 1), :]

    o_ref[...] = lax.fori_loop(0, HEAD_DIM, key_row, jnp.zeros_like(iv), unroll=8)


def _state_step(kernel, name, vecs, states, layer, new_states):
    B = states.shape[-1]
    n_vec = len(vecs)
    vblk = pl.BlockSpec((HEAD_DIM, B), lambda h: (h, 0))
    sblk = pl.BlockSpec((None, None, HEAD_DIM, HEAD_DIM, B), lambda h: (layer, h, 0, 0, 0))

    def body(*refs):
        kernel(*refs[:n_vec + 1], *refs[n_vec + 2:])

    return pl.pallas_call(
        body,
        grid=(N_HEADS,),
        in_specs=[vblk] * n_vec + [sblk, pl.BlockSpec(memory_space=pl.ANY)],
        out_specs=[vblk, sblk],
        out_shape=[jax.ShapeDtypeStruct((C_HEADS, B), F32), jax.ShapeDtypeStruct(states.shape, F32)],
        input_output_aliases={n_vec + 1: 1},
        compiler_params=_cparams("parallel"),
        name=name,
    )(*vecs, states, new_states)


def _rwkv_post(out, r, k, v, g, r_k, ln_w, ln_b):
    inv_n = 1.0 / HEAD_DIM
    mean = _head_sum(out) * inv_n
    d = out - mean
    var = _head_sum(d * d) * inv_n
    gn = d * lax.rsqrt(var + GN_EPS) * ln_w + ln_b
    bonus = _head_sum(r * k * r_k) * v
    return (gn + bonus) * g


def _hgrn_post(o, gate, norm_g):
    o = o * lax.rsqrt(_head_sum(o * o) * (1.0 / HEAD_DIM) + NORM_EPS)
    return o * norm_g * jax.nn.silu(gate)


def _step_post_kernel(woT_ref, r_ref, k_ref, v_ref, g_ref, rk_ref, lnw_ref, lnb_ref,
                      hoT_ref, hg_ref, hng_ref, lh_ref, lg_ref, or_ref, oh_ref, ol_ref):
    or_ref[...] = _rwkv_post(woT_ref[...].T, r_ref[...], k_ref[...], v_ref[...], g_ref[...],
                             rk_ref[...], lnw_ref[...], lnb_ref[...])
    oh_ref[...] = _hgrn_post(hoT_ref[...].T, hg_ref[...], hng_ref[...])
    ol_ref[...] = lh_ref[...] * jax.nn.gelu(lg_ref[...])


def _step_post(woT, r, k, v, g, r_k, ln_w, ln_b, hoT, ph, hng, lh, plru):
    B = r.shape[0]
    full = lambda a: pl.BlockSpec(a.shape, lambda i: (0,) * a.ndim)
    return pl.pallas_call(
        _step_post_kernel,
        grid=(1,),
        in_specs=[full(woT), full(r), full(k), full(v), full(g), full(r_k), full(ln_w), full(ln_b),
                  full(hoT), pl.BlockSpec((B, C_HEADS), lambda i: (0, 3)), full(hng), full(lh),
                  pl.BlockSpec((B, C_LRU), lambda i: (0, 1))],
        out_specs=[pl.BlockSpec((B, c), lambda i: (0, 0)) for c in (C_HEADS, C_HEADS, C_LRU)],
        out_shape=[jax.ShapeDtypeStruct((B, c), F32) for c in (C_HEADS, C_HEADS, C_LRU)],
        compiler_params=_cparams("arbitrary"),
        name="step_post",
    )(woT, r, k, v, g, r_k, ln_w, ln_b, hoT, ph, hng, lh, plru)


CHUNK = 64
CHUNK_LOG2 = 6
SEQS = 4
GROUP_HEADS = 4
GROUP = GROUP_HEADS * HEAD_DIM
N_GROUPS = SEQS * N_HEADS // GROUP_HEADS

NN = ((1,), (0,))
NT = ((1,), (1,))


def _mm(a, b, dims, passes=1):
    assert passes == 1
    return lax.dot_general(a.astype(BF16), b.astype(BF16), (dims, ((), ())), preferred_element_type=F32)


def _iota2(shape):
    return lax.broadcasted_iota(jnp.int32, shape, 0), lax.broadcasted_iota(jnp.int32, shape, 1)


def _head_of(group, i):
    hw = group * GROUP_HEADS + i
    return hw // N_HEADS, hw % N_HEADS


def _group_state_load(s_ref, S_s, transpose):
    z = jnp.zeros((HEAD_DIM, HEAD_DIM), F32)
    for gi in range(N_GROUPS):
        rows = []
        for i in range(GROUP_HEADS):
            b, h = _head_of(gi, i)
            blk = s_ref[b, h].T if transpose else s_ref[b, h]
            rows.append(jnp.concatenate([blk if i2 == i else z for i2 in range(GROUP_HEADS)], axis=1))
        S_s[gi] = jnp.concatenate(rows, axis=0)


def _group_state_store(S_s, s_ref, transpose):
    for gi in range(N_GROUPS):
        Sg = S_s[gi]
        for i in range(GROUP_HEADS):
            b, h = _head_of(gi, i)
            blk = Sg[i * HEAD_DIM:(i + 1) * HEAD_DIM, i * HEAD_DIM:(i + 1) * HEAD_DIM]
            s_ref[b, h] = blk.T if transpose else blk


def _wide(x):
    return jnp.concatenate([x[b * CHUNK:(b + 1) * CHUNK] for b in range(SEQS)], axis=1)


def _tall(x):
    return jnp.concatenate([x[:, b * C_HEADS:(b + 1) * C_HEADS] for b in range(SEQS)], axis=0)


def _seq_cumsum(x):
    tr, tc = _iota2((SEQS * CHUNK, SEQS * CHUNK))
    tri = ((tr >= tc) & ((tr >> CHUNK_LOG2) == (tc >> CHUNK_LOG2))).astype(BF16)
    hi = x.astype(BF16)
    rest = x - hi.astype(F32)
    mid = rest.astype(BF16)
    lo = (rest - mid.astype(F32)).astype(BF16)
    dot = lambda part: jnp.dot(tri, part, preferred_element_type=F32)
    return dot(hi) + dot(mid) + dot(lo)


def _lane_head():
    return lax.broadcasted_iota(jnp.int32, (CHUNK, GROUP), 1) >> 6


def _pair_masks():
    lane = lax.broadcasted_iota(jnp.int32, (CHUNK, GROUP), 1)
    return lane < 2 * HEAD_DIM, (lane & HEAD_DIM) == 0


def _split_rows(x, mask):
    return jnp.concatenate([jnp.where(mask, x, 0.0), jnp.where(mask, 0.0, x)], axis=0)


def _pick(x, lane_head):
    out = x[:CHUNK]
    for i in range(1, GROUP_HEADS):
        out = jnp.where(lane_head == i, x[i * CHUNK:(i + 1) * CHUNK], out)
    return out


def _unit_lower_inverse(Ls):
    r, c = _iota2(Ls[0].shape)
    eye = (r == c).astype(F32)
    diag_blk = (r >> 4) == (c >> 4)
    mm = lambda xs, ys: [_mm(x, y, NN, 1) for x, y in zip(xs, ys)]
    Ld = [jnp.where(diag_blk, L, 0.0) for L in Ls]
    N = [jnp.where(diag_blk, 0.0, L) for L in Ls]
    L2 = mm(Ld, Ld)
    L4 = mm(L2, L2)
    P1 = mm([eye - x for x in Ld], [eye + x for x in L2])
    L8 = mm(L4, L4)
    P2 = mm([eye + x for x in L4], [eye + x for x in L8])
    Dinv = mm(P1, P2)
    M = mm(Dinv, N)
    M2 = mm(M, M)
    X = mm([eye - x for x in M], [eye + x for x in M2])
    return mm(X, Dinv)


def _seq_project(x_ref, norm_g, w_ref):
    x = x_ref[...].reshape(-1, D_MODEL)
    return jnp.dot(_rmsnorm(x, norm_g[...]).astype(BF16), w_ref[...], preferred_element_type=F32)


def _rwkv_seq_kernel(x_ref, shift_ref, norm_g, w_ref, s0_ref,
                     mu, w0, w_up, a0, a_up, g_up, k_k, k_a, r_k, ln_w, ln_b,
                     o_ref, sT_ref, carry_s, S_s):
    j = pl.program_id(1)
    C = CHUNK

    @pl.when(j == 0)
    def _():
        p0 = jnp.dot(shift_ref[...].reshape(SEQS, D_MODEL).astype(BF16), w_ref[...], preferred_element_type=F32)
        carry_s[...] = p0.reshape(SEQS, 1, RWKV_COLS)
        _group_state_load(s0_ref, S_s, transpose=False)

    pr = _seq_project(x_ref, norm_g, w_ref)
    prow = lax.broadcasted_iota(jnp.int32, pr.shape, 0)
    prev = pltpu.roll(pr, 1, 0)
    for b in range(SEQS):
        prev = jnp.where(prow == b * C, carry_s[b], prev)
        carry_s[b] = pr[(b + 1) * C - 1:(b + 1) * C, :]

    r, lw, k, v, kk, kb, g = _rwkv_pre(pr, prev, mu[...], w0[...], w_up[...], a0[...], a_up[...],
                                       g_up[...], k_k[...], k_a[...])
    cum_w, lw_w = _wide(_seq_cumsum(lw)), _wide(lw)
    r_w, k_w, v_w, kk_w, kb_w = _wide(r), _wide(k), _wide(v), _wide(kk), _wide(kb)
    cum_last = cum_w[C - 1:C, :]
    e_inv = jnp.exp(-cum_w)
    e_rel = jnp.exp(cum_last - cum_w)
    e_last = jnp.exp(cum_last)
    Kg = kk_w * jnp.exp(cum_w - lw_w)
    Rg = r_w * jnp.exp(cum_w)
    Ki, Bi = k_w * e_inv, kb_w * e_inv
    Kt, Bt = k_w * e_rel, kb_w * e_rel

    lane_head = _lane_head()
    low_pair, even_head = _pair_masks()
    by_pair = lambda x: _split_rows(x, low_pair)
    by_parity = lambda x: _split_rows(x, even_head)
    gr, gc = _iota2((GROUP, GROUP))
    same_head = (gr >> 6) == (gc >> 6)
    ar, ac = _iota2((2 * C, 2 * C))
    strict = (ar & (C - 1)) > (ac & (C - 1))
    incl = (ar & (C - 1)) >= (ac & (C - 1))
    low_half = lax.broadcasted_iota(jnp.int32, (C, 2 * C), 1) < HEAD_DIM

    groups = range(N_GROUPS)
    sl = [slice(gi * GROUP, (gi + 1) * GROUP) for gi in groups]
    P = [_mm(jnp.concatenate([by_pair(Kg[:, s]), by_pair(Rg[:, s])], axis=0),
             jnp.concatenate([by_parity(Ki[:, s]), by_parity(Bi[:, s])], axis=0), NT, 1) for s in sl]
    z2 = jnp.zeros((2 * C, 2 * C), F32)
    Akb = []
    for gi in groups:
        a2 = jnp.where(strict, P[gi][:2 * C, 2 * C:], 0.0)
        halves = [jnp.concatenate([jnp.where(low_half, a2[p * C:(p + 1) * C], 0.0),
                                   jnp.where(low_half, 0.0, a2[p * C:(p + 1) * C])], axis=0) for p in range(2)]
        Akb.append(jnp.concatenate([jnp.concatenate([halves[0], z2], axis=1),
                                    jnp.concatenate([z2, halves[1]], axis=1)], axis=0))
    Tinv = _unit_lower_inverse(Akb)
    S = [S_s[gi] for gi in groups]
    H0 = [_mm(jnp.concatenate([Kg[:, s], Rg[:, s]], axis=0), S[gi], NT, 1) for gi, s in enumerate(sl)]
    Vp = [by_parity(v_w[:, s]) for s in sl]
    AkkV = [_mm(jnp.where(strict, P[gi][:2 * C, :2 * C], 0.0), Vp[gi], NN, 1) for gi in groups]
    rhs_u = [jnp.concatenate([H0[gi][:C] + AkkV[gi][:C]] * 2 + [H0[gi][:C] + AkkV[gi][C:]] * 2, axis=0)
             for gi in groups]
    U = [_pick(_mm(Tinv[gi], rhs_u[gi], NN, 1), lane_head) for gi in groups]
    O = []
    for gi in groups:
        Y = _mm(jnp.concatenate([jnp.where(incl, P[gi][2 * C:, :2 * C], 0.0),
                                 jnp.where(incl, -P[gi][2 * C:, 2 * C:], 0.0)], axis=1),
                jnp.concatenate([Vp[gi], by_parity(U[gi])], axis=0), NN, 1)
        O.append(H0[gi][C:] + jnp.where(low_pair, Y[:C], Y[C:]))
    for gi, s in enumerate(sl):
        Z = _mm(jnp.concatenate([v_w[:, s], U[gi]], axis=0).T,
                jnp.concatenate([Kt[:, s], -Bt[:, s]], axis=0), NN, 1)
        S_s[gi] = S[gi] * e_last[:, s] + jnp.where(same_head, Z, 0.0)

    res = _rwkv_post(_tall(jnp.concatenate(O, axis=1)), r, k, v, g, r_k[...], ln_w[...], ln_b[...])
    o_ref[...] = res.reshape(SEQS, C, C_HEADS)

    @pl.when(j == pl.num_programs(1) - 1)
    def _():
        _group_state_store(S_s, sT_ref, transpose=False)


def _hgrn_seq_kernel(x_ref, norm1_g, w_ref, s0_ref, lb, norm_g, o_ref, sT_ref, S_s):
    j = pl.program_id(1)
    C = CHUNK

    @pl.when(j == 0)
    def _():
        _group_state_load(s0_ref, S_s, transpose=True)

    ph = _seq_project(x_ref, norm1_g, w_ref)
    f, kin = _hgrn_pre(ph[:, C_HEADS:2 * C_HEADS], lb[...])
    cum = _wide(_seq_cumsum(jnp.log(f)))
    q_w, kin_w, iv_w = _wide(ph[:, :C_HEADS]), _wide(kin), _wide(ph[:, 2 * C_HEADS:3 * C_HEADS])
    cum_last = cum[C - 1:C, :]
    e_last = jnp.exp(cum_last)
    qd = q_w * jnp.exp(cum)
    kd = kin_w * jnp.exp(cum_last - cum)

    low_pair, even_head = _pair_masks()
    by_pair = lambda x: _split_rows(x, low_pair)
    by_parity = lambda x: _split_rows(x, even_head)
    ar, ac = _iota2((2 * C, 2 * C))
    t_pos, s_pos = ar & (C - 1), ac & (C - 1)
    differ = jnp.where(t_pos > s_pos, t_pos ^ s_pos, 0)
    gr, gc = _iota2((GROUP, GROUP))
    same_head = (gr >> 6) == (gc >> 6)
    groups = range(N_GROUPS)
    sl = [slice(gi * GROUP, (gi + 1) * GROUP) for gi in groups]

    att = [jnp.where(t_pos == s_pos, _mm(by_pair(q_w[:, s]), by_parity(kin_w[:, s]), NT, 1), 0.0) for s in sl]
    row = lax.broadcasted_iota(jnp.int32, cum.shape, 0)
    last = cum
    for l in range(CHUNK_LOG2):
        m = 1 << l
        in_low = (row & (2 * m - 1)) < m
        d = cum - jnp.where(in_low, last, pltpu.roll(last, m, 0))
        last = jnp.where(in_low, pltpu.roll(last, C - m, 0), last)
        e = jnp.exp(-jnp.abs(d))
        ql, kl = q_w * e, kin_w * e
        level = (differ >> l) == 1
        att = [jnp.where(level, _mm(by_pair(ql[:, s]), by_parity(kl[:, s]), NT, 1), att[gi])
               for gi, s in enumerate(sl)]

    outs = []
    for gi, s in enumerate(sl):
        S = S_s[gi]
        v = iv_w[:, s]
        intra = _mm(att[gi], by_parity(v), NN, 1)
        outs.append(jnp.where(low_pair, intra[:C], intra[C:]) + _mm(qd[:, s], S, NT, 1))
        Z = _mm(v.T, kd[:, s], NN, 1)
        S_s[gi] = S * e_last[:, s] + jnp.where(same_head, Z, 0.0)

    res = _hgrn_post(_tall(jnp.concatenate(outs, axis=1)), ph[:, 3 * C_HEADS:], norm_g[...])
    o_ref[...] = res.reshape(SEQS, C, C_HEADS)

    @pl.when(j == pl.num_programs(1) - 1)
    def _():
        _group_state_store(S_s, sT_ref, transpose=True)


def _lru_seq_kernel(x_ref, norm_g, w_ref, c0_ref, h0_ref, conv_w, conv_b, wa, ba, wx, bx, lam,
                    o_ref, hT_ref, cT_ref, xbuf_s, h_s, a_s, u_s, *, tm):
    j = pl.program_id(1)

    @pl.when(j == 0)
    def _():
        xbuf_s[5:8, :] = c0_ref[0]
        h_s[...] = h0_ref[0]

    blk = _seq_project(x_ref, norm_g, w_ref)
    xb = blk[:, :C_LRU]
    xbuf_s[8:8 + tm, :] = xb
    x1 = xbuf_s[7:7 + tm, :]
    x2 = xbuf_s[6:6 + tm, :]
    x3 = xbuf_s[5:5 + tm, :]
    a, u = _lru_pre(xb, x1, x2, x3, conv_w[...], conv_b[...], wa[...], ba[...], wx[...], bx[...], lam[...])
    xbuf_s[5:8, :] = xb[tm - 3:tm, :]

    row8 = lax.broadcasted_iota(jnp.int32, a.shape, 0) & 7
    for d in (1, 2, 4):
        ok = row8 >= d
        u = jnp.where(ok, a * pltpu.roll(u, d, 0) + u, u)
        a = jnp.where(ok, a * pltpu.roll(a, d, 0), a)
    a_s[...] = a
    u_s[...] = u

    def group(gi, h):
        i = pl.multiple_of(gi * 8, 8)
        ht = u_s[pl.ds(i, 8), :] + a_s[pl.ds(i, 8), :] * h
        o_ref[0, pl.ds(i, 8), :] = ht
        return ht[7:8, :]

    h = lax.fori_loop(0, tm // 8, group, h_s[...])
    h_s[...] = h
    o_ref[0] = o_ref[0] * jax.nn.gelu(blk[:, C_LRU:])

    @pl.when(j == pl.num_programs(1) - 1)
    def _():
        hT_ref[0] = h
        cT_ref[0] = xb[tm - (CONV_WIDTH - 1):tm, :]


def _full_spec(a):
    return pl.BlockSpec(a.shape, lambda b, j: (0,) * a.ndim)


def _seq_block(c):
    return pl.BlockSpec((SEQS, CHUNK, c), lambda b, j: (b, j, 0))


_STATE_BLOCK = pl.BlockSpec((SEQS, N_HEADS, HEAD_DIM, HEAD_DIM), lambda b, j: (b, 0, 0, 0))
_GROUP_STATE = pltpu.VMEM((N_GROUPS, GROUP, GROUP), F32)


def _rwkv_seq(x, shift, norm_g, w, s0, prm, B, T):
    o, sT = pl.pallas_call(
        _rwkv_seq_kernel,
        grid=(B // SEQS, T // CHUNK),
        in_specs=[_seq_block(D_MODEL), pl.BlockSpec((SEQS, 1, D_MODEL), lambda b, j: (b, 0, 0)),
                  _full_spec(norm_g), _full_spec(w), _STATE_BLOCK] + [_full_spec(a) for a in prm],
        out_specs=[_seq_block(C_HEADS), _STATE_BLOCK],
        out_shape=[jax.ShapeDtypeStruct((B, T, C_HEADS), F32), jax.ShapeDtypeStruct(s0.shape, F32)],
        scratch_shapes=[pltpu.VMEM((SEQS, 1, RWKV_COLS), F32), _GROUP_STATE],
        compiler_params=_cparams("parallel", "arbitrary"),
        name="rwkv_seq",
    )(x, shift.reshape(B, 1, D_MODEL), norm_g, w, s0, *prm)
    return o.reshape(B * T, C_HEADS), sT


def _hgrn_seq(x, norm1_g, w, s0, lb, norm_g, B, T):
    o, sT = pl.pallas_call(
        _hgrn_seq_kernel,
        grid=(B // SEQS, T // CHUNK),
        in_specs=[_seq_block(D_MODEL), _full_spec(norm1_g), _full_spec(w), _STATE_BLOCK,
                  _full_spec(lb), _full_spec(norm_g)],
        out_specs=[_seq_block(C_HEADS), _STATE_BLOCK],
        out_shape=[jax.ShapeDtypeStruct((B, T, C_HEADS), F32), jax.ShapeDtypeStruct(s0.shape, F32)],
        scratch_shapes=[_GROUP_STATE],
        compiler_params=_cparams("parallel", "arbitrary"),
        name="hgrn_seq",
    )(x, norm1_g, w, s0, lb, norm_g)
    return o.reshape(B * T, C_HEADS), sT


def _lru_seq(x, norm_g, w, conv0, h0, prm, B, T):
    tm = min(T, 512)
    seq = lambda c: pl.BlockSpec((1, tm, c), lambda b, j: (b, j, 0))
    hblk = pl.BlockSpec((1, 1, C_LRU), lambda b, j: (b, 0, 0))
    cblk = pl.BlockSpec((1, CONV_WIDTH - 1, C_LRU), lambda b, j: (b, 0, 0))
    o, hT, cT = pl.pallas_call(
        functools.partial(_lru_seq_kernel, tm=tm),
        grid=(B, T // tm),
        in_specs=[seq(D_MODEL), _full_spec(norm_g), _full_spec(w), cblk, hblk] + [_full_spec(a) for a in prm],
        out_specs=[seq(C_LRU), hblk, cblk],
        out_shape=[jax.ShapeDtypeStruct((B, T, C_LRU), F32), jax.ShapeDtypeStruct((B, 1, C_LRU), F32),
                   jax.ShapeDtypeStruct((B, CONV_WIDTH - 1, C_LRU), F32)],
        scratch_shapes=[pltpu.VMEM((tm + 8, C_LRU), F32), pltpu.VMEM((1, C_LRU), F32),
                        pltpu.VMEM((tm, C_LRU), F32), pltpu.VMEM((tm, C_LRU), F32)],
        compiler_params=_cparams("parallel", "arbitrary"),
        name="lru_seq",
    )(x, norm_g, w, conv0, h0.reshape(B, 1, C_LRU), *prm)
    return o.reshape(B * T, C_LRU), hT.reshape(B, C_LRU), cT


def _block_diag(w):
    out = jnp.zeros((C_LRU, C_LRU), w.dtype)
    n = C_LRU // LRU_BLOCKS
    for i in range(LRU_BLOCKS):
        out = out.at[i * n:(i + 1) * n, i * n:(i + 1) * n].set(w[i])
    return out


def _prepare_params(p):
    row = lambda a: a.reshape(1, -1)
    zpad = jnp.zeros((W_LORA, C_HEADS), F32)
    layers = []
    for l in range(DEPTH):
        pre = (row(p['mu_shift'][l]), row(p['rwkv_w0'][l]),
               jnp.concatenate([p['rwkv_w_up'][l], zpad], 0).astype(BF16),
               row(p['rwkv_a0'][l]),
               jnp.concatenate([zpad, p['rwkv_a_up'][l]], 0).astype(BF16),
               p['rwkv_g_up'][l].astype(BF16), row(p['rwkv_k_k'][l]), row(p['rwkv_k_a'][l]),
               None,
               p['lru_conv_w'][l], row(p['lru_conv_b'][l]),
               _block_diag(p['lru_wa'][l]).astype(BF16), row(p['lru_ba'][l]),
               _block_diag(p['lru_wx'][l]).astype(BF16), row(p['lru_bx'][l]), row(p['lru_lambda'][l]))
        w_in = p['w_in'][l]
        layers.append(dict(
            norm1_g=row(p['norm1_g'][l]), pre=pre,
            w_r=w_in[:, :RWKV_COLS].astype(BF16),
            w_h=w_in[:, RWKV_COLS:RWKV_COLS + HGRN_COLS].astype(BF16),
            w_l=w_in[:, RWKV_COLS + HGRN_COLS:].astype(BF16),
            r_k=row(p['rwkv_r_k'][l]), ln_w=row(p['rwkv_ln_w'][l]), ln_b=row(p['rwkv_ln_b'][l]),
            hgrn_norm_g=row(p['hgrn_norm_g'][l]),
            w_out=p['w_out'][l].astype(BF16), norm2_g=row(p['norm2_g'][l]),
            mlp_w1=p['mlp_w1'][l].astype(BF16), mlp_w2=p['mlp_w2'][l].astype(BF16)))
    return layers


def _trunk(x, wkv, shift, hgrn, lru, conv, layers, lb_all, final_g):
    B, T, _ = x.shape
    m = B * T
    assert T == 1 or (T % CHUNK == 0 and B % SEQS == 0), "sequence kernels tile (B, T) by (SEQS, CHUNK)"
    x = x.reshape(m, D_MODEL)
    n_wkv, n_shift, n_hgrn, n_lru, n_conv = [], [], [], [], []
    if T == 1:
        wkv_t, hgrn_t = jnp.transpose(wkv, (0, 2, 3, 4, 1)), jnp.transpose(hgrn, (0, 2, 3, 4, 1))
        s_wkv_t, s_hgrn_t = jnp.zeros(wkv_t.shape, F32), jnp.zeros(hgrn_t.shape, F32)
    for l, lp in enumerate(layers):
        n_shift.append(_rmsnorm_rows(x.reshape(B, T, D_MODEL)[:, -1], lp['norm1_g']))
        prm = list(lp['pre'])
        prm[8] = lb_all[l:l + 1]
        if T == 1:
            pr, ph, plru, p0 = _norm_proj(x, shift[l], lp['norm1_g'], lp['w_r'], lp['w_h'], lp['w_l'])
            (r, k, v, g, rT, wT, kT, vT, kkT, kaT, qT, fT, kinT, ivT, h_l) = _step_pre(
                pr, ph, plru, p0, conv[l], lru[l], prm)
            woT, s_wkv_t = _state_step(_rwkv_step_kernel, "rwkv_step", (rT, wT, kT, vT, kkT, kaT),
                                       wkv_t, l, s_wkv_t)
            hoT, s_hgrn_t = _state_step(_hgrn_step_kernel, "hgrn_step", (qT, fT, kinT, ivT),
                                        hgrn_t, l, s_hgrn_t)
            o_r, o_h, o_l = _step_post(woT, r, k, v, g, lp['r_k'], lp['ln_w'], lp['ln_b'],
                                       hoT, ph, lp['hgrn_norm_g'], h_l, plru)
            c_l = jnp.concatenate([conv[l][:, 1:], plru[:, None, :C_LRU]], axis=1)
        else:
            x3 = x.reshape(B, T, D_MODEL)
            o_r, S_r = _rwkv_seq(x3, shift[l], lp['norm1_g'], lp['w_r'], wkv[l],
                                 prm[:8] + [lp['r_k'], lp['ln_w'], lp['ln_b']], B, T)
            o_h, S_h = _hgrn_seq(x3, lp['norm1_g'], lp['w_h'], hgrn[l], prm[8], lp['hgrn_norm_g'], B, T)
            o_l, h_l, c_l = _lru_seq(x3, lp['norm1_g'], lp['w_l'], conv[l], lru[l], prm[9:], B, T)
            n_wkv.append(S_r)
            n_hgrn.append(S_h)
        x = _out_mlp(x, o_r, o_h, o_l, lp['w_out'], lp['norm2_g'], lp['mlp_w1'], lp['mlp_w2'])
        n_lru.append(h_l)
        n_conv.append(c_l)
    y = _rmsnorm_rows(x, final_g).reshape(B, T, D_MODEL)
    if T == 1:
        s_wkv, s_hgrn = jnp.transpose(s_wkv_t, (0, 4, 1, 2, 3)), jnp.transpose(s_hgrn_t, (0, 4, 1, 2, 3))
    else:
        s_wkv, s_hgrn = jnp.stack(n_wkv), jnp.stack(n_hgrn)
    return (y, s_wkv, jnp.stack(n_shift), s_hgrn, jnp.stack(n_lru), jnp.stack(n_conv))


def kernel(x_prompt, x_sample, state_wkv, state_shift, state_hgrn, state_lru, state_conv, norm1_g, w_in, mu_shift, rwkv_w0, rwkv_w_up, rwkv_a0, rwkv_a_up, rwkv_g_up, rwkv_k_k, rwkv_k_a, rwkv_r_k, rwkv_ln_w, rwkv_ln_b, hgrn_lb, hgrn_norm_g, lru_conv_w, lru_conv_b, lru_wa, lru_ba, lru_wx, lru_bx, lru_lambda, w_out, norm2_g, mlp_w1, mlp_w2, final_g):
    prm = dict(norm1_g=norm1_g, w_in=w_in, mu_shift=mu_shift, rwkv_w0=rwkv_w0, rwkv_w_up=rwkv_w_up,
               rwkv_a0=rwkv_a0, rwkv_a_up=rwkv_a_up, rwkv_g_up=rwkv_g_up, rwkv_k_k=rwkv_k_k,
               rwkv_k_a=rwkv_k_a, rwkv_r_k=rwkv_r_k.reshape(DEPTH, C_HEADS), rwkv_ln_w=rwkv_ln_w,
               rwkv_ln_b=rwkv_ln_b, hgrn_norm_g=hgrn_norm_g, lru_conv_w=lru_conv_w,
               lru_conv_b=lru_conv_b, lru_wa=lru_wa, lru_ba=lru_ba, lru_wx=lru_wx, lru_bx=lru_bx,
               lru_lambda=lru_lambda, w_out=w_out, norm2_g=norm2_g, mlp_w1=mlp_w1, mlp_w2=mlp_w2)
    layers = _prepare_params(prm)
    lb_all = _hgrn_lower_bounds(hgrn_lb)
    fg = final_g.reshape(1, D_MODEL)
    Bp = x_prompt.shape[0]
    dt = x_prompt.dtype
    z_wkv = jnp.zeros((DEPTH, Bp, N_HEADS, HEAD_DIM, HEAD_DIM), dt)
    z_shift = jnp.zeros((DEPTH, Bp, D_MODEL), dt)
    z_hgrn = jnp.zeros((DEPTH, Bp, N_HEADS, HEAD_DIM, HEAD_DIM), dt)
    z_lru = jnp.zeros((DEPTH, Bp, C_LRU), dt)
    z_conv = jnp.zeros((DEPTH, Bp, CONV_WIDTH - 1, C_LRU), dt)
    y_p, p_wkv, p_shift, p_hgrn, p_lru, p_conv = _trunk(x_prompt, z_wkv, z_shift, z_hgrn, z_lru, z_conv,
                                                         layers, lb_all, fg)
    y_s, s_wkv, s_shift, s_hgrn, s_lru, s_conv = _trunk(x_sample, state_wkv, state_shift, state_hgrn,
                                                         state_lru, state_conv, layers, lb_all, fg)
    return (y_p, y_s, p_wkv, p_shift, p_hgrn, p_lru, p_conv, s_wkv, s_shift, s_hgrn, s_lru, s_conv)
```

```python
import functools
import math

import jax
import jax.numpy as jnp
from jax import lax
from jax.experimental import pallas as pl
from jax.experimental.pallas import tpu as pltpu

F32 = jnp.float32
BF16 = jnp.bfloat16

D_MODEL = 1024
DEPTH = 4
HEAD_DIM = 64
N_HEADS = 6
C_HEADS = N_HEADS * HEAD_DIM
C_LRU = 256
LRU_BLOCKS = 4
CONV_WIDTH = 4
LRU_C = 8.0
W_LORA = 64
A_LORA = 64
G_LORA = 128
RWKV_COLS = 3 * C_HEADS + W_LORA + A_LORA + G_LORA
HGRN_COLS = 4 * C_HEADS
LRU_COLS = 2 * C_LRU
C_IN = RWKV_COLS + HGRN_COLS + LRU_COLS
D_FF = 4 * D_MODEL
NORM_EPS = 1e-6
GN_EPS = 64e-5
DECAY_SCALE = math.exp(-0.5)

LANES = 128
VMEM_LIMIT_BYTES = 48 * 1024 * 1024


def _cparams(*sem):
    return pltpu.CompilerParams(dimension_semantics=sem, vmem_limit_bytes=VMEM_LIMIT_BYTES)


def _dot(a, b):
    return jnp.dot(a.astype(BF16), b.astype(BF16), preferred_element_type=F32)


def _rmsnorm(x, g):
    return x * lax.rsqrt(jnp.mean(x * x, axis=-1, keepdims=True) + NORM_EPS) * g


def _softplus(x):
    return jnp.maximum(x, 0.0) + jnp.log1p(jnp.exp(-jnp.abs(x)))


def _head_sum(x):
    m = x.shape[0]
    lo = lax.broadcasted_iota(jnp.int32, (m, LANES), 1) < HEAD_DIM
    outs = []
    for p in range(x.shape[1] // LANES):
        xp = x[:, p * LANES:(p + 1) * LANES]
        s_lo = jnp.sum(jnp.where(lo, xp, 0.0), axis=1, keepdims=True)
        s_hi = jnp.sum(jnp.where(lo, 0.0, xp), axis=1, keepdims=True)
        outs.append(jnp.where(lo, s_lo, s_hi))
    return jnp.concatenate(outs, axis=1)


def _norm_proj_kernel(x_ref, shift_ref, g_ref, wr_ref, wh_ref, wl_ref, pr_ref, ph_ref, pl_ref, p0_ref):
    xn = _rmsnorm(x_ref[...], g_ref[...]).astype(BF16)
    pr_ref[...] = jnp.dot(xn, wr_ref[...], preferred_element_type=F32)
    ph_ref[...] = jnp.dot(xn, wh_ref[...], preferred_element_type=F32)
    pl_ref[...] = jnp.dot(xn, wl_ref[...], preferred_element_type=F32)
    p0_ref[...] = jnp.dot(shift_ref[...].astype(BF16), wr_ref[...], preferred_element_type=F32)


def _norm_proj(x, shift, g, w_r, w_h, w_l):
    m = x.shape[0]
    return pl.pallas_call(
        _norm_proj_kernel,
        out_shape=[jax.ShapeDtypeStruct((m, c), F32) for c in (RWKV_COLS, HGRN_COLS, LRU_COLS, RWKV_COLS)],
        compiler_params=pltpu.CompilerParams(vmem_limit_bytes=VMEM_LIMIT_BYTES),
        name="norm_proj",
    )(x, shift, g, w_r, w_h, w_l)


def _out_mlp_kernel(x_ref, or_ref, oh_ref, ol_ref, wo_ref, g2_ref, w1_ref, w2_ref, o_ref, x1_s, xn_s, acc_s):
    j = pl.program_id(1)

    @pl.when(j == 0)
    def _():
        mix = jnp.concatenate([or_ref[...], oh_ref[...], ol_ref[...]], axis=1)
        x1 = x_ref[...] + jnp.dot(mix.astype(BF16), wo_ref[...], preferred_element_type=F32)
        x1_s[...] = x1
        xn_s[...] = _rmsnorm(x1, g2_ref[...]).astype(BF16)
        acc_s[...] = jnp.zeros_like(acc_s)

    h = jnp.dot(xn_s[...], w1_ref[...], preferred_element_type=F32)
    h = jnp.square(jnp.maximum(h, 0.0))
    acc_s[...] += jnp.dot(h.astype(BF16), w2_ref[...], preferred_element_type=F32)

    @pl.when(j == pl.num_programs(1) - 1)
    def _():
        o_ref[...] = x1_s[...] + acc_s[...]


def _out_mlp(x, o_r, o_h, o_l, wo, g2, w1, w2):
    m = x.shape[0]
    tm = min(m, 512)
    tf = 2048
    row = pl.BlockSpec((tm, D_MODEL), lambda i, j: (i, 0))
    part = lambda c: pl.BlockSpec((tm, c), lambda i, j: (i, 0))
    return pl.pallas_call(
        _out_mlp_kernel,
        grid=(m // tm, D_FF // tf),
        in_specs=[row, part(C_HEADS), part(C_HEADS), part(C_LRU),
                  pl.BlockSpec((D_MODEL, D_MODEL), lambda i, j: (0, 0)),
                  pl.BlockSpec((1, D_MODEL), lambda i, j: (0, 0)),
                  pl.BlockSpec((D_MODEL, tf), lambda i, j: (0, j)),
                  pl.BlockSpec((tf, D_MODEL), lambda i, j: (j, 0))],
        out_specs=row,
        out_shape=jax.ShapeDtypeStruct((m, D_MODEL), F32),
        scratch_shapes=[pltpu.VMEM((tm, D_MODEL), F32), pltpu.VMEM((tm, D_MODEL), BF16),
                        pltpu.VMEM((tm, D_MODEL), F32)],
        compiler_params=_cparams("parallel", "arbitrary"),
        name="out_mlp",
    )(x, o_r, o_h, o_l, wo, g2, w1, w2)


def _rmsnorm_rows_kernel(x_ref, g_ref, o_ref):
    o_ref[...] = _rmsnorm(x_ref[...], g_ref[...])


def _rmsnorm_rows(x, g):
    m = x.shape[0]
    tm = min(m, 512)
    row = pl.BlockSpec((tm, D_MODEL), lambda i: (i, 0))
    return pl.pallas_call(
        _rmsnorm_rows_kernel,
        grid=(m // tm,),
        in_specs=[row, pl.BlockSpec((1, D_MODEL), lambda i: (0, 0))],
        out_specs=row,
        out_shape=jax.ShapeDtypeStruct((m, D_MODEL), F32),
        compiler_params=_cparams("parallel"),
        name="rmsnorm_rows",
    )(x, g)


def _lb_kernel(p_ref, o_ref):
    p = p_ref[...]
    e = jnp.exp(p - jnp.max(p, axis=0, keepdims=True))
    s = e / jnp.sum(e, axis=0, keepdims=True)
    acc = jnp.zeros_like(s[0:1])
    for l in range(DEPTH):
        if l > 0:
            acc = acc + s[l:l + 1]
        o_ref[l:l + 1, :] = acc


def _hgrn_lower_bounds(hgrn_lb):
    return pl.pallas_call(_lb_kernel, out_shape=jax.ShapeDtypeStruct(hgrn_lb.shape, F32),
                          name="hgrn_lower_bounds")(hgrn_lb)


def _rwkv_pre(pr, prev, mu, w0, w_up, a0, a_up, g_up, k_k, k_a):
    c = C_HEADS
    pm = pr + mu * (prev - pr)
    r = pm[:, :c]
    k0 = pm[:, c:2 * c]
    v = pm[:, 2 * c:3 * c]
    xwa = pm[:, 3 * c:3 * c + W_LORA + A_LORA]
    xg = pm[:, 3 * c + W_LORA + A_LORA:]
    log_decay = -DECAY_SCALE * jax.nn.sigmoid(w0 + _dot(jnp.tanh(xwa), w_up))
    a = jax.nn.sigmoid(a0 + _dot(xwa, a_up))
    g = _dot(jax.nn.sigmoid(xg), g_up)
    kk = k0 * k_k
    kk = kk * lax.rsqrt(jnp.maximum(_head_sum(kk * kk), 1e-24))
    k = k0 * (1.0 + (a - 1.0) * k_a)
    return r, log_decay, k, v, kk, kk * a, g


def _hgrn_pre(fpre, lb):
    sg = jax.nn.sigmoid(fpre)
    f = lb + (1.0 - lb) * sg
    kin = (1.0 - lb) * jax.nn.sigmoid(-fpre)
    return f, kin


def _lru_pre(xb, x1, x2, x3, conv_w, conv_b, wa, ba, wx, bx, lam):
    xc = conv_b + (((x3 * conv_w[0:1] + x2 * conv_w[1:2]) + x1 * conv_w[2:3]) + xb * conv_w[3:4])
    r = jax.nn.sigmoid(_dot(xc, wa) + ba)
    i = jax.nn.sigmoid(_dot(xc, wx) + bx)
    log_a = -LRU_C * r * _softplus(-lam)
    a = jnp.exp(log_a)
    one_minus_a2 = -jnp.tanh(log_a) * (a * a + 1.0)
    u = xc * i * jnp.sqrt(jnp.maximum(one_minus_a2, 1e-12))
    return a, u


_N_PRE_PARAMS = 16


def _step_pre_kernel(*refs):
    pr_ref, ph_ref, pl_ref, p0_ref, c0_ref, c1_ref, c2_ref, h0_ref = refs[:8]
    (mu, w0, w_up, a0, a_up, g_up, k_k, k_a, lb, conv_w, conv_b, wa, ba, wx, bx, lam) = refs[8:8 + _N_PRE_PARAMS]
    (r_o, k_o, v_o, g_o, rT_o, wT_o, kT_o, vT_o, kkT_o, kaT_o,
     qT_o, fT_o, kinT_o, ivT_o, h_o) = refs[8 + _N_PRE_PARAMS:]
    r, lw, k, v, kk, ka, g = _rwkv_pre(pr_ref[...], p0_ref[...], mu[...], w0[...], w_up[...], a0[...],
                                       a_up[...], g_up[...], k_k[...], k_a[...])
    r_o[...], k_o[...], v_o[...], g_o[...] = r, k, v, g
    for o_ref, val in ((rT_o, r), (wT_o, jnp.exp(lw)), (kT_o, k), (vT_o, v), (kkT_o, kk), (kaT_o, ka)):
        o_ref[...] = val.T
    ph = ph_ref[...]
    f, kin = _hgrn_pre(ph[:, C_HEADS:2 * C_HEADS], lb[...])
    for o_ref, val in ((qT_o, ph[:, :C_HEADS]), (fT_o, f), (kinT_o, kin), (ivT_o, ph[:, 2 * C_HEADS:3 * C_HEADS])):
        o_ref[...] = val.T
    a, u = _lru_pre(pl_ref[...][:, :C_LRU], c2_ref[...], c1_ref[...], c0_ref[...], conv_w[...], conv_b[...],
                    wa[...], ba[...], wx[...], bx[...], lam[...])
    h_o[...] = a * h0_ref[...] + u


def _step_pre(pr, ph, plru, p0, conv0, h0, prm):
    B = pr.shape[0]
    shapes = [(B, C_HEADS)] * 4 + [(C_HEADS, B)] * 10 + [(B, C_LRU)]
    return pl.pallas_call(
        _step_pre_kernel,
        out_shape=[jax.ShapeDtypeStruct(s, F32) for s in shapes],
        compiler_params=pltpu.CompilerParams(vmem_limit_bytes=VMEM_LIMIT_BYTES),
        name="step_pre",
    )(pr, ph, plru, p0, conv0[:, 0], conv0[:, 1], conv0[:, 2], h0, *prm)


def _rwkv_step_kernel(r_ref, w_ref, k_ref, v_ref, kk_ref, ka_ref, s_ref, o_ref, sT_ref):
    kk, w, ka, k, r = kk_ref[...], w_ref[...], ka_ref[...], k_ref[...], r_ref[...]

    def value_row(vi, carry):
        S = s_ref[vi]
        sa = -jnp.sum(S * kk, axis=0, keepdims=True)
        S = S * w + sa * ka + v_ref[pl.ds(vi, 1), :] * k
        sT_ref[vi] = S
        o_ref[pl.ds(vi, 1), :] = jnp.sum(S * r, axis=0, keepdims=True)
        return carry

    lax.fori_loop(0, HEAD_DIM, value_row, 0, unroll=8)


def _hgrn_step_kernel(q_ref, f_ref, kin_ref, iv_ref, s_ref, o_ref, sT_ref):
    iv = iv_ref[...]

    def key_row(ki, acc):
        S = f_ref[pl.ds(ki, 1), :] * s_ref[ki] + kin_ref[pl.ds(ki, 1), :] * iv
        sT_ref[ki] = S
        return acc + S * q_ref[pl.ds(ki, 1), :]

    o_ref[...] = lax.fori_loop(0, HEAD_DIM, key_row, jnp.zeros_like(iv), unroll=8)


def _state_step(kernel, name, vecs, states, layer, new_states):
    B = states.shape[-1]
    n_vec = len(vecs)
    vblk = pl.BlockSpec((HEAD_DIM, B), lambda h: (h, 0))
    sblk = pl.BlockSpec((None, None, HEAD_DIM, HEAD_DIM, B), lambda h: (layer, h, 0, 0, 0))

    def body(*refs):
        kernel(*refs[:n_vec + 1], *refs[n_vec + 2:])

    return pl.pallas_call(
        body,
        grid=(N_HEADS,),
        in_specs=[vblk] * n_vec + [sblk, pl.BlockSpec(memory_space=pl.ANY)],
        out_specs=[vblk, sblk],
        out_shape=[jax.ShapeDtypeStruct((C_HEADS, B), F32), jax.ShapeDtypeStruct(states.shape, F32)],
        input_output_aliases={n_vec + 1: 1},
        compiler_params=_cparams("parallel"),
        name=name,
    )(*vecs, states, new_states)


def _rwkv_post(out, r, k, v, g, r_k, ln_w, ln_b):
    inv_n = 1.0 / HEAD_DIM
    mean = _head_sum(out) * inv_n
    d = out - mean
    var = _head_sum(d * d) * inv_n
    gn = d * lax.rsqrt(var + GN_EPS) * ln_w + ln_b
    bonus = _head_sum(r * k * r_k) * v
    return (gn + bonus) * g


def _hgrn_post(o, gate, norm_g):
    o = o * lax.rsqrt(_head_sum(o * o) * (1.0 / HEAD_DIM) + NORM_EPS)
    return o * norm_g * jax.nn.silu(gate)


def _step_post_kernel(woT_ref, r_ref, k_ref, v_ref, g_ref, rk_ref, lnw_ref, lnb_ref,
                      hoT_ref, hg_ref, hng_ref, lh_ref, lg_ref, or_ref, oh_ref, ol_ref):
    or_ref[...] = _rwkv_post(woT_ref[...].T, r_ref[...], k_ref[...], v_ref[...], g_ref[...],
                             rk_ref[...], lnw_ref[...], lnb_ref[...])
    oh_ref[...] = _hgrn_post(hoT_ref[...].T, hg_ref[...], hng_ref[...])
    ol_ref[...] = lh_ref[...] * jax.nn.gelu(lg_ref[...])


def _step_post(woT, r, k, v, g, r_k, ln_w, ln_b, hoT, ph, hng, lh, plru):
    B = r.shape[0]
    full = lambda a: pl.BlockSpec(a.shape, lambda i: (0,) * a.ndim)
    return pl.pallas_call(
        _step_post_kernel,
        grid=(1,),
        in_specs=[full(woT), full(r), full(k), full(v), full(g), full(r_k), full(ln_w), full(ln_b),
                  full(hoT), pl.BlockSpec((B, C_HEADS), lambda i: (0, 3)), full(hng), full(lh),
                  pl.BlockSpec((B, C_LRU), lambda i: (0, 1))],
        out_specs=[pl.BlockSpec((B, c), lambda i: (0, 0)) for c in (C_HEADS, C_HEADS, C_LRU)],
        out_shape=[jax.ShapeDtypeStruct((B, c), F32) for c in (C_HEADS, C_HEADS, C_LRU)],
        compiler_params=_cparams("arbitrary"),
        name="step_post",
    )(woT, r, k, v, g, r_k, ln_w, ln_b, hoT, ph, hng, lh, plru)


CHUNK = 64
CHUNK_LOG2 = 6
SEQS = 4
GROUP_HEADS = 4
GROUP = GROUP_HEADS * HEAD_DIM
N_GROUPS = SEQS * N_HEADS // GROUP_HEADS

NN = ((1,), (0,))
NT = ((1,), (1,))


def _mm(a, b, dims, passes=1):
    assert passes == 1
    return lax.dot_general(a.astype(BF16), b.astype(BF16), (dims, ((), ())), preferred_element_type=F32)


def _iota2(shape):
    return lax.broadcasted_iota(jnp.int32, shape, 0), lax.broadcasted_iota(jnp.int32, shape, 1)


def _head_of(group, i):
    hw = group * GROUP_HEADS + i
    return hw // N_HEADS, hw % N_HEADS


def _group_state_load(s_ref, S_s, transpose):
    z = jnp.zeros((HEAD_DIM, HEAD_DIM), F32)
    for gi in range(N_GROUPS):
        rows = []
        for i in range(GROUP_HEADS):
            b, h = _head_of(gi, i)
            blk = s_ref[b, h].T if transpose else s_ref[b, h]
            rows.append(jnp.concatenate([blk if i2 == i else z for i2 in range(GROUP_HEADS)], axis=1))
        S_s[gi] = jnp.concatenate(rows, axis=0)


def _group_state_store(S_s, s_ref, transpose):
    for gi in range(N_GROUPS):
        Sg = S_s[gi]
        for i in range(GROUP_HEADS):
            b, h = _head_of(gi, i)
            blk = Sg[i * HEAD_DIM:(i + 1) * HEAD_DIM, i * HEAD_DIM:(i + 1) * HEAD_DIM]
            s_ref[b, h] = blk.T if transpose else blk


def _wide(x):
    return jnp.concatenate([x[b * CHUNK:(b + 1) * CHUNK] for b in range(SEQS)], axis=1)


def _tall(x):
    return jnp.concatenate([x[:, b * C_HEADS:(b + 1) * C_HEADS] for b in range(SEQS)], axis=0)


def _seq_cumsum(x):
    tr, tc = _iota2((SEQS * CHUNK, SEQS * CHUNK))
    tri = ((tr >= tc) & ((tr >> CHUNK_LOG2) == (tc >> CHUNK_LOG2))).astype(BF16)
    hi = x.astype(BF16)
    rest = x - hi.astype(F32)
    mid = rest.astype(BF16)
    lo = (rest - mid.astype(F32)).astype(BF16)
    dot = lambda part: jnp.dot(tri, part, preferred_element_type=F32)
    return dot(hi) + dot(mid) + dot(lo)


def _lane_head():
    return lax.broadcasted_iota(jnp.int32, (CHUNK, GROUP), 1) >> 6


def _pair_masks():
    lane = lax.broadcasted_iota(jnp.int32, (CHUNK, GROUP), 1)
    return lane < 2 * HEAD_DIM, (lane & HEAD_DIM) == 0


def _split_rows(x, mask):
    return jnp.concatenate([jnp.where(mask, x, 0.0), jnp.where(mask, 0.0, x)], axis=0)


def _pick(x, lane_head):
    out = x[:CHUNK]
    for i in range(1, GROUP_HEADS):
        out = jnp.where(lane_head == i, x[i * CHUNK:(i + 1) * CHUNK], out)
    return out


def _unit_lower_inverse(Ls):
    r, c = _iota2(Ls[0].shape)
    eye = (r == c).astype(F32)
    diag_blk = (r >> 4) == (c >> 4)
    mm = lambda xs, ys: [_mm(x, y, NN, 1) for x, y in zip(xs, ys)]
    Ld = [jnp.where(diag_blk, L, 0.0) for L in Ls]
    N = [jnp.where(diag_blk, 0.0, L) for L in Ls]
    L2 = mm(Ld, Ld)
    L4 = mm(L2, L2)
    P1 = mm([eye - x for x in Ld], [eye + x for x in L2])
    L8 = mm(L4, L4)
    P2 = mm([eye + x for x in L4], [eye + x for x in L8])
    Dinv = mm(P1, P2)
    M = mm(Dinv, N)
    M2 = mm(M, M)
    X = mm([eye - x for x in M], [eye + x for x in M2])
    return mm(X, Dinv)


def _seq_project(x_ref, norm_g, w_ref):
    x = x_ref[...].reshape(-1, D_MODEL)
    return jnp.dot(_rmsnorm(x, norm_g[...]).astype(BF16), w_ref[...], preferred_element_type=F32)


def _rwkv_seq_kernel(x_ref, shift_ref, norm_g, w_ref, s0_ref,
                     mu, w0, w_up, a0, a_up, g_up, k_k, k_a, r_k, ln_w, ln_b,
                     o_ref, sT_ref, carry_s, S_s):
    j = pl.program_id(1)
    C = CHUNK

    @pl.when(j == 0)
    def _():
        p0 = jnp.dot(shift_ref[...].reshape(SEQS, D_MODEL).astype(BF16), w_ref[...], preferred_element_type=F32)
        carry_s[...] = p0.reshape(SEQS, 1, RWKV_COLS)
        _group_state_load(s0_ref, S_s, transpose=False)

    pr = _seq_project(x_ref, norm_g, w_ref)
    prow = lax.broadcasted_iota(jnp.int32, pr.shape, 0)
    prev = pltpu.roll(pr, 1, 0)
    for b in range(SEQS):
        prev = jnp.where(prow == b * C, carry_s[b], prev)
        carry_s[b] = pr[(b + 1) * C - 1:(b + 1) * C, :]

    r, lw, k, v, kk, kb, g = _rwkv_pre(pr, prev, mu[...], w0[...], w_up[...], a0[...], a_up[...],
                                       g_up[...], k_k[...], k_a[...])
    cum_w, lw_w = _wide(_seq_cumsum(lw)), _wide(lw)
    r_w, k_w, v_w, kk_w, kb_w = _wide(r), _wide(k), _wide(v), _wide(kk), _wide(kb)
    cum_last = cum_w[C - 1:C, :]
    e_inv = jnp.exp(-cum_w)
    e_rel = jnp.exp(cum_last - cum_w)
    e_last = jnp.exp(cum_last)
    Kg = kk_w * jnp.exp(cum_w - lw_w)
    Rg = r_w * jnp.exp(cum_w)
    Ki, Bi = k_w * e_inv, kb_w * e_inv
    Kt, Bt = k_w * e_rel, kb_w * e_rel

    lane_head = _lane_head()
    low_pair, even_head = _pair_masks()
    by_pair = lambda x: _split_rows(x, low_pair)
    by_parity = lambda x: _split_rows(x, even_head)
    gr, gc = _iota2((GROUP, GROUP))
    same_head = (gr >> 6) == (gc >> 6)
    ar, ac = _iota2((2 * C, 2 * C))
    strict = (ar & (C - 1)) > (ac & (C - 1))
    incl = (ar & (C - 1)) >= (ac & (C - 1))
    low_half = lax.broadcasted_iota(jnp.int32, (C, 2 * C), 1) < HEAD_DIM

    groups = range(N_GROUPS)
    sl = [slice(gi * GROUP, (gi + 1) * GROUP) for gi in groups]
    P = [_mm(jnp.concatenate([by_pair(Kg[:, s]), by_pair(Rg[:, s])], axis=0),
             jnp.concatenate([by_parity(Ki[:, s]), by_parity(Bi[:, s])], axis=0), NT, 1) for s in sl]
    z2 = jnp.zeros((2 * C, 2 * C), F32)
    Akb = []
    for gi in groups:
        a2 = jnp.where(strict, P[gi][:2 * C, 2 * C:], 0.0)
        halves = [jnp.concatenate([jnp.where(low_half, a2[p * C:(p + 1) * C], 0.0),
                                   jnp.where(low_half, 0.0, a2[p * C:(p + 1) * C])], axis=0) for p in range(2)]
        Akb.append(jnp.concatenate([jnp.concatenate([halves[0], z2], axis=1),
                                    jnp.concatenate([z2, halves[1]], axis=1)], axis=0))
    Tinv = _unit_lower_inverse(Akb)
    S = [S_s[gi] for gi in groups]
    H0 = [_mm(jnp.concatenate([Kg[:, s], Rg[:, s]], axis=0), S[gi], NT, 1) for gi, s in enumerate(sl)]
    Vp = [by_parity(v_w[:, s]) for s in sl]
    AkkV = [_mm(jnp.where(strict, P[gi][:2 * C, :2 * C], 0.0), Vp[gi], NN, 1) for gi in groups]
    rhs_u = [jnp.concatenate([H0[gi][:C] + AkkV[gi][:C]] * 2 + [H0[gi][:C] + AkkV[gi][C:]] * 2, axis=0)
             for gi in groups]
    U = [_pick(_mm(Tinv[gi], rhs_u[gi], NN, 1), lane_head) for gi in groups]
    O = []
    for gi in groups:
        Y = _mm(jnp.concatenate([jnp.where(incl, P[gi][2 * C:, :2 * C], 0.0),
                                 jnp.where(incl, -P[gi][2 * C:, 2 * C:], 0.0)], axis=1),
                jnp.concatenate([Vp[gi], by_parity(U[gi])], axis=0), NN, 1)
        O.append(H0[gi][C:] + jnp.where(low_pair, Y[:C], Y[C:]))
    for gi, s in enumerate(sl):
        Z = _mm(jnp.concatenate([v_w[:, s], U[gi]], axis=0).T,
                jnp.concatenate([Kt[:, s], -Bt[:, s]], axis=0), NN, 1)
        S_s[gi] = S[gi] * e_last[:, s] + jnp.where(same_head, Z, 0.0)

    res = _rwkv_post(_tall(jnp.concatenate(O, axis=1)), r, k, v, g, r_k[...], ln_w[...], ln_b[...])
    o_ref[...] = res.reshape(SEQS, C, C_HEADS)

    @pl.when(j == pl.num_programs(1) - 1)
    def _():
        _group_state_store(S_s, sT_ref, transpose=False)


def _hgrn_seq_kernel(x_ref, norm1_g, w_ref, s0_ref, lb, norm_g, o_ref, sT_ref, S_s):
    j = pl.program_id(1)
    C = CHUNK

    @pl.when(j == 0)
    def _():
        _group_state_load(s0_ref, S_s, transpose=True)

    ph = _seq_project(x_ref, norm1_g, w_ref)
    f, kin = _hgrn_pre(ph[:, C_HEADS:2 * C_HEADS], lb[...])
    cum = _wide(_seq_cumsum(jnp.log(f)))
    q_w, kin_w, iv_w = _wide(ph[:, :C_HEADS]), _wide(kin), _wide(ph[:, 2 * C_HEADS:3 * C_HEADS])
    cum_last = cum[C - 1:C, :]
    e_last = jnp.exp(cum_last)
    qd = q_w * jnp.exp(cum)
    kd = kin_w * jnp.exp(cum_last - cum)

    low_pair, even_head = _pair_masks()
    by_pair = lambda x: _split_rows(x, low_pair)
    by_parity = lambda x: _split_rows(x, even_head)
    ar, ac = _iota2((2 * C, 2 * C))
    t_pos, s_pos = ar & (C - 1), ac & (C - 1)
    differ = jnp.where(t_pos > s_pos, t_pos ^ s_pos, 0)
    gr, gc = _iota2((GROUP, GROUP))
    same_head = (gr >> 6) == (gc >> 6)
    groups = range(N_GROUPS)
    sl = [slice(gi * GROUP, (gi + 1) * GROUP) for gi in groups]

    att = [jnp.where(t_pos == s_pos, _mm(by_pair(q_w[:, s]), by_parity(kin_w[:, s]), NT, 1), 0.0) for s in sl]
    row = lax.broadcasted_iota(jnp.int32, cum.shape, 0)
    last = cum

    def shift_rows(x, k):
        if 2 * abs(k) <= 8:
            x3 = x.reshape(C // 8, 8, x.shape[1])
            return pltpu.roll(x3, k % 8, 1).reshape(x.shape)
        return pltpu.roll(x, k % C, 0)

    for l in range(CHUNK_LOG2):
        m = 1 << l
        in_low = (row & (2 * m - 1)) < m
        d = cum - jnp.where(in_low, last, shift_rows(last, m))
        last = jnp.where(in_low, shift_rows(last, -m), last)
        e = jnp.exp(-jnp.abs(d))
        ql, kl = q_w * e, kin_w * e
        level = (differ >> l) == 1
        att = [jnp.where(level, _mm(by_pair(ql[:, s]), by_parity(kl[:, s]), NT, 1), att[gi])
               for gi, s in enumerate(sl)]

    outs = []
    for gi, s in enumerate(sl):
        S = S_s[gi]
        v = iv_w[:, s]
        intra = _mm(att[gi], by_parity(v), NN, 1)
        outs.append(jnp.where(low_pair, intra[:C], intra[C:]) + _mm(qd[:, s], S, NT, 1))
        Z = _mm(v.T, kd[:, s], NN, 1)
        S_s[gi] = S * e_last[:, s] + jnp.where(same_head, Z, 0.0)

    res = _hgrn_post(_tall(jnp.concatenate(outs, axis=1)), ph[:, 3 * C_HEADS:], norm_g[...])
    o_ref[...] = res.reshape(SEQS, C, C_HEADS)

    @pl.when(j == pl.num_programs(1) - 1)
    def _():
        _group_state_store(S_s, sT_ref, transpose=True)


def _lru_seq_kernel(x_ref, norm_g, w_ref, c0_ref, h0_ref, conv_w, conv_b, wa, ba, wx, bx, lam,
                    o_ref, hT_ref, cT_ref, xbuf_s, h_s, a_s, u_s, *, tm):
    j = pl.program_id(1)

    @pl.when(j == 0)
    def _():
        xbuf_s[5:8, :] = c0_ref[0]
        h_s[...] = h0_ref[0]

    blk = _seq_project(x_ref, norm_g, w_ref)
    xb = blk[:, :C_LRU]
    xbuf_s[8:8 + tm, :] = xb
    x1 = xbuf_s[7:7 + tm, :]
    x2 = xbuf_s[6:6 + tm, :]
    x3 = xbuf_s[5:5 + tm, :]
    a, u = _lru_pre(xb, x1, x2, x3, conv_w[...], conv_b[...], wa[...], ba[...], wx[...], bx[...], lam[...])
    xbuf_s[5:8, :] = xb[tm - 3:tm, :]

    row8 = lax.broadcasted_iota(jnp.int32, a.shape, 0) & 7
    for d in (1, 2, 4):
        ok = row8 >= d
        u = jnp.where(ok, a * pltpu.roll(u, d, 0) + u, u)
        a = jnp.where(ok, a * pltpu.roll(a, d, 0), a)
    a_s[...] = a
    u_s[...] = u

    def group(gi, h):
        i = pl.multiple_of(gi * 8, 8)
        ht = u_s[pl.ds(i, 8), :] + a_s[pl.ds(i, 8), :] * h
        o_ref[0, pl.ds(i, 8), :] = ht
        return ht[7:8, :]

    h = lax.fori_loop(0, tm // 8, group, h_s[...])
    h_s[...] = h
    o_ref[0] = o_ref[0] * jax.nn.gelu(blk[:, C_LRU:])

    @pl.when(j == pl.num_programs(1) - 1)
    def _():
        hT_ref[0] = h
        cT_ref[0] = xb[tm - (CONV_WIDTH - 1):tm, :]


def _full_spec(a):
    return pl.BlockSpec(a.shape, lambda b, j: (0,) * a.ndim)


def _seq_block(c):
    return pl.BlockSpec((SEQS, CHUNK, c), lambda b, j: (b, j, 0))


_STATE_BLOCK = pl.BlockSpec((SEQS, N_HEADS, HEAD_DIM, HEAD_DIM), lambda b, j: (b, 0, 0, 0))
_GROUP_STATE = pltpu.VMEM((N_GROUPS, GROUP, GROUP), F32)


def _rwkv_seq(x, shift, norm_g, w, s0, prm, B, T):
    o, sT = pl.pallas_call(
        _rwkv_seq_kernel,
        grid=(B // SEQS, T // CHUNK),
        in_specs=[_seq_block(D_MODEL), pl.BlockSpec((SEQS, 1, D_MODEL), lambda b, j: (b, 0, 0)),
                  _full_spec(norm_g), _full_spec(w), _STATE_BLOCK] + [_full_spec(a) for a in prm],
        out_specs=[_seq_block(C_HEADS), _STATE_BLOCK],
        out_shape=[jax.ShapeDtypeStruct((B, T, C_HEADS), F32), jax.ShapeDtypeStruct(s0.shape, F32)],
        scratch_shapes=[pltpu.VMEM((SEQS, 1, RWKV_COLS), F32), _GROUP_STATE],
        compiler_params=_cparams("parallel", "arbitrary"),
        name="rwkv_seq",
    )(x, shift.reshape(B, 1, D_MODEL), norm_g, w, s0, *prm)
    return o.reshape(B * T, C_HEADS), sT


def _hgrn_seq(x, norm1_g, w, s0, lb, norm_g, B, T):
    o, sT = pl.pallas_call(
        _hgrn_seq_kernel,
        grid=(B // SEQS, T // CHUNK),
        in_specs=[_seq_block(D_MODEL), _full_spec(norm1_g), _full_spec(w), _STATE_BLOCK,
                  _full_spec(lb), _full_spec(norm_g)],
        out_specs=[_seq_block(C_HEADS), _STATE_BLOCK],
        out_shape=[jax.ShapeDtypeStruct((B, T, C_HEADS), F32), jax.ShapeDtypeStruct(s0.shape, F32)],
        scratch_shapes=[_GROUP_STATE],
        compiler_params=_cparams("parallel", "arbitrary"),
        name="hgrn_seq",
    )(x, norm1_g, w, s0, lb, norm_g)
    return o.reshape(B * T, C_HEADS), sT


def _lru_seq(x, norm_g, w, conv0, h0, prm, B, T):
    tm = min(T, 512)
    seq = lambda c: pl.BlockSpec((1, tm, c), lambda b, j: (b, j, 0))
    hblk = pl.BlockSpec((1, 1, C_LRU), lambda b, j: (b, 0, 0))
    cblk = pl.BlockSpec((1, CONV_WIDTH - 1, C_LRU), lambda b, j: (b, 0, 0))
    o, hT, cT = pl.pallas_call(
        functools.partial(_lru_seq_kernel, tm=tm),
        grid=(B, T // tm),
        in_specs=[seq(D_MODEL), _full_spec(norm_g), _full_spec(w), cblk, hblk] + [_full_spec(a) for a in prm],
        out_specs=[seq(C_LRU), hblk, cblk],
        out_shape=[jax.ShapeDtypeStruct((B, T, C_LRU), F32), jax.ShapeDtypeStruct((B, 1, C_LRU), F32),
                   jax.ShapeDtypeStruct((B, CONV_WIDTH - 1, C_LRU), F32)],
        scratch_shapes=[pltpu.VMEM((tm + 8, C_LRU), F32), pltpu.VMEM((1, C_LRU), F32),
                        pltpu.VMEM((tm, C_LRU), F32), pltpu.VMEM((tm, C_LRU), F32)],
        compiler_params=_cparams("parallel", "arbitrary"),
        name="lru_seq",
    )(x, norm_g, w, conv0, h0.reshape(B, 1, C_LRU), *prm)
    return o.reshape(B * T, C_LRU), hT.reshape(B, C_LRU), cT


def _block_diag(w):
    out = jnp.zeros((C_LRU, C_LRU), w.dtype)
    n = C_LRU // LRU_BLOCKS
    for i in range(LRU_BLOCKS):
        out = out.at[i * n:(i + 1) * n, i * n:(i + 1) * n].set(w[i])
    return out


def _prepare_params(p):
    row = lambda a: a.reshape(1, -1)
    zpad = jnp.zeros((W_LORA, C_HEADS), F32)
    layers = []
    for l in range(DEPTH):
        pre = (row(p['mu_shift'][l]), row(p['rwkv_w0'][l]),
               jnp.concatenate([p['rwkv_w_up'][l], zpad], 0).astype(BF16),
               row(p['rwkv_a0'][l]),
               jnp.concatenate([zpad, p['rwkv_a_up'][l]], 0).astype(BF16),
               p['rwkv_g_up'][l].astype(BF16), row(p['rwkv_k_k'][l]), row(p['rwkv_k_a'][l]),
               None,
               p['lru_conv_w'][l], row(p['lru_conv_b'][l]),
               _block_diag(p['lru_wa'][l]).astype(BF16), row(p['lru_ba'][l]),
               _block_diag(p['lru_wx'][l]).astype(BF16), row(p['lru_bx'][l]), row(p['lru_lambda'][l]))
        w_in = p['w_in'][l]
        layers.append(dict(
            norm1_g=row(p['norm1_g'][l]), pre=pre,
            w_r=w_in[:, :RWKV_COLS].astype(BF16),
            w_h=w_in[:, RWKV_COLS:RWKV_COLS + HGRN_COLS].astype(BF16),
            w_l=w_in[:, RWKV_COLS + HGRN_COLS:].astype(BF16),
            r_k=row(p['rwkv_r_k'][l]), ln_w=row(p['rwkv_ln_w'][l]), ln_b=row(p['rwkv_ln_b'][l]),
            hgrn_norm_g=row(p['hgrn_norm_g'][l]),
            w_out=p['w_out'][l].astype(BF16), norm2_g=row(p['norm2_g'][l]),
            mlp_w1=p['mlp_w1'][l].astype(BF16), mlp_w2=p['mlp_w2'][l].astype(BF16)))
    return layers


def _trunk(x, wkv, shift, hgrn, lru, conv, layers, lb_all, final_g):
    B, T, _ = x.shape
    m = B * T
    assert T == 1 or (T % CHUNK == 0 and B % SEQS == 0), "sequence kernels tile (B, T) by (SEQS, CHUNK)"
    x = x.reshape(m, D_MODEL)
    n_wkv, n_shift, n_hgrn, n_lru, n_conv = [], [], [], [], []
    if T == 1:
        wkv_t, hgrn_t = jnp.transpose(wkv, (0, 2, 3, 4, 1)), jnp.transpose(hgrn, (0, 2, 3, 4, 1))
        s_wkv_t, s_hgrn_t = jnp.zeros(wkv_t.shape, F32), jnp.zeros(hgrn_t.shape, F32)
    for l, lp in enumerate(layers):
        n_shift.append(_rmsnorm_rows(x.reshape(B, T, D_MODEL)[:, -1], lp['norm1_g']))
        prm = list(lp['pre'])
        prm[8] = lb_all[l:l + 1]
        if T == 1:
            pr, ph, plru, p0 = _norm_proj(x, shift[l], lp['norm1_g'], lp['w_r'], lp['w_h'], lp['w_l'])
            (r, k, v, g, rT, wT, kT, vT, kkT, kaT, qT, fT, kinT, ivT, h_l) = _step_pre(
                pr, ph, plru, p0, conv[l], lru[l], prm)
            woT, s_wkv_t = _state_step(_rwkv_step_kernel, "rwkv_step", (rT, wT, kT, vT, kkT, kaT),
                                       wkv_t, l, s_wkv_t)
            hoT, s_hgrn_t = _state_step(_hgrn_step_kernel, "hgrn_step", (qT, fT, kinT, ivT),
                                        hgrn_t, l, s_hgrn_t)
            o_r, o_h, o_l = _step_post(woT, r, k, v, g, lp['r_k'], lp['ln_w'], lp['ln_b'],
                                       hoT, ph, lp['hgrn_norm_g'], h_l, plru)
            c_l = jnp.concatenate([conv[l][:, 1:], plru[:, None, :C_LRU]], axis=1)
        else:
            x3 = x.reshape(B, T, D_MODEL)
            o_r, S_r = _rwkv_seq(x3, shift[l], lp['norm1_g'], lp['w_r'], wkv[l],
                                 prm[:8] + [lp['r_k'], lp['ln_w'], lp['ln_b']], B, T)
            o_h, S_h = _hgrn_seq(x3, lp['norm1_g'], lp['w_h'], hgrn[l], prm[8], lp['hgrn_norm_g'], B, T)
            o_l, h_l, c_l = _lru_seq(x3, lp['norm1_g'], lp['w_l'], conv[l], lru[l], prm[9:], B, T)
            n_wkv.append(S_r)
            n_hgrn.append(S_h)
        x = _out_mlp(x, o_r, o_h, o_l, lp['w_out'], lp['norm2_g'], lp['mlp_w1'], lp['mlp_w2'])
        n_lru.append(h_l)
        n_conv.append(c_l)
    y = _rmsnorm_rows(x, final_g).reshape(B, T, D_MODEL)
    if T == 1:
        s_wkv, s_hgrn = jnp.transpose(s_wkv_t, (0, 4, 1, 2, 3)), jnp.transpose(s_hgrn_t, (0, 4, 1, 2, 3))
    else:
        s_wkv, s_hgrn = jnp.stack(n_wkv), jnp.stack(n_hgrn)
    return (y, s_wkv, jnp.stack(n_shift), s_hgrn, jnp.stack(n_lru), jnp.stack(n_conv))


def kernel(x_prompt, x_sample, state_wkv, state_shift, state_hgrn, state_lru, state_conv, norm1_g, w_in, mu_shift, rwkv_w0, rwkv_w_up, rwkv_a0, rwkv_a_up, rwkv_g_up, rwkv_k_k, rwkv_k_a, rwkv_r_k, rwkv_ln_w, rwkv_ln_b, hgrn_lb, hgrn_norm_g, lru_conv_w, lru_conv_b, lru_wa, lru_ba, lru_wx, lru_bx, lru_lambda, w_out, norm2_g, mlp_w1, mlp_w2, final_g):
    prm = dict(norm1_g=norm1_g, w_in=w_in, mu_shift=mu_shift, rwkv_w0=rwkv_w0, rwkv_w_up=rwkv_w_up,
               rwkv_a0=rwkv_a0, rwkv_a_up=rwkv_a_up, rwkv_g_up=rwkv_g_up, rwkv_k_k=rwkv_k_k,
               rwkv_k_a=rwkv_k_a, rwkv_r_k=rwkv_r_k.reshape(DEPTH, C_HEADS), rwkv_ln_w=rwkv_ln_w,
               rwkv_ln_b=rwkv_ln_b, hgrn_norm_g=hgrn_norm_g, lru_conv_w=lru_conv_w,
               lru_conv_b=lru_conv_b, lru_wa=lru_wa, lru_ba=lru_ba, lru_wx=lru_wx, lru_bx=lru_bx,
               lru_lambda=lru_lambda, w_out=w_out, norm2_g=norm2_g, mlp_w1=mlp_w1, mlp_w2=mlp_w2)
    layers = _prepare_params(prm)
    lb_all = _hgrn_lower_bounds(hgrn_lb)
    fg = final_g.reshape(1, D_MODEL)
    Bp = x_prompt.shape[0]
    dt = x_prompt.dtype
    z_wkv = [jnp.zeros((Bp, N_HEADS, HEAD_DIM, HEAD_DIM), dt)] * DEPTH
    z_shift = [jnp.zeros((Bp, D_MODEL), dt)] * DEPTH
    z_hgrn = [jnp.zeros((Bp, N_HEADS, HEAD_DIM, HEAD_DIM), dt)] * DEPTH
    z_lru = [jnp.zeros((Bp, C_LRU), dt)] * DEPTH
    z_conv = [jnp.zeros((Bp, CONV_WIDTH - 1, C_LRU), dt)] * DEPTH
    y_p, p_wkv, p_shift, p_hgrn, p_lru, p_conv = _trunk(x_prompt, z_wkv, z_shift, z_hgrn, z_lru, z_conv,
                                                         layers, lb_all, fg)
    y_s, s_wkv, s_shift, s_hgrn, s_lru, s_conv = _trunk(x_sample, state_wkv, state_shift, state_hgrn,
                                                         state_lru, state_conv, layers, lb_all, fg)
    return (y_p, y_s, p_wkv, p_shift, p_hgrn, p_lru, p_conv, s_wkv, s_shift, s_hgrn, s_lru, s_conv)
```

```python
import functools
import math

import jax
import jax.numpy as jnp
from jax import lax
from jax.experimental import pallas as pl
from jax.experimental.pallas import tpu as pltpu

F32 = jnp.float32
BF16 = jnp.bfloat16

D_MODEL = 1024
DEPTH = 4
HEAD_DIM = 64
N_HEADS = 6
C_HEADS = N_HEADS * HEAD_DIM
C_LRU = 256
LRU_BLOCKS = 4
CONV_WIDTH = 4
LRU_C = 8.0
W_LORA = 64
A_LORA = 64
G_LORA = 128
RWKV_COLS = 3 * C_HEADS + W_LORA + A_LORA + G_LORA
HGRN_COLS = 4 * C_HEADS
LRU_COLS = 2 * C_LRU
C_IN = RWKV_COLS + HGRN_COLS + LRU_COLS
D_FF = 4 * D_MODEL
NORM_EPS = 1e-6
GN_EPS = 64e-5
DECAY_SCALE = math.exp(-0.5)

LANES = 128
VMEM_LIMIT_BYTES = 48 * 1024 * 1024


def _cparams(*sem):
    return pltpu.CompilerParams(dimension_semantics=sem, vmem_limit_bytes=VMEM_LIMIT_BYTES)


def _dot(a, b):
    return jnp.dot(a.astype(BF16), b.astype(BF16), preferred_element_type=F32)


def _rmsnorm(x, g):
    return x * lax.rsqrt(jnp.mean(x * x, axis=-1, keepdims=True) + NORM_EPS) * g


def _softplus(x):
    return jnp.maximum(x, 0.0) + jnp.log1p(jnp.exp(-jnp.abs(x)))


def _head_sum(x):
    m = x.shape[0]
    lo = lax.broadcasted_iota(jnp.int32, (m, LANES), 1) < HEAD_DIM
    outs = []
    for p in range(x.shape[1] // LANES):
        xp = x[:, p * LANES:(p + 1) * LANES]
        s_lo = jnp.sum(jnp.where(lo, xp, 0.0), axis=1, keepdims=True)
        s_hi = jnp.sum(jnp.where(lo, 0.0, xp), axis=1, keepdims=True)
        outs.append(jnp.where(lo, s_lo, s_hi))
    return jnp.concatenate(outs, axis=1)


def _norm_proj_kernel(x_ref, shift_ref, g_ref, wr_ref, wh_ref, wl_ref, pr_ref, ph_ref, pl_ref, p0_ref):
    xn = _rmsnorm(x_ref[...], g_ref[...]).astype(BF16)
    pr_ref[...] = jnp.dot(xn, wr_ref[...], preferred_element_type=F32)
    ph_ref[...] = jnp.dot(xn, wh_ref[...], preferred_element_type=F32)
    pl_ref[...] = jnp.dot(xn, wl_ref[...], preferred_element_type=F32)
    p0_ref[...] = jnp.dot(shift_ref[...].astype(BF16), wr_ref[...], preferred_element_type=F32)


def _norm_proj(x, shift, g, w_r, w_h, w_l):
    m = x.shape[0]
    return pl.pallas_call(
        _norm_proj_kernel,
        out_shape=[jax.ShapeDtypeStruct((m, c), F32) for c in (RWKV_COLS, HGRN_COLS, LRU_COLS, RWKV_COLS)],
        compiler_params=pltpu.CompilerParams(vmem_limit_bytes=VMEM_LIMIT_BYTES),
        name="norm_proj",
    )(x, shift, g, w_r, w_h, w_l)


def _out_mlp_kernel(x_ref, or_ref, oh_ref, ol_ref, wo_ref, g2_ref, w1_ref, w2_ref, o_ref):
    mix = jnp.concatenate([or_ref[...], oh_ref[...], ol_ref[...]], axis=1)
    x1 = x_ref[...] + jnp.dot(mix.astype(BF16), wo_ref[...], preferred_element_type=F32)
    xn = _rmsnorm(x1, g2_ref[...]).astype(BF16)
    h = jnp.square(jnp.maximum(jnp.dot(xn, w1_ref[...], preferred_element_type=F32), 0.0))
    o_ref[...] = x1 + jnp.dot(h.astype(BF16), w2_ref[...], preferred_element_type=F32)


def _out_mlp(x, o_r, o_h, o_l, wo, g2, w1, w2):
    m = x.shape[0]
    tm = min(m, 256)
    row = lambda c: pl.BlockSpec((tm, c), lambda i: (i, 0))
    resident = lambda a: pl.BlockSpec(a.shape, lambda i: (0,) * a.ndim, pipeline_mode=pl.Buffered(1))
    return pl.pallas_call(
        _out_mlp_kernel,
        grid=(m // tm,),
        in_specs=[row(D_MODEL), row(C_HEADS), row(C_HEADS), row(C_LRU),
                  resident(wo), resident(g2), resident(w1), resident(w2)],
        out_specs=row(D_MODEL),
        out_shape=jax.ShapeDtypeStruct((m, D_MODEL), F32),
        compiler_params=_cparams("parallel"),
        name="out_mlp",
    )(x, o_r, o_h, o_l, wo, g2, w1, w2)


def _rmsnorm_rows_kernel(x_ref, g_ref, o_ref):
    o_ref[...] = _rmsnorm(x_ref[...], g_ref[...])


def _rmsnorm_rows(x, g):
    m = x.shape[0]
    tm = min(m, 512)
    row = pl.BlockSpec((tm, D_MODEL), lambda i: (i, 0))
    return pl.pallas_call(
        _rmsnorm_rows_kernel,
        grid=(m // tm,),
        in_specs=[row, pl.BlockSpec((1, D_MODEL), lambda i: (0, 0))],
        out_specs=row,
        out_shape=jax.ShapeDtypeStruct((m, D_MODEL), F32),
        compiler_params=_cparams("parallel"),
        name="rmsnorm_rows",
    )(x, g)


def _lb_kernel(p_ref, o_ref):
    p = p_ref[...]
    e = jnp.exp(p - jnp.max(p, axis=0, keepdims=True))
    s = e / jnp.sum(e, axis=0, keepdims=True)
    acc = jnp.zeros_like(s[0:1])
    for l in range(DEPTH):
        if l > 0:
            acc = acc + s[l:l + 1]
        o_ref[l:l + 1, :] = acc


def _hgrn_lower_bounds(hgrn_lb):
    return pl.pallas_call(_lb_kernel, out_shape=jax.ShapeDtypeStruct(hgrn_lb.shape, F32),
                          name="hgrn_lower_bounds")(hgrn_lb)


def _rwkv_pre(pr, prev, mu, w0, w_up, a0, a_up, g_up, k_k, k_a):
    c = C_HEADS
    pm = pr + mu * (prev - pr)
    r = pm[:, :c]
    k0 = pm[:, c:2 * c]
    v = pm[:, 2 * c:3 * c]
    xwa = pm[:, 3 * c:3 * c + W_LORA + A_LORA]
    xg = pm[:, 3 * c + W_LORA + A_LORA:]
    log_decay = -DECAY_SCALE * jax.nn.sigmoid(w0 + _dot(jnp.tanh(xwa), w_up))
    a = jax.nn.sigmoid(a0 + _dot(xwa, a_up))
    g = _dot(jax.nn.sigmoid(xg), g_up)
    kk = k0 * k_k
    kk = kk * lax.rsqrt(jnp.maximum(_head_sum(kk * kk), 1e-24))
    k = k0 * (1.0 + (a - 1.0) * k_a)
    return r, log_decay, k, v, kk, kk * a, g


def _hgrn_pre(fpre, lb):
    sg = jax.nn.sigmoid(fpre)
    f = lb + (1.0 - lb) * sg
    kin = (1.0 - lb) * jax.nn.sigmoid(-fpre)
    return f, kin


def _lru_pre(xb, x1, x2, x3, conv_w, conv_b, wa, ba, wx, bx, lam):
    xc = conv_b + (((x3 * conv_w[0:1] + x2 * conv_w[1:2]) + x1 * conv_w[2:3]) + xb * conv_w[3:4])
    r = jax.nn.sigmoid(_dot(xc, wa) + ba)
    i = jax.nn.sigmoid(_dot(xc, wx) + bx)
    log_a = -LRU_C * r * _softplus(-lam)
    a = jnp.exp(log_a)
    one_minus_a2 = -jnp.tanh(log_a) * (a * a + 1.0)
    u = xc * i * jnp.sqrt(jnp.maximum(one_minus_a2, 1e-12))
    return a, u


_N_PRE_PARAMS = 16


def _step_pre_kernel(*refs):
    pr_ref, ph_ref, pl_ref, p0_ref, c0_ref, c1_ref, c2_ref, h0_ref = refs[:8]
    (mu, w0, w_up, a0, a_up, g_up, k_k, k_a, lb, conv_w, conv_b, wa, ba, wx, bx, lam) = refs[8:8 + _N_PRE_PARAMS]
    (r_o, k_o, v_o, g_o, rT_o, wT_o, kT_o, vT_o, kkT_o, kaT_o,
     qT_o, fT_o, kinT_o, ivT_o, h_o) = refs[8 + _N_PRE_PARAMS:]
    r, lw, k, v, kk, ka, g = _rwkv_pre(pr_ref[...], p0_ref[...], mu[...], w0[...], w_up[...], a0[...],
                                       a_up[...], g_up[...], k_k[...], k_a[...])
    r_o[...], k_o[...], v_o[...], g_o[...] = r, k, v, g
    for o_ref, val in ((rT_o, r), (wT_o, jnp.exp(lw)), (kT_o, k), (vT_o, v), (kkT_o, kk), (kaT_o, ka)):
        o_ref[...] = val.T
    ph = ph_ref[...]
    f, kin = _hgrn_pre(ph[:, C_HEADS:2 * C_HEADS], lb[...])
    for o_ref, val in ((qT_o, ph[:, :C_HEADS]), (fT_o, f), (kinT_o, kin), (ivT_o, ph[:, 2 * C_HEADS:3 * C_HEADS])):
        o_ref[...] = val.T
    a, u = _lru_pre(pl_ref[...][:, :C_LRU], c2_ref[...], c1_ref[...], c0_ref[...], conv_w[...], conv_b[...],
                    wa[...], ba[...], wx[...], bx[...], lam[...])
    h_o[...] = a * h0_ref[...] + u


def _step_pre(pr, ph, plru, p0, conv0, h0, prm):
    B = pr.shape[0]
    shapes = [(B, C_HEADS)] * 4 + [(C_HEADS, B)] * 10 + [(B, C_LRU)]
    return pl.pallas_call(
        _step_pre_kernel,
        out_shape=[jax.ShapeDtypeStruct(s, F32) for s in shapes],
        compiler_params=pltpu.CompilerParams(vmem_limit_bytes=VMEM_LIMIT_BYTES),
        name="step_pre",
    )(pr, ph, plru, p0, conv0[:, 0], conv0[:, 1], conv0[:, 2], h0, *prm)


def _rwkv_step_kernel(r_ref, w_ref, k_ref, v_ref, kk_ref, ka_ref, s_ref, o_ref, sT_ref):
    kk, w, ka, k, r = kk_ref[...], w_ref[...], ka_ref[...], k_ref[...], r_ref[...]

    def value_row(vi, carry):
        S = s_ref[vi]
        sa = -jnp.sum(S * kk, axis=0, keepdims=True)
        S = S * w + sa * ka + v_ref[pl.ds(vi, 1), :] * k
        sT_ref[vi] = S
        o_ref[pl.ds(vi, 1), :] = jnp.sum(S * r, axis=0, keepdims=True)
        return carry

    lax.fori_loop(0, HEAD_DIM, value_row, 0, unroll=8)


def _hgrn_step_kernel(q_ref, f_ref, kin_ref, iv_ref, s_ref, o_ref, sT_ref):
    iv = iv_ref[...]

    def key_row(ki, acc):
        S = f_ref[pl.ds(ki, 1), :] * s_ref[ki] + kin_ref[pl.ds(ki, 1), :] * iv
        sT_ref[ki] = S
        return acc + S * q_ref[pl.ds(ki, 1), :]

    o_ref[...] = lax.fori_loop(0, HEAD_DIM, key_row, jnp.zeros_like(iv), unroll=8)


def _state_step(kernel, name, vecs, states, layer, new_states):
    B = states.shape[-1]
    n_vec = len(vecs)
    vblk = pl.BlockSpec((HEAD_DIM, B), lambda h: (h, 0))
    sblk = pl.BlockSpec((None, None, HEAD_DIM, HEAD_DIM, B), lambda h: (layer, h, 0, 0, 0))

    def body(*refs):
        kernel(*refs[:n_vec + 1], *refs[n_vec + 2:])

    return pl.pallas_call(
        body,
        grid=(N_HEADS,),
        in_specs=[vblk] * n_vec + [sblk, pl.BlockSpec(memory_space=pl.ANY)],
        out_specs=[vblk, sblk],
        out_shape=[jax.ShapeDtypeStruct((C_HEADS, B), F32), jax.ShapeDtypeStruct(states.shape, F32)],
        input_output_aliases={n_vec + 1: 1},
        compiler_params=_cparams("parallel"),
        name=name,
    )(*vecs, states, new_states)


def _rwkv_post(out, r, k, v, g, r_k, ln_w, ln_b):
    inv_n = 1.0 / HEAD_DIM
    mean = _head_sum(out) * inv_n
    d = out - mean
    var = _head_sum(d * d) * inv_n
    gn = d * lax.rsqrt(var + GN_EPS) * ln_w + ln_b
    bonus = _head_sum(r * k * r_k) * v
    return (gn + bonus) * g


def _hgrn_post(o, gate, norm_g):
    o = o * lax.rsqrt(_head_sum(o * o) * (1.0 / HEAD_DIM) + NORM_EPS)
    return o * norm_g * jax.nn.silu(gate)


def _step_post_kernel(woT_ref, r_ref, k_ref, v_ref, g_ref, rk_ref, lnw_ref, lnb_ref,
                      hoT_ref, hg_ref, hng_ref, lh_ref, lg_ref, or_ref, oh_ref, ol_ref):
    or_ref[...] = _rwkv_post(woT_ref[...].T, r_ref[...], k_ref[...], v_ref[...], g_ref[...],
                             rk_ref[...], lnw_ref[...], lnb_ref[...])
    oh_ref[...] = _hgrn_post(hoT_ref[...].T, hg_ref[...], hng_ref[...])
    ol_ref[...] = lh_ref[...] * jax.nn.gelu(lg_ref[...])


def _step_post(woT, r, k, v, g, r_k, ln_w, ln_b, hoT, ph, hng, lh, plru):
    B = r.shape[0]
    full = lambda a: pl.BlockSpec(a.shape, lambda i: (0,) * a.ndim)
    return pl.pallas_call(
        _step_post_kernel,
        grid=(1,),
        in_specs=[full(woT), full(r), full(k), full(v), full(g), full(r_k), full(ln_w), full(ln_b),
                  full(hoT), pl.BlockSpec((B, C_HEADS), lambda i: (0, 3)), full(hng), full(lh),
                  pl.BlockSpec((B, C_LRU), lambda i: (0, 1))],
        out_specs=[pl.BlockSpec((B, c), lambda i: (0, 0)) for c in (C_HEADS, C_HEADS, C_LRU)],
        out_shape=[jax.ShapeDtypeStruct((B, c), F32) for c in (C_HEADS, C_HEADS, C_LRU)],
        compiler_params=_cparams("arbitrary"),
        name="step_post",
    )(woT, r, k, v, g, r_k, ln_w, ln_b, hoT, ph, hng, lh, plru)


CHUNK = 64
CHUNK_LOG2 = 6
SEQS = 4
GROUP_HEADS = 4
GROUP = GROUP_HEADS * HEAD_DIM
N_GROUPS = SEQS * N_HEADS // GROUP_HEADS

NN = ((1,), (0,))
NT = ((1,), (1,))


def _mm(a, b, dims, passes=1):
    assert passes == 1
    return lax.dot_general(a.astype(BF16), b.astype(BF16), (dims, ((), ())), preferred_element_type=F32)


def _iota2(shape):
    return lax.broadcasted_iota(jnp.int32, shape, 0), lax.broadcasted_iota(jnp.int32, shape, 1)


def _head_of(group, i):
    hw = group * GROUP_HEADS + i
    return hw // N_HEADS, hw % N_HEADS


def _group_state_load(s_ref, S_s, transpose):
    z = jnp.zeros((HEAD_DIM, HEAD_DIM), F32)
    for gi in range(N_GROUPS):
        rows = []
        for i in range(GROUP_HEADS):
            b, h = _head_of(gi, i)
            blk = s_ref[b, h].T if transpose else s_ref[b, h]
            rows.append(jnp.concatenate([blk if i2 == i else z for i2 in range(GROUP_HEADS)], axis=1))
        S_s[gi] = jnp.concatenate(rows, axis=0)


def _group_state_store(S_s, s_ref, transpose):
    for gi in range(N_GROUPS):
        Sg = S_s[gi]
        for i in range(GROUP_HEADS):
            b, h = _head_of(gi, i)
            blk = Sg[i * HEAD_DIM:(i + 1) * HEAD_DIM, i * HEAD_DIM:(i + 1) * HEAD_DIM]
            s_ref[b, h] = blk.T if transpose else blk


def _wide(x):
    return jnp.concatenate([x[b * CHUNK:(b + 1) * CHUNK] for b in range(SEQS)], axis=1)


def _tall(x):
    return jnp.concatenate([x[:, b * C_HEADS:(b + 1) * C_HEADS] for b in range(SEQS)], axis=0)


def _seq_cumsum(x):
    tr, tc = _iota2((SEQS * CHUNK, SEQS * CHUNK))
    tri = ((tr >= tc) & ((tr >> CHUNK_LOG2) == (tc >> CHUNK_LOG2))).astype(BF16)
    hi = x.astype(BF16)
    rest = x - hi.astype(F32)
    mid = rest.astype(BF16)
    lo = (rest - mid.astype(F32)).astype(BF16)
    dot = lambda part: jnp.dot(tri, part, preferred_element_type=F32)
    return dot(hi) + dot(mid) + dot(lo)


def _lane_head():
    return lax.broadcasted_iota(jnp.int32, (CHUNK, GROUP), 1) >> 6


def _pair_masks():
    lane = lax.broadcasted_iota(jnp.int32, (CHUNK, GROUP), 1)
    return lane < 2 * HEAD_DIM, (lane & HEAD_DIM) == 0


def _split_rows(x, mask):
    return jnp.concatenate([jnp.where(mask, x, 0.0), jnp.where(mask, 0.0, x)], axis=0)


def _pick(x, lane_head):
    out = x[:CHUNK]
    for i in range(1, GROUP_HEADS):
        out = jnp.where(lane_head == i, x[i * CHUNK:(i + 1) * CHUNK], out)
    return out


def _unit_lower_inverse(Ls):
    r, c = _iota2(Ls[0].shape)
    eye = (r == c).astype(F32)
    diag_blk = (r >> 4) == (c >> 4)
    mm = lambda xs, ys: [_mm(x, y, NN, 1) for x, y in zip(xs, ys)]
    Ld = [jnp.where(diag_blk, L, 0.0) for L in Ls]
    N = [jnp.where(diag_blk, 0.0, L) for L in Ls]
    L2 = mm(Ld, Ld)
    L4 = mm(L2, L2)
    P1 = mm([eye - x for x in Ld], [eye + x for x in L2])
    L8 = mm(L4, L4)
    P2 = mm([eye + x for x in L4], [eye + x for x in L8])
    Dinv = mm(P1, P2)
    M = mm(Dinv, N)
    M2 = mm(M, M)
    X = mm([eye - x for x in M], [eye + x for x in M2])
    return mm(X, Dinv)


def _seq_project(x_ref, norm_g, w_ref):
    x = x_ref[...].reshape(-1, D_MODEL)
    return jnp.dot(_rmsnorm(x, norm_g[...]).astype(BF16), w_ref[...], preferred_element_type=F32)


def _rwkv_seq_kernel(x_ref, shift_ref, norm_g, w_ref, s0_ref,
                     mu, w0, w_up, a0, a_up, g_up, k_k, k_a, r_k, ln_w, ln_b,
                     o_ref, sT_ref, carry_s, S_s):
    j = pl.program_id(1)
    C = CHUNK

    @pl.when(j == 0)
    def _():
        p0 = jnp.dot(shift_ref[...].reshape(SEQS, D_MODEL).astype(BF16), w_ref[...], preferred_element_type=F32)
        carry_s[...] = p0.reshape(SEQS, 1, RWKV_COLS)
        _group_state_load(s0_ref, S_s, transpose=False)

    pr = _seq_project(x_ref, norm_g, w_ref)
    prow = lax.broadcasted_iota(jnp.int32, pr.shape, 0)
    prev = pltpu.roll(pr, 1, 0)
    for b in range(SEQS):
        prev = jnp.where(prow == b * C, carry_s[b], prev)
        carry_s[b] = pr[(b + 1) * C - 1:(b + 1) * C, :]

    r, lw, k, v, kk, kb, g = _rwkv_pre(pr, prev, mu[...], w0[...], w_up[...], a0[...], a_up[...],
                                       g_up[...], k_k[...], k_a[...])
    cum_w, lw_w = _wide(_seq_cumsum(lw)), _wide(lw)
    r_w, k_w, v_w, kk_w, kb_w = _wide(r), _wide(k), _wide(v), _wide(kk), _wide(kb)
    cum_last = cum_w[C - 1:C, :]
    e_inv = jnp.exp(-cum_w)
    e_rel = jnp.exp(cum_last - cum_w)
    e_last = jnp.exp(cum_last)
    Kg = kk_w * jnp.exp(cum_w - lw_w)
    Rg = r_w * jnp.exp(cum_w)
    Ki, Bi = k_w * e_inv, kb_w * e_inv
    Kt, Bt = k_w * e_rel, kb_w * e_rel

    lane_head = _lane_head()
    low_pair, even_head = _pair_masks()
    by_pair = lambda x: _split_rows(x, low_pair)
    by_parity = lambda x: _split_rows(x, even_head)
    gr, gc = _iota2((GROUP, GROUP))
    same_head = (gr >> 6) == (gc >> 6)
    ar, ac = _iota2((2 * C, 2 * C))
    strict = (ar & (C - 1)) > (ac & (C - 1))
    incl = (ar & (C - 1)) >= (ac & (C - 1))
    low_half = lax.broadcasted_iota(jnp.int32, (C, 2 * C), 1) < HEAD_DIM

    groups = range(N_GROUPS)
    sl = [slice(gi * GROUP, (gi + 1) * GROUP) for gi in groups]
    P = [_mm(jnp.concatenate([by_pair(Kg[:, s]), by_pair(Rg[:, s])], axis=0),
             jnp.concatenate([by_parity(Ki[:, s]), by_parity(Bi[:, s])], axis=0), NT, 1) for s in sl]
    z2 = jnp.zeros((2 * C, 2 * C), F32)
    Akb = []
    for gi in groups:
        a2 = jnp.where(strict, P[gi][:2 * C, 2 * C:], 0.0)
        halves = [jnp.concatenate([jnp.where(low_half, a2[p * C:(p + 1) * C], 0.0),
                                   jnp.where(low_half, 0.0, a2[p * C:(p + 1) * C])], axis=0) for p in range(2)]
        Akb.append(jnp.concatenate([jnp.concatenate([halves[0], z2], axis=1),
                                    jnp.concatenate([z2, halves[1]], axis=1)], axis=0))
    Tinv = _unit_lower_inverse(Akb)
    S = [S_s[gi] for gi in groups]
    H0 = [_mm(jnp.concatenate([Kg[:, s], Rg[:, s]], axis=0), S[gi], NT, 1) for gi, s in enumerate(sl)]
    Vp = [by_parity(v_w[:, s]) for s in sl]
    AkkV = [_mm(jnp.where(strict, P[gi][:2 * C, :2 * C], 0.0), Vp[gi], NN, 1) for gi in groups]
    rhs_u = [jnp.concatenate([H0[gi][:C] + AkkV[gi][:C]] * 2 + [H0[gi][:C] + AkkV[gi][C:]] * 2, axis=0)
             for gi in groups]
    U = [_pick(_mm(Tinv[gi], rhs_u[gi], NN, 1), lane_head) for gi in groups]
    O = []
    for gi in groups:
        Y = _mm(jnp.concatenate([jnp.where(incl, P[gi][2 * C:, :2 * C], 0.0),
                                 jnp.where(incl, -P[gi][2 * C:, 2 * C:], 0.0)], axis=1),
                jnp.concatenate([Vp[gi], by_parity(U[gi])], axis=0), NN, 1)
        O.append(H0[gi][C:] + jnp.where(low_pair, Y[:C], Y[C:]))
    for gi, s in enumerate(sl):
        Z = _mm(jnp.concatenate([v_w[:, s], U[gi]], axis=0).T,
                jnp.concatenate([Kt[:, s], -Bt[:, s]], axis=0), NN, 1)
        S_s[gi] = S[gi] * e_last[:, s] + jnp.where(same_head, Z, 0.0)

    res = _rwkv_post(_tall(jnp.concatenate(O, axis=1)), r, k, v, g, r_k[...], ln_w[...], ln_b[...])
    o_ref[...] = res.reshape(SEQS, C, C_HEADS)

    @pl.when(j == pl.num_programs(1) - 1)
    def _():
        _group_state_store(S_s, sT_ref, transpose=False)


def _hgrn_seq_kernel(x_ref, norm1_g, w_ref, s0_ref, lb, norm_g, o_ref, sT_ref, S_s):
    j = pl.program_id(1)
    C = CHUNK

    @pl.when(j == 0)
    def _():
        _group_state_load(s0_ref, S_s, transpose=True)

    ph = _seq_project(x_ref, norm1_g, w_ref)
    f, kin = _hgrn_pre(ph[:, C_HEADS:2 * C_HEADS], lb[...])
    cum = _wide(_seq_cumsum(jnp.log(f)))
    q_w, kin_w, iv_w = _wide(ph[:, :C_HEADS]), _wide(kin), _wide(ph[:, 2 * C_HEADS:3 * C_HEADS])
    cum_last = cum[C - 1:C, :]
    e_last = jnp.exp(cum_last)
    qd = q_w * jnp.exp(cum)
    kd = kin_w * jnp.exp(cum_last - cum)

    low_pair, even_head = _pair_masks()
    by_pair = lambda x: _split_rows(x, low_pair)
    by_parity = lambda x: _split_rows(x, even_head)
    ar, ac = _iota2((2 * C, 2 * C))
    t_pos, s_pos = ar & (C - 1), ac & (C - 1)
    differ = jnp.where(t_pos > s_pos, t_pos ^ s_pos, 0)
    gr, gc = _iota2((GROUP, GROUP))
    same_head = (gr >> 6) == (gc >> 6)
    groups = range(N_GROUPS)
    sl = [slice(gi * GROUP, (gi + 1) * GROUP) for gi in groups]

    att = [jnp.where(t_pos == s_pos, _mm(by_pair(q_w[:, s]), by_parity(kin_w[:, s]), NT, 1), 0.0) for s in sl]
    row = lax.broadcasted_iota(jnp.int32, cum.shape, 0)
    last = cum

    def shift_rows(x, k):
        if 2 * abs(k) <= 8:
            x3 = x.reshape(C // 8, 8, x.shape[1])
            return pltpu.roll(x3, k % 8, 1).reshape(x.shape)
        return pltpu.roll(x, k % C, 0)

    for l in range(CHUNK_LOG2):
        m = 1 << l
        in_low = (row & (2 * m - 1)) < m
        d = cum - jnp.where(in_low, last, shift_rows(last, m))
        last = jnp.where(in_low, shift_rows(last, -m), last)
        e = jnp.exp(-jnp.abs(d))
        ql, kl = q_w * e, kin_w * e
        level = (differ >> l) == 1
        att = [jnp.where(level, _mm(by_pair(ql[:, s]), by_parity(kl[:, s]), NT, 1), att[gi])
               for gi, s in enumerate(sl)]

    outs = []
    for gi, s in enumerate(sl):
        S = S_s[gi]
        v = iv_w[:, s]
        intra = _mm(att[gi], by_parity(v), NN, 1)
        outs.append(jnp.where(low_pair, intra[:C], intra[C:]) + _mm(qd[:, s], S, NT, 1))
        Z = _mm(v.T, kd[:, s], NN, 1)
        S_s[gi] = S * e_last[:, s] + jnp.where(same_head, Z, 0.0)

    res = _hgrn_post(_tall(jnp.concatenate(outs, axis=1)), ph[:, 3 * C_HEADS:], norm_g[...])
    o_ref[...] = res.reshape(SEQS, C, C_HEADS)

    @pl.when(j == pl.num_programs(1) - 1)
    def _():
        _group_state_store(S_s, sT_ref, transpose=True)


def _lru_seq_kernel(x_ref, norm_g, w_ref, c0_ref, h0_ref, conv_w, conv_b, wa, ba, wx, bx, lam,
                    o_ref, hT_ref, cT_ref, xbuf_s, h_s, a_s, u_s, *, tm):
    j = pl.program_id(1)

    @pl.when(j == 0)
    def _():
        xbuf_s[5:8, :] = c0_ref[0]
        h_s[...] = h0_ref[0]

    blk = _seq_project(x_ref, norm_g, w_ref)
    xb = blk[:, :C_LRU]
    xbuf_s[8:8 + tm, :] = xb
    x1 = xbuf_s[7:7 + tm, :]
    x2 = xbuf_s[6:6 + tm, :]
    x3 = xbuf_s[5:5 + tm, :]
    a, u = _lru_pre(xb, x1, x2, x3, conv_w[...], conv_b[...], wa[...], ba[...], wx[...], bx[...], lam[...])
    xbuf_s[5:8, :] = xb[tm - 3:tm, :]

    row8 = lax.broadcasted_iota(jnp.int32, a.shape, 0) & 7
    for d in (1, 2, 4):
        ok = row8 >= d
        u = jnp.where(ok, a * pltpu.roll(u, d, 0) + u, u)
        a = jnp.where(ok, a * pltpu.roll(a, d, 0), a)
    a_s[...] = a
    u_s[...] = u

    def group(gi, h):
        i = pl.multiple_of(gi * 8, 8)
        ht = u_s[pl.ds(i, 8), :] + a_s[pl.ds(i, 8), :] * h
        o_ref[0, pl.ds(i, 8), :] = ht
        return ht[7:8, :]

    h = lax.fori_loop(0, tm // 8, group, h_s[...])
    h_s[...] = h
    o_ref[0] = o_ref[0] * jax.nn.gelu(blk[:, C_LRU:])

    @pl.when(j == pl.num_programs(1) - 1)
    def _():
        hT_ref[0] = h
        cT_ref[0] = xb[tm - (CONV_WIDTH - 1):tm, :]


def _full_spec(a):
    return pl.BlockSpec(a.shape, lambda b, j: (0,) * a.ndim)


def _seq_block(c):
    return pl.BlockSpec((SEQS, CHUNK, c), lambda b, j: (b, j, 0))


_STATE_BLOCK = pl.BlockSpec((SEQS, N_HEADS, HEAD_DIM, HEAD_DIM), lambda b, j: (b, 0, 0, 0))
_GROUP_STATE = pltpu.VMEM((N_GROUPS, GROUP, GROUP), F32)


def _rwkv_seq(x, shift, norm_g, w, s0, prm, B, T):
    o, sT = pl.pallas_call(
        _rwkv_seq_kernel,
        grid=(B // SEQS, T // CHUNK),
        in_specs=[_seq_block(D_MODEL), pl.BlockSpec((SEQS, 1, D_MODEL), lambda b, j: (b, 0, 0)),
                  _full_spec(norm_g), _full_spec(w), _STATE_BLOCK] + [_full_spec(a) for a in prm],
        out_specs=[_seq_block(C_HEADS), _STATE_BLOCK],
        out_shape=[jax.ShapeDtypeStruct((B, T, C_HEADS), F32), jax.ShapeDtypeStruct(s0.shape, F32)],
        scratch_shapes=[pltpu.VMEM((SEQS, 1, RWKV_COLS), F32), _GROUP_STATE],
        compiler_params=_cparams("parallel", "arbitrary"),
        name="rwkv_seq",
    )(x, shift.reshape(B, 1, D_MODEL), norm_g, w, s0, *prm)
    return o.reshape(B * T, C_HEADS), sT


def _hgrn_seq(x, norm1_g, w, s0, lb, norm_g, B, T):
    o, sT = pl.pallas_call(
        _hgrn_seq_kernel,
        grid=(B // SEQS, T // CHUNK),
        in_specs=[_seq_block(D_MODEL), _full_spec(norm1_g), _full_spec(w), _STATE_BLOCK,
                  _full_spec(lb), _full_spec(norm_g)],
        out_specs=[_seq_block(C_HEADS), _STATE_BLOCK],
        out_shape=[jax.ShapeDtypeStruct((B, T, C_HEADS), F32), jax.ShapeDtypeStruct(s0.shape, F32)],
        scratch_shapes=[_GROUP_STATE],
        compiler_params=_cparams("parallel", "arbitrary"),
        name="hgrn_seq",
    )(x, norm1_g, w, s0, lb, norm_g)
    return o.reshape(B * T, C_HEADS), sT


def _lru_seq(x, norm_g, w, conv0, h0, prm, B, T):
    tm = min(T, 512)
    seq = lambda c: pl.BlockSpec((1, tm, c), lambda b, j: (b, j, 0))
    hblk = pl.BlockSpec((1, 1, C_LRU), lambda b, j: (b, 0, 0))
    cblk = pl.BlockSpec((1, CONV_WIDTH - 1, C_LRU), lambda b, j: (b, 0, 0))
    o, hT, cT = pl.pallas_call(
        functools.partial(_lru_seq_kernel, tm=tm),
        grid=(B, T // tm),
        in_specs=[seq(D_MODEL), _full_spec(norm_g), _full_spec(w), cblk, hblk] + [_full_spec(a) for a in prm],
        out_specs=[seq(C_LRU), hblk, cblk],
        out_shape=[jax.ShapeDtypeStruct((B, T, C_LRU), F32), jax.ShapeDtypeStruct((B, 1, C_LRU), F32),
                   jax.ShapeDtypeStruct((B, CONV_WIDTH - 1, C_LRU), F32)],
        scratch_shapes=[pltpu.VMEM((tm + 8, C_LRU), F32), pltpu.VMEM((1, C_LRU), F32),
                        pltpu.VMEM((tm, C_LRU), F32), pltpu.VMEM((tm, C_LRU), F32)],
        compiler_params=_cparams("parallel", "arbitrary"),
        name="lru_seq",
    )(x, norm_g, w, conv0, h0.reshape(B, 1, C_LRU), *prm)
    return o.reshape(B * T, C_LRU), hT.reshape(B, C_LRU), cT


def _block_diag(w):
    out = jnp.zeros((C_LRU, C_LRU), w.dtype)
    n = C_LRU // LRU_BLOCKS
    for i in range(LRU_BLOCKS):
        out = out.at[i * n:(i + 1) * n, i * n:(i + 1) * n].set(w[i])
    return out


def _prepare_params(p):
    row = lambda a: a.reshape(1, -1)
    zpad = jnp.zeros((W_LORA, C_HEADS), F32)
    layers = []
    for l in range(DEPTH):
        pre = (row(p['mu_shift'][l]), row(p['rwkv_w0'][l]),
               jnp.concatenate([p['rwkv_w_up'][l], zpad], 0).astype(BF16),
               row(p['rwkv_a0'][l]),
               jnp.concatenate([zpad, p['rwkv_a_up'][l]], 0).astype(BF16),
               p['rwkv_g_up'][l].astype(BF16), row(p['rwkv_k_k'][l]), row(p['rwkv_k_a'][l]),
               None,
               p['lru_conv_w'][l], row(p['lru_conv_b'][l]),
               _block_diag(p['lru_wa'][l]).astype(BF16), row(p['lru_ba'][l]),
               _block_diag(p['lru_wx'][l]).astype(BF16), row(p['lru_bx'][l]), row(p['lru_lambda'][l]))
        w_in = p['w_in'][l]
        layers.append(dict(
            norm1_g=row(p['norm1_g'][l]), pre=pre,
            w_r=w_in[:, :RWKV_COLS].astype(BF16),
            w_h=w_in[:, RWKV_COLS:RWKV_COLS + HGRN_COLS].astype(BF16),
            w_l=w_in[:, RWKV_COLS + HGRN_COLS:].astype(BF16),
            r_k=row(p['rwkv_r_k'][l]), ln_w=row(p['rwkv_ln_w'][l]), ln_b=row(p['rwkv_ln_b'][l]),
            hgrn_norm_g=row(p['hgrn_norm_g'][l]),
            w_out=p['w_out'][l].astype(BF16), norm2_g=row(p['norm2_g'][l]),
            mlp_w1=p['mlp_w1'][l].astype(BF16), mlp_w2=p['mlp_w2'][l].astype(BF16)))
    return layers


def _trunk(x, wkv, shift, hgrn, lru, conv, layers, lb_all, final_g):
    B, T, _ = x.shape
    m = B * T
    assert T == 1 or (T % CHUNK == 0 and B % SEQS == 0), "sequence kernels tile (B, T) by (SEQS, CHUNK)"
    x = x.reshape(m, D_MODEL)
    n_wkv, n_shift, n_hgrn, n_lru, n_conv = [], [], [], [], []
    if T == 1:
        wkv_t, hgrn_t = jnp.transpose(wkv, (0, 2, 3, 4, 1)), jnp.transpose(hgrn, (0, 2, 3, 4, 1))
        s_wkv_t, s_hgrn_t = jnp.zeros(wkv_t.shape, F32), jnp.zeros(hgrn_t.shape, F32)
    for l, lp in enumerate(layers):
        n_shift.append(_rmsnorm_rows(x.reshape(B, T, D_MODEL)[:, -1], lp['norm1_g']))
        prm = list(lp['pre'])
        prm[8] = lb_all[l:l + 1]
        if T == 1:
            pr, ph, plru, p0 = _norm_proj(x, shift[l], lp['norm1_g'], lp['w_r'], lp['w_h'], lp['w_l'])
            (r, k, v, g, rT, wT, kT, vT, kkT, kaT, qT, fT, kinT, ivT, h_l) = _step_pre(
                pr, ph, plru, p0, conv[l], lru[l], prm)
            woT, s_wkv_t = _state_step(_rwkv_step_kernel, "rwkv_step", (rT, wT, kT, vT, kkT, kaT),
                                       wkv_t, l, s_wkv_t)
            hoT, s_hgrn_t = _state_step(_hgrn_step_kernel, "hgrn_step", (qT, fT, kinT, ivT),
                                        hgrn_t, l, s_hgrn_t)
            o_r, o_h, o_l = _step_post(woT, r, k, v, g, lp['r_k'], lp['ln_w'], lp['ln_b'],
                                       hoT, ph, lp['hgrn_norm_g'], h_l, plru)
            c_l = jnp.concatenate([conv[l][:, 1:], plru[:, None, :C_LRU]], axis=1)
        else:
            x3 = x.reshape(B, T, D_MODEL)
            o_r, S_r = _rwkv_seq(x3, shift[l], lp['norm1_g'], lp['w_r'], wkv[l],
                                 prm[:8] + [lp['r_k'], lp['ln_w'], lp['ln_b']], B, T)
            o_h, S_h = _hgrn_seq(x3, lp['norm1_g'], lp['w_h'], hgrn[l], prm[8], lp['hgrn_norm_g'], B, T)
            o_l, h_l, c_l = _lru_seq(x3, lp['norm1_g'], lp['w_l'], conv[l], lru[l], prm[9:], B, T)
            n_wkv.append(S_r)
            n_hgrn.append(S_h)
        x = _out_mlp(x, o_r, o_h, o_l, lp['w_out'], lp['norm2_g'], lp['mlp_w1'], lp['mlp_w2'])
        n_lru.append(h_l)
        n_conv.append(c_l)
    y = _rmsnorm_rows(x, final_g).reshape(B, T, D_MODEL)
    if T == 1:
        s_wkv, s_hgrn = jnp.transpose(s_wkv_t, (0, 4, 1, 2, 3)), jnp.transpose(s_hgrn_t, (0, 4, 1, 2, 3))
    else:
        s_wkv, s_hgrn = jnp.stack(n_wkv), jnp.stack(n_hgrn)
    return (y, s_wkv, jnp.stack(n_shift), s_hgrn, jnp.stack(n_lru), jnp.stack(n_conv))


def kernel(x_prompt, x_sample, state_wkv, state_shift, state_hgrn, state_lru, state_conv, norm1_g, w_in, mu_shift, rwkv_w0, rwkv_w_up, rwkv_a0, rwkv_a_up, rwkv_g_up, rwkv_k_k, rwkv_k_a, rwkv_r_k, rwkv_ln_w, rwkv_ln_b, hgrn_lb, hgrn_norm_g, lru_conv_w, lru_conv_b, lru_wa, lru_ba, lru_wx, lru_bx, lru_lambda, w_out, norm2_g, mlp_w1, mlp_w2, final_g):
    prm = dict(norm1_g=norm1_g, w_in=w_in, mu_shift=mu_shift, rwkv_w0=rwkv_w0, rwkv_w_up=rwkv_w_up,
               rwkv_a0=rwkv_a0, rwkv_a_up=rwkv_a_up, rwkv_g_up=rwkv_g_up, rwkv_k_k=rwkv_k_k,
               rwkv_k_a=rwkv_k_a, rwkv_r_k=rwkv_r_k.reshape(DEPTH, C_HEADS), rwkv_ln_w=rwkv_ln_w,
               rwkv_ln_b=rwkv_ln_b, hgrn_norm_g=hgrn_norm_g, lru_conv_w=lru_conv_w,
               lru_conv_b=lru_conv_b, lru_wa=lru_wa, lru_ba=lru_ba, lru_wx=lru_wx, lru_bx=lru_bx,
               lru_lambda=lru_lambda, w_out=w_out, norm2_g=norm2_g, mlp_w1=mlp_w1, mlp_w2=mlp_w2)
    layers = _prepare_params(prm)
    lb_all = _hgrn_lower_bounds(hgrn_lb)
    fg = final_g.reshape(1, D_MODEL)
    Bp = x_prompt.shape[0]
    dt = x_prompt.dtype
    z_wkv = [jnp.zeros((Bp, N_HEADS, HEAD_DIM, HEAD_DIM), dt)] * DEPTH
    z_shift = [jnp.zeros((Bp, D_MODEL), dt)] * DEPTH
    z_hgrn = [jnp.zeros((Bp, N_HEADS, HEAD_DIM, HEAD_DIM), dt)] * DEPTH
    z_lru = [jnp.zeros((Bp, C_LRU), dt)] * DEPTH
    z_conv = [jnp.zeros((Bp, CONV_WIDTH - 1, C_LRU), dt)] * DEPTH
    y_p, p_wkv, p_shift, p_hgrn, p_lru, p_conv = _trunk(x_prompt, z_wkv, z_shift, z_hgrn, z_lru, z_conv,
                                                         layers, lb_all, fg)
    y_s, s_wkv, s_shift, s_hgrn, s_lru, s_conv = _trunk(x_sample, state_wkv, state_shift, state_hgrn,
                                                         state_lru, state_conv, layers, lb_all, fg)
    return (y_p, y_s, p_wkv, p_shift, p_hgrn, p_lru, p_conv, s_wkv, s_shift, s_hgrn, s_lru, s_conv)
```

```python
import functools
import math

import jax
import jax.numpy as jnp
from jax import lax
from jax.experimental import pallas as pl
from jax.experimental.pallas import tpu as pltpu

F32 = jnp.float32
BF16 = jnp.bfloat16

D_MODEL = 1024
DEPTH = 4
HEAD_DIM = 64
N_HEADS = 6
C_HEADS = N_HEADS * HEAD_DIM
C_LRU = 256
LRU_BLOCKS = 4
CONV_WIDTH = 4
LRU_C = 8.0
W_LORA = 64
A_LORA = 64
G_LORA = 128
RWKV_COLS = 3 * C_HEADS + W_LORA + A_LORA + G_LORA
HGRN_COLS = 4 * C_HEADS
LRU_COLS = 2 * C_LRU
C_IN = RWKV_COLS + HGRN_COLS + LRU_COLS
D_FF = 4 * D_MODEL
NORM_EPS = 1e-6
GN_EPS = 64e-5
DECAY_SCALE = math.exp(-0.5)

LANES = 128
VMEM_LIMIT_BYTES = 48 * 1024 * 1024


def _cparams(*sem):
    return pltpu.CompilerParams(dimension_semantics=sem, vmem_limit_bytes=VMEM_LIMIT_BYTES)


def _dot(a, b):
    return jnp.dot(a.astype(BF16), b.astype(BF16), preferred_element_type=F32)


def _rmsnorm(x, g):
    return x * lax.rsqrt(jnp.mean(x * x, axis=-1, keepdims=True) + NORM_EPS) * g


def _softplus(x):
    return jnp.maximum(x, 0.0) + jnp.log1p(jnp.exp(-jnp.abs(x)))


def _head_sum(x):
    m = x.shape[0]
    lo = lax.broadcasted_iota(jnp.int32, (m, LANES), 1) < HEAD_DIM
    outs = []
    for p in range(x.shape[1] // LANES):
        xp = x[:, p * LANES:(p + 1) * LANES]
        s_lo = jnp.sum(jnp.where(lo, xp, 0.0), axis=1, keepdims=True)
        s_hi = jnp.sum(jnp.where(lo, 0.0, xp), axis=1, keepdims=True)
        outs.append(jnp.where(lo, s_lo, s_hi))
    return jnp.concatenate(outs, axis=1)


def _norm_proj_kernel(x_ref, shift_ref, g_ref, wr_ref, wh_ref, wl_ref, pr_ref, ph_ref, pl_ref, p0_ref):
    xn = _rmsnorm(x_ref[...], g_ref[...]).astype(BF16)
    pr_ref[...] = jnp.dot(xn, wr_ref[...], preferred_element_type=F32)
    ph_ref[...] = jnp.dot(xn, wh_ref[...], preferred_element_type=F32)
    pl_ref[...] = jnp.dot(xn, wl_ref[...], preferred_element_type=F32)
    p0_ref[...] = jnp.dot(shift_ref[...].astype(BF16), wr_ref[...], preferred_element_type=F32)


def _norm_proj(x, shift, g, w_r, w_h, w_l):
    m = x.shape[0]
    return pl.pallas_call(
        _norm_proj_kernel,
        out_shape=[jax.ShapeDtypeStruct((m, c), F32) for c in (RWKV_COLS, HGRN_COLS, LRU_COLS, RWKV_COLS)],
        compiler_params=pltpu.CompilerParams(vmem_limit_bytes=VMEM_LIMIT_BYTES),
        name="norm_proj",
    )(x, shift, g, w_r, w_h, w_l)


def _out_mlp_kernel(x_ref, or_ref, oh_ref, ol_ref, wo_ref, g2_ref, w1_ref, w2_ref, gout_ref, o_ref, *, norm_out):
    mix = jnp.concatenate([or_ref[...], oh_ref[...], ol_ref[...]], axis=1)
    x1 = x_ref[...] + jnp.dot(mix.astype(BF16), wo_ref[...], preferred_element_type=F32)
    xn = _rmsnorm(x1, g2_ref[...]).astype(BF16)
    h = jnp.square(jnp.maximum(jnp.dot(xn, w1_ref[...], preferred_element_type=F32), 0.0))
    x2 = x1 + jnp.dot(h.astype(BF16), w2_ref[...], preferred_element_type=F32)
    o_ref[...] = _rmsnorm(x2, gout_ref[...]) if norm_out else x2


def _out_mlp(x, o_r, o_h, o_l, wo, g2, w1, w2, g_out, norm_out):
    m = x.shape[0]
    tm = min(m, 256)
    row = lambda c: pl.BlockSpec((tm, c), lambda i: (i, 0))
    resident = lambda a: pl.BlockSpec(a.shape, lambda i: (0,) * a.ndim, pipeline_mode=pl.Buffered(1))
    return pl.pallas_call(
        functools.partial(_out_mlp_kernel, norm_out=norm_out),
        grid=(m // tm,),
        in_specs=[row(D_MODEL), row(C_HEADS), row(C_HEADS), row(C_LRU),
                  resident(wo), resident(g2), resident(w1), resident(w2), resident(g_out)],
        out_specs=row(D_MODEL),
        out_shape=jax.ShapeDtypeStruct((m, D_MODEL), F32),
        compiler_params=_cparams("parallel"),
        name="out_mlp",
    )(x, o_r, o_h, o_l, wo, g2, w1, w2, g_out)


def _rmsnorm_rows_kernel(x_ref, g_ref, o_ref):
    o_ref[...] = _rmsnorm(x_ref[...], g_ref[...])


def _rmsnorm_rows(x, g):
    m = x.shape[0]
    tm = min(m, 512)
    row = pl.BlockSpec((tm, D_MODEL), lambda i: (i, 0))
    return pl.pallas_call(
        _rmsnorm_rows_kernel,
        grid=(m // tm,),
        in_specs=[row, pl.BlockSpec((1, D_MODEL), lambda i: (0, 0))],
        out_specs=row,
        out_shape=jax.ShapeDtypeStruct((m, D_MODEL), F32),
        compiler_params=_cparams("parallel"),
        name="rmsnorm_rows",
    )(x, g)


def _lb_kernel(p_ref, o_ref):
    p = p_ref[...]
    e = jnp.exp(p - jnp.max(p, axis=0, keepdims=True))
    s = e / jnp.sum(e, axis=0, keepdims=True)
    acc = jnp.zeros_like(s[0:1])
    for l in range(DEPTH):
        if l > 0:
            acc = acc + s[l:l + 1]
        o_ref[l:l + 1, :] = acc


def _hgrn_lower_bounds(hgrn_lb):
    return pl.pallas_call(_lb_kernel, out_shape=jax.ShapeDtypeStruct(hgrn_lb.shape, F32),
                          name="hgrn_lower_bounds")(hgrn_lb)


def _rwkv_pre(pr, prev, mu, w0, w_up, a0, a_up, g_up, k_k, k_a):
    c = C_HEADS
    pm = pr + mu * (prev - pr)
    r = pm[:, :c]
    k0 = pm[:, c:2 * c]
    v = pm[:, 2 * c:3 * c]
    xwa = pm[:, 3 * c:3 * c + W_LORA + A_LORA]
    xg = pm[:, 3 * c + W_LORA + A_LORA:]
    log_decay = -DECAY_SCALE * jax.nn.sigmoid(w0 + _dot(jnp.tanh(xwa), w_up))
    a = jax.nn.sigmoid(a0 + _dot(xwa, a_up))
    g = _dot(jax.nn.sigmoid(xg), g_up)
    kk = k0 * k_k
    kk = kk * lax.rsqrt(jnp.maximum(_head_sum(kk * kk), 1e-24))
    k = k0 * (1.0 + (a - 1.0) * k_a)
    return r, log_decay, k, v, kk, kk * a, g


def _hgrn_pre(fpre, lb):
    sg = jax.nn.sigmoid(fpre)
    f = lb + (1.0 - lb) * sg
    kin = (1.0 - lb) * jax.nn.sigmoid(-fpre)
    return f, kin


def _lru_pre(xb, x1, x2, x3, conv_w, conv_b, wa, ba, wx, bx, lam):
    xc = conv_b + (((x3 * conv_w[0:1] + x2 * conv_w[1:2]) + x1 * conv_w[2:3]) + xb * conv_w[3:4])
    r = jax.nn.sigmoid(_dot(xc, wa) + ba)
    i = jax.nn.sigmoid(_dot(xc, wx) + bx)
    log_a = -LRU_C * r * _softplus(-lam)
    a = jnp.exp(log_a)
    one_minus_a2 = -jnp.tanh(log_a) * (a * a + 1.0)
    u = xc * i * jnp.sqrt(jnp.maximum(one_minus_a2, 1e-12))
    return a, u


_N_PRE_PARAMS = 16


def _step_pre_kernel(*refs):
    pr_ref, ph_ref, pl_ref, p0_ref, c0_ref, c1_ref, c2_ref, h0_ref = refs[:8]
    (mu, w0, w_up, a0, a_up, g_up, k_k, k_a, lb, conv_w, conv_b, wa, ba, wx, bx, lam) = refs[8:8 + _N_PRE_PARAMS]
    (r_o, k_o, v_o, g_o, rT_o, wT_o, kT_o, vT_o, kkT_o, kaT_o,
     qT_o, fT_o, kinT_o, ivT_o, h_o) = refs[8 + _N_PRE_PARAMS:]
    r, lw, k, v, kk, ka, g = _rwkv_pre(pr_ref[...], p0_ref[...], mu[...], w0[...], w_up[...], a0[...],
                                       a_up[...], g_up[...], k_k[...], k_a[...])
    r_o[...], k_o[...], v_o[...], g_o[...] = r, k, v, g
    for o_ref, val in ((rT_o, r), (wT_o, jnp.exp(lw)), (kT_o, k), (vT_o, v), (kkT_o, kk), (kaT_o, ka)):
        o_ref[...] = val.T
    ph = ph_ref[...]
    f, kin = _hgrn_pre(ph[:, C_HEADS:2 * C_HEADS], lb[...])
    for o_ref, val in ((qT_o, ph[:, :C_HEADS]), (fT_o, f), (kinT_o, kin), (ivT_o, ph[:, 2 * C_HEADS:3 * C_HEADS])):
        o_ref[...] = val.T
    a, u = _lru_pre(pl_ref[...][:, :C_LRU], c2_ref[...], c1_ref[...], c0_ref[...], conv_w[...], conv_b[...],
                    wa[...], ba[...], wx[...], bx[...], lam[...])
    h_o[...] = a * h0_ref[...] + u


def _step_pre(pr, ph, plru, p0, conv0, h0, prm):
    B = pr.shape[0]
    shapes = [(B, C_HEADS)] * 4 + [(C_HEADS, B)] * 10 + [(B, C_LRU)]
    return pl.pallas_call(
        _step_pre_kernel,
        out_shape=[jax.ShapeDtypeStruct(s, F32) for s in shapes],
        compiler_params=pltpu.CompilerParams(vmem_limit_bytes=VMEM_LIMIT_BYTES),
        name="step_pre",
    )(pr, ph, plru, p0, conv0[:, 0], conv0[:, 1], conv0[:, 2], h0, *prm)


def _rwkv_step_kernel(r_ref, w_ref, k_ref, v_ref, kk_ref, ka_ref, s_ref, o_ref, sT_ref):
    kk, w, ka, k, r = kk_ref[...], w_ref[...], ka_ref[...], k_ref[...], r_ref[...]

    def value_row(vi, carry):
        S = s_ref[vi]
        sa = -jnp.sum(S * kk, axis=0, keepdims=True)
        S = S * w + sa * ka + v_ref[pl.ds(vi, 1), :] * k
        sT_ref[vi] = S
        o_ref[pl.ds(vi, 1), :] = jnp.sum(S * r, axis=0, keepdims=True)
        return carry

    lax.fori_loop(0, HEAD_DIM, value_row, 0, unroll=8)


def _hgrn_step_kernel(q_ref, f_ref, kin_ref, iv_ref, s_ref, o_ref, sT_ref):
    iv = iv_ref[...]

    def key_row(ki, acc):
        S = f_ref[pl.ds(ki, 1), :] * s_ref[ki] + kin_ref[pl.ds(ki, 1), :] * iv
        sT_ref[ki] = S
        return acc + S * q_ref[pl.ds(ki, 1), :]

    o_ref[...] = lax.fori_loop(0, HEAD_DIM, key_row, jnp.zeros_like(iv), unroll=8)


def _state_step(kernel, name, vecs, states, layer, new_states):
    B = states.shape[-1]
    n_vec = len(vecs)
    vblk = pl.BlockSpec((HEAD_DIM, B), lambda h: (h, 0))
    sblk = pl.BlockSpec((None, None, HEAD_DIM, HEAD_DIM, B), lambda h: (layer, h, 0, 0, 0))

    def body(*refs):
        kernel(*refs[:n_vec + 1], *refs[n_vec + 2:])

    return pl.pallas_call(
        body,
        grid=(N_HEADS,),
        in_specs=[vblk] * n_vec + [sblk, pl.BlockSpec(memory_space=pl.ANY)],
        out_specs=[vblk, sblk],
        out_shape=[jax.ShapeDtypeStruct((C_HEADS, B), F32), jax.ShapeDtypeStruct(states.shape, F32)],
        input_output_aliases={n_vec + 1: 1},
        compiler_params=_cparams("parallel"),
        name=name,
    )(*vecs, states, new_states)


def _rwkv_post(out, r, k, v, g, r_k, ln_w, ln_b):
    inv_n = 1.0 / HEAD_DIM
    mean = _head_sum(out) * inv_n
    d = out - mean
    var = _head_sum(d * d) * inv_n
    gn = d * lax.rsqrt(var + GN_EPS) * ln_w + ln_b
    bonus = _head_sum(r * k * r_k) * v
    return (gn + bonus) * g


def _hgrn_post(o, gate, norm_g):
    o = o * lax.rsqrt(_head_sum(o * o) * (1.0 / HEAD_DIM) + NORM_EPS)
    return o * norm_g * jax.nn.silu(gate)


def _step_post_kernel(woT_ref, r_ref, k_ref, v_ref, g_ref, rk_ref, lnw_ref, lnb_ref,
                      hoT_ref, hg_ref, hng_ref, lh_ref, lg_ref, or_ref, oh_ref, ol_ref):
    or_ref[...] = _rwkv_post(woT_ref[...].T, r_ref[...], k_ref[...], v_ref[...], g_ref[...],
                             rk_ref[...], lnw_ref[...], lnb_ref[...])
    oh_ref[...] = _hgrn_post(hoT_ref[...].T, hg_ref[...], hng_ref[...])
    ol_ref[...] = lh_ref[...] * jax.nn.gelu(lg_ref[...])


def _step_post(woT, r, k, v, g, r_k, ln_w, ln_b, hoT, ph, hng, lh, plru):
    B = r.shape[0]
    full = lambda a: pl.BlockSpec(a.shape, lambda i: (0,) * a.ndim)
    return pl.pallas_call(
        _step_post_kernel,
        grid=(1,),
        in_specs=[full(woT), full(r), full(k), full(v), full(g), full(r_k), full(ln_w), full(ln_b),
                  full(hoT), pl.BlockSpec((B, C_HEADS), lambda i: (0, 3)), full(hng), full(lh),
                  pl.BlockSpec((B, C_LRU), lambda i: (0, 1))],
        out_specs=[pl.BlockSpec((B, c), lambda i: (0, 0)) for c in (C_HEADS, C_HEADS, C_LRU)],
        out_shape=[jax.ShapeDtypeStruct((B, c), F32) for c in (C_HEADS, C_HEADS, C_LRU)],
        compiler_params=_cparams("arbitrary"),
        name="step_post",
    )(woT, r, k, v, g, r_k, ln_w, ln_b, hoT, ph, hng, lh, plru)


CHUNK = 64
CHUNK_LOG2 = 6
SEQS = 4
GROUP_HEADS = 4
GROUP = GROUP_HEADS * HEAD_DIM
N_GROUPS = SEQS * N_HEADS // GROUP_HEADS

NN = ((1,), (0,))
NT = ((1,), (1,))


def _mm(a, b, dims, passes=1):
    assert passes == 1
    return lax.dot_general(a.astype(BF16), b.astype(BF16), (dims, ((), ())), preferred_element_type=F32)


def _iota2(shape):
    return lax.broadcasted_iota(jnp.int32, shape, 0), lax.broadcasted_iota(jnp.int32, shape, 1)


def _head_of(group, i):
    hw = group * GROUP_HEADS + i
    return hw // N_HEADS, hw % N_HEADS


def _group_state_load(s_ref, S_s, transpose):
    z = jnp.zeros((HEAD_DIM, HEAD_DIM), F32)
    for gi in range(N_GROUPS):
        rows = []
        for i in range(GROUP_HEADS):
            b, h = _head_of(gi, i)
            blk = s_ref[b, h].T if transpose else s_ref[b, h]
            rows.append(jnp.concatenate([blk if i2 == i else z for i2 in range(GROUP_HEADS)], axis=1))
        S_s[gi] = jnp.concatenate(rows, axis=0)


def _group_state_store(S_s, s_ref, transpose):
    for gi in range(N_GROUPS):
        Sg = S_s[gi]
        for i in range(GROUP_HEADS):
            b, h = _head_of(gi, i)
            blk = Sg[i * HEAD_DIM:(i + 1) * HEAD_DIM, i * HEAD_DIM:(i + 1) * HEAD_DIM]
            s_ref[b, h] = blk.T if transpose else blk


def _wide(x):
    return jnp.concatenate([x[b * CHUNK:(b + 1) * CHUNK] for b in range(SEQS)], axis=1)


def _tall(x):
    return jnp.concatenate([x[:, b * C_HEADS:(b + 1) * C_HEADS] for b in range(SEQS)], axis=0)


def _seq_cumsum(x):
    tr, tc = _iota2((SEQS * CHUNK, SEQS * CHUNK))
    tri = ((tr >= tc) & ((tr >> CHUNK_LOG2) == (tc >> CHUNK_LOG2))).astype(BF16)
    hi = x.astype(BF16)
    rest = x - hi.astype(F32)
    mid = rest.astype(BF16)
    lo = (rest - mid.astype(F32)).astype(BF16)
    dot = lambda part: jnp.dot(tri, part, preferred_element_type=F32)
    return dot(hi) + dot(mid) + dot(lo)


def _lane_head():
    return lax.broadcasted_iota(jnp.int32, (CHUNK, GROUP), 1) >> 6


def _pair_masks():
    lane = lax.broadcasted_iota(jnp.int32, (CHUNK, GROUP), 1)
    return lane < 2 * HEAD_DIM, (lane & HEAD_DIM) == 0


def _split_rows(x, mask):
    return jnp.concatenate([jnp.where(mask, x, 0.0), jnp.where(mask, 0.0, x)], axis=0)


def _pick(x, lane_head):
    out = x[:CHUNK]
    for i in range(1, GROUP_HEADS):
        out = jnp.where(lane_head == i, x[i * CHUNK:(i + 1) * CHUNK], out)
    return out


def _unit_lower_inverse(Ls):
    r, c = _iota2(Ls[0].shape)
    eye = (r == c).astype(F32)
    diag_blk = (r >> 4) == (c >> 4)
    mm = lambda xs, ys: [_mm(x, y, NN, 1) for x, y in zip(xs, ys)]
    Ld = [jnp.where(diag_blk, L, 0.0) for L in Ls]
    N = [jnp.where(diag_blk, 0.0, L) for L in Ls]
    L2 = mm(Ld, Ld)
    L4 = mm(L2, L2)
    P1 = mm([eye - x for x in Ld], [eye + x for x in L2])
    L8 = mm(L4, L4)
    P2 = mm([eye + x for x in L4], [eye + x for x in L8])
    Dinv = mm(P1, P2)
    M = mm(Dinv, N)
    M2 = mm(M, M)
    X = mm([eye - x for x in M], [eye + x for x in M2])
    return mm(X, Dinv)


def _seq_project(x_ref, norm_g, w_ref):
    x = x_ref[...].reshape(-1, D_MODEL)
    return jnp.dot(_rmsnorm(x, norm_g[...]).astype(BF16), w_ref[...], preferred_element_type=F32)


def _rwkv_seq_kernel(x_ref, shift_ref, norm_g, w_ref, s0_ref,
                     mu, w0, w_up, a0, a_up, g_up, k_k, k_a, r_k, ln_w, ln_b,
                     o_ref, sT_ref, carry_s, S_s):
    j = pl.program_id(1)
    C = CHUNK

    @pl.when(j == 0)
    def _():
        p0 = jnp.dot(shift_ref[...].reshape(SEQS, D_MODEL).astype(BF16), w_ref[...], preferred_element_type=F32)
        carry_s[...] = p0.reshape(SEQS, 1, RWKV_COLS)
        _group_state_load(s0_ref, S_s, transpose=False)

    pr = _seq_project(x_ref, norm_g, w_ref)
    prow = lax.broadcasted_iota(jnp.int32, pr.shape, 0)
    prev = pltpu.roll(pr, 1, 0)
    for b in range(SEQS):
        prev = jnp.where(prow == b * C, carry_s[b], prev)
        carry_s[b] = pr[(b + 1) * C - 1:(b + 1) * C, :]

    r, lw, k, v, kk, kb, g = _rwkv_pre(pr, prev, mu[...], w0[...], w_up[...], a0[...], a_up[...],
                                       g_up[...], k_k[...], k_a[...])
    cum_w, lw_w = _wide(_seq_cumsum(lw)), _wide(lw)
    r_w, k_w, v_w, kk_w, kb_w = _wide(r), _wide(k), _wide(v), _wide(kk), _wide(kb)
    cum_last = cum_w[C - 1:C, :]
    e_inv = jnp.exp(-cum_w)
    e_rel = jnp.exp(cum_last - cum_w)
    e_last = jnp.exp(cum_last)
    Kg = kk_w * jnp.exp(cum_w - lw_w)
    Rg = r_w * jnp.exp(cum_w)
    Ki, Bi = k_w * e_inv, kb_w * e_inv
    Kt, Bt = k_w * e_rel, kb_w * e_rel

    lane_head = _lane_head()
    low_pair, even_head = _pair_masks()
    by_pair = lambda x: _split_rows(x, low_pair)
    by_parity = lambda x: _split_rows(x, even_head)
    gr, gc = _iota2((GROUP, GROUP))
    same_head = (gr >> 6) == (gc >> 6)
    ar, ac = _iota2((2 * C, 2 * C))
    strict = (ar & (C - 1)) > (ac & (C - 1))
    incl = (ar & (C - 1)) >= (ac & (C - 1))
    low_half = lax.broadcasted_iota(jnp.int32, (C, 2 * C), 1) < HEAD_DIM

    groups = range(N_GROUPS)
    sl = [slice(gi * GROUP, (gi + 1) * GROUP) for gi in groups]
    P = [_mm(jnp.concatenate([by_pair(Kg[:, s]), by_pair(Rg[:, s])], axis=0),
             jnp.concatenate([by_parity(Ki[:, s]), by_parity(Bi[:, s])], axis=0), NT, 1) for s in sl]
    z2 = jnp.zeros((2 * C, 2 * C), F32)
    Akb = []
    for gi in groups:
        a2 = jnp.where(strict, P[gi][:2 * C, 2 * C:], 0.0)
        halves = [jnp.concatenate([jnp.where(low_half, a2[p * C:(p + 1) * C], 0.0),
                                   jnp.where(low_half, 0.0, a2[p * C:(p + 1) * C])], axis=0) for p in range(2)]
        Akb.append(jnp.concatenate([jnp.concatenate([halves[0], z2], axis=1),
                                    jnp.concatenate([z2, halves[1]], axis=1)], axis=0))
    Tinv = _unit_lower_inverse(Akb)
    S = [S_s[gi] for gi in groups]
    H0 = [_mm(jnp.concatenate([Kg[:, s], Rg[:, s]], axis=0), S[gi], NT, 1) for gi, s in enumerate(sl)]
    Vp = [by_parity(v_w[:, s]) for s in sl]
    AkkV = [_mm(jnp.where(strict, P[gi][:2 * C, :2 * C], 0.0), Vp[gi], NN, 1) for gi in groups]
    rhs_u = [jnp.concatenate([H0[gi][:C] + AkkV[gi][:C]] * 2 + [H0[gi][:C] + AkkV[gi][C:]] * 2, axis=0)
             for gi in groups]
    U = [_pick(_mm(Tinv[gi], rhs_u[gi], NN, 1), lane_head) for gi in groups]
    O = []
    for gi in groups:
        Y = _mm(jnp.concatenate([jnp.where(incl, P[gi][2 * C:, :2 * C], 0.0),
                                 jnp.where(incl, -P[gi][2 * C:, 2 * C:], 0.0)], axis=1),
                jnp.concatenate([Vp[gi], by_parity(U[gi])], axis=0), NN, 1)
        O.append(H0[gi][C:] + jnp.where(low_pair, Y[:C], Y[C:]))
    for gi, s in enumerate(sl):
        Z = _mm(jnp.concatenate([v_w[:, s], U[gi]], axis=0).T,
                jnp.concatenate([Kt[:, s], -Bt[:, s]], axis=0), NN, 1)
        S_s[gi] = S[gi] * e_last[:, s] + jnp.where(same_head, Z, 0.0)

    res = _rwkv_post(_tall(jnp.concatenate(O, axis=1)), r, k, v, g, r_k[...], ln_w[...], ln_b[...])
    o_ref[...] = res.reshape(SEQS, C, C_HEADS)

    @pl.when(j == pl.num_programs(1) - 1)
    def _():
        _group_state_store(S_s, sT_ref, transpose=False)


def _hgrn_seq_kernel(x_ref, norm1_g, w_ref, s0_ref, lb, norm_g, o_ref, sT_ref, S_s):
    j = pl.program_id(1)
    C = CHUNK

    @pl.when(j == 0)
    def _():
        _group_state_load(s0_ref, S_s, transpose=True)

    ph = _seq_project(x_ref, norm1_g, w_ref)
    f, kin = _hgrn_pre(ph[:, C_HEADS:2 * C_HEADS], lb[...])
    cum = _wide(_seq_cumsum(jnp.log(f)))
    q_w, kin_w, iv_w = _wide(ph[:, :C_HEADS]), _wide(kin), _wide(ph[:, 2 * C_HEADS:3 * C_HEADS])
    cum_last = cum[C - 1:C, :]
    e_last = jnp.exp(cum_last)
    qd = q_w * jnp.exp(cum)
    kd = kin_w * jnp.exp(cum_last - cum)

    low_pair, even_head = _pair_masks()
    by_pair = lambda x: _split_rows(x, low_pair)
    by_parity = lambda x: _split_rows(x, even_head)
    ar, ac = _iota2((2 * C, 2 * C))
    t_pos, s_pos = ar & (C - 1), ac & (C - 1)
    differ = jnp.where(t_pos > s_pos, t_pos ^ s_pos, 0)
    gr, gc = _iota2((GROUP, GROUP))
    same_head = (gr >> 6) == (gc >> 6)
    groups = range(N_GROUPS)
    sl = [slice(gi * GROUP, (gi + 1) * GROUP) for gi in groups]

    att = [jnp.where(t_pos == s_pos, _mm(by_pair(q_w[:, s]), by_parity(kin_w[:, s]), NT, 1), 0.0) for s in sl]
    row = lax.broadcasted_iota(jnp.int32, cum.shape, 0)
    last = cum

    def shift_rows(x, k):
        if 2 * abs(k) <= 8:
            x3 = x.reshape(C // 8, 8, x.shape[1])
            return pltpu.roll(x3, k % 8, 1).reshape(x.shape)
        return pltpu.roll(x, k % C, 0)

    for l in range(CHUNK_LOG2):
        m = 1 << l
        in_low = (row & (2 * m - 1)) < m
        d = cum - jnp.where(in_low, last, shift_rows(last, m))
        last = jnp.where(in_low, shift_rows(last, -m), last)
        e = jnp.exp(-jnp.abs(d))
        ql, kl = q_w * e, kin_w * e
        level = (differ >> l) == 1
        att = [jnp.where(level, _mm(by_pair(ql[:, s]), by_parity(kl[:, s]), NT, 1), att[gi])
               for gi, s in enumerate(sl)]

    outs = []
    for gi, s in enumerate(sl):
        S = S_s[gi]
        v = iv_w[:, s]
        intra = _mm(att[gi], by_parity(v), NN, 1)
        outs.append(jnp.where(low_pair, intra[:C], intra[C:]) + _mm(qd[:, s], S, NT, 1))
        Z = _mm(v.T, kd[:, s], NN, 1)
        S_s[gi] = S * e_last[:, s] + jnp.where(same_head, Z, 0.0)

    res = _hgrn_post(_tall(jnp.concatenate(outs, axis=1)), ph[:, 3 * C_HEADS:], norm_g[...])
    o_ref[...] = res.reshape(SEQS, C, C_HEADS)

    @pl.when(j == pl.num_programs(1) - 1)
    def _():
        _group_state_store(S_s, sT_ref, transpose=True)


def _lru_seq_kernel(x_ref, norm_g, w_ref, c0_ref, h0_ref, conv_w, conv_b, wa, ba, wx, bx, lam,
                    o_ref, hT_ref, cT_ref, xbuf_s, h_s, a_s, u_s, *, tm):
    j = pl.program_id(1)

    @pl.when(j == 0)
    def _():
        xbuf_s[5:8, :] = c0_ref[0]
        h_s[...] = h0_ref[0]

    blk = _seq_project(x_ref, norm_g, w_ref)
    xb = blk[:, :C_LRU]
    xbuf_s[8:8 + tm, :] = xb
    x1 = xbuf_s[7:7 + tm, :]
    x2 = xbuf_s[6:6 + tm, :]
    x3 = xbuf_s[5:5 + tm, :]
    a, u = _lru_pre(xb, x1, x2, x3, conv_w[...], conv_b[...], wa[...], ba[...], wx[...], bx[...], lam[...])
    xbuf_s[5:8, :] = xb[tm - 3:tm, :]

    row8 = lax.broadcasted_iota(jnp.int32, a.shape, 0) & 7
    for d in (1, 2, 4):
        ok = row8 >= d
        u = jnp.where(ok, a * pltpu.roll(u, d, 0) + u, u)
        a = jnp.where(ok, a * pltpu.roll(a, d, 0), a)
    a_s[...] = a
    u_s[...] = u

    def group(gi, h):
        i = pl.multiple_of(gi * 8, 8)
        ht = u_s[pl.ds(i, 8), :] + a_s[pl.ds(i, 8), :] * h
        o_ref[0, pl.ds(i, 8), :] = ht
        return ht[7:8, :]

    h = lax.fori_loop(0, tm // 8, group, h_s[...])
    h_s[...] = h
    o_ref[0] = o_ref[0] * jax.nn.gelu(blk[:, C_LRU:])

    @pl.when(j == pl.num_programs(1) - 1)
    def _():
        hT_ref[0] = h
        cT_ref[0] = xb[tm - (CONV_WIDTH - 1):tm, :]


def _full_spec(a):
    return pl.BlockSpec(a.shape, lambda b, j: (0,) * a.ndim)


def _seq_block(c):
    return pl.BlockSpec((SEQS, CHUNK, c), lambda b, j: (b, j, 0))


_STATE_BLOCK = pl.BlockSpec((SEQS, N_HEADS, HEAD_DIM, HEAD_DIM), lambda b, j: (b, 0, 0, 0))
_GROUP_STATE = pltpu.VMEM((N_GROUPS, GROUP, GROUP), F32)


def _rwkv_seq(x, shift, norm_g, w, s0, prm, B, T):
    o, sT = pl.pallas_call(
        _rwkv_seq_kernel,
        grid=(B // SEQS, T // CHUNK),
        in_specs=[_seq_block(D_MODEL), pl.BlockSpec((SEQS, 1, D_MODEL), lambda b, j: (b, 0, 0)),
                  _full_spec(norm_g), _full_spec(w), _STATE_BLOCK] + [_full_spec(a) for a in prm],
        out_specs=[_seq_block(C_HEADS), _STATE_BLOCK],
        out_shape=[jax.ShapeDtypeStruct((B, T, C_HEADS), F32), jax.ShapeDtypeStruct(s0.shape, F32)],
        scratch_shapes=[pltpu.VMEM((SEQS, 1, RWKV_COLS), F32), _GROUP_STATE],
        compiler_params=_cparams("parallel", "arbitrary"),
        name="rwkv_seq",
    )(x, shift.reshape(B, 1, D_MODEL), norm_g, w, s0, *prm)
    return o.reshape(B * T, C_HEADS), sT


def _hgrn_seq(x, norm1_g, w, s0, lb, norm_g, B, T):
    o, sT = pl.pallas_call(
        _hgrn_seq_kernel,
        grid=(B // SEQS, T // CHUNK),
        in_specs=[_seq_block(D_MODEL), _full_spec(norm1_g), _full_spec(w), _STATE_BLOCK,
                  _full_spec(lb), _full_spec(norm_g)],
        out_specs=[_seq_block(C_HEADS), _STATE_BLOCK],
        out_shape=[jax.ShapeDtypeStruct((B, T, C_HEADS), F32), jax.ShapeDtypeStruct(s0.shape, F32)],
        scratch_shapes=[_GROUP_STATE],
        compiler_params=_cparams("parallel", "arbitrary"),
        name="hgrn_seq",
    )(x, norm1_g, w, s0, lb, norm_g)
    return o.reshape(B * T, C_HEADS), sT


def _lru_seq(x, norm_g, w, conv0, h0, prm, B, T):
    tm = min(T, 512)
    seq = lambda c: pl.BlockSpec((1, tm, c), lambda b, j: (b, j, 0))
    hblk = pl.BlockSpec((1, 1, C_LRU), lambda b, j: (b, 0, 0))
    cblk = pl.BlockSpec((1, CONV_WIDTH - 1, C_LRU), lambda b, j: (b, 0, 0))
    o, hT, cT = pl.pallas_call(
        functools.partial(_lru_seq_kernel, tm=tm),
        grid=(B, T // tm),
        in_specs=[seq(D_MODEL), _full_spec(norm_g), _full_spec(w), cblk, hblk] + [_full_spec(a) for a in prm],
        out_specs=[seq(C_LRU), hblk, cblk],
        out_shape=[jax.ShapeDtypeStruct((B, T, C_LRU), F32), jax.ShapeDtypeStruct((B, 1, C_LRU), F32),
                   jax.ShapeDtypeStruct((B, CONV_WIDTH - 1, C_LRU), F32)],
        scratch_shapes=[pltpu.VMEM((tm + 8, C_LRU), F32), pltpu.VMEM((1, C_LRU), F32),
                        pltpu.VMEM((tm, C_LRU), F32), pltpu.VMEM((tm, C_LRU), F32)],
        compiler_params=_cparams("parallel", "arbitrary"),
        name="lru_seq",
    )(x, norm_g, w, conv0, h0.reshape(B, 1, C_LRU), *prm)
    return o.reshape(B * T, C_LRU), hT.reshape(B, C_LRU), cT


def _block_diag(w):
    out = jnp.zeros((C_LRU, C_LRU), w.dtype)
    n = C_LRU // LRU_BLOCKS
    for i in range(LRU_BLOCKS):
        out = out.at[i * n:(i + 1) * n, i * n:(i + 1) * n].set(w[i])
    return out


def _prepare_params(p):
    row = lambda a: a.reshape(1, -1)
    zpad = jnp.zeros((W_LORA, C_HEADS), F32)
    layers = []
    for l in range(DEPTH):
        pre = (row(p['mu_shift'][l]), row(p['rwkv_w0'][l]),
               jnp.concatenate([p['rwkv_w_up'][l], zpad], 0).astype(BF16),
               row(p['rwkv_a0'][l]),
               jnp.concatenate([zpad, p['rwkv_a_up'][l]], 0).astype(BF16),
               p['rwkv_g_up'][l].astype(BF16), row(p['rwkv_k_k'][l]), row(p['rwkv_k_a'][l]),
               None,
               p['lru_conv_w'][l], row(p['lru_conv_b'][l]),
               _block_diag(p['lru_wa'][l]).astype(BF16), row(p['lru_ba'][l]),
               _block_diag(p['lru_wx'][l]).astype(BF16), row(p['lru_bx'][l]), row(p['lru_lambda'][l]))
        w_in = p['w_in'][l]
        layers.append(dict(
            norm1_g=row(p['norm1_g'][l]), pre=pre,
            w_r=w_in[:, :RWKV_COLS].astype(BF16),
            w_h=w_in[:, RWKV_COLS:RWKV_COLS + HGRN_COLS].astype(BF16),
            w_l=w_in[:, RWKV_COLS + HGRN_COLS:].astype(BF16),
            r_k=row(p['rwkv_r_k'][l]), ln_w=row(p['rwkv_ln_w'][l]), ln_b=row(p['rwkv_ln_b'][l]),
            hgrn_norm_g=row(p['hgrn_norm_g'][l]),
            w_out=p['w_out'][l].astype(BF16), norm2_g=row(p['norm2_g'][l]),
            mlp_w1=p['mlp_w1'][l].astype(BF16), mlp_w2=p['mlp_w2'][l].astype(BF16)))
    return layers


def _trunk(x, wkv, shift, hgrn, lru, conv, layers, lb_all, final_g):
    B, T, _ = x.shape
    m = B * T
    assert T == 1 or (T % CHUNK == 0 and B % SEQS == 0), "sequence kernels tile (B, T) by (SEQS, CHUNK)"
    x = x.reshape(m, D_MODEL)
    n_wkv, n_shift, n_hgrn, n_lru, n_conv = [], [], [], [], []
    if T == 1:
        wkv_t, hgrn_t = jnp.transpose(wkv, (0, 2, 3, 4, 1)), jnp.transpose(hgrn, (0, 2, 3, 4, 1))
        s_wkv_t, s_hgrn_t = jnp.zeros(wkv_t.shape, F32), jnp.zeros(hgrn_t.shape, F32)
    for l, lp in enumerate(layers):
        n_shift.append(_rmsnorm_rows(x.reshape(B, T, D_MODEL)[:, -1], lp['norm1_g']))
        prm = list(lp['pre'])
        prm[8] = lb_all[l:l + 1]
        if T == 1:
            pr, ph, plru, p0 = _norm_proj(x, shift[l], lp['norm1_g'], lp['w_r'], lp['w_h'], lp['w_l'])
            (r, k, v, g, rT, wT, kT, vT, kkT, kaT, qT, fT, kinT, ivT, h_l) = _step_pre(
                pr, ph, plru, p0, conv[l], lru[l], prm)
            woT, s_wkv_t = _state_step(_rwkv_step_kernel, "rwkv_step", (rT, wT, kT, vT, kkT, kaT),
                                       wkv_t, l, s_wkv_t)
            hoT, s_hgrn_t = _state_step(_hgrn_step_kernel, "hgrn_step", (qT, fT, kinT, ivT),
                                        hgrn_t, l, s_hgrn_t)
            o_r, o_h, o_l = _step_post(woT, r, k, v, g, lp['r_k'], lp['ln_w'], lp['ln_b'],
                                       hoT, ph, lp['hgrn_norm_g'], h_l, plru)
            c_l = jnp.concatenate([conv[l][:, 1:], plru[:, None, :C_LRU]], axis=1)
        else:
            x3 = x.reshape(B, T, D_MODEL)
            o_r, S_r = _rwkv_seq(x3, shift[l], lp['norm1_g'], lp['w_r'], wkv[l],
                                 prm[:8] + [lp['r_k'], lp['ln_w'], lp['ln_b']], B, T)
            o_h, S_h = _hgrn_seq(x3, lp['norm1_g'], lp['w_h'], hgrn[l], prm[8], lp['hgrn_norm_g'], B, T)
            o_l, h_l, c_l = _lru_seq(x3, lp['norm1_g'], lp['w_l'], conv[l], lru[l], prm[9:], B, T)
            n_wkv.append(S_r)
            n_hgrn.append(S_h)
        x = _out_mlp(x, o_r, o_h, o_l, lp['w_out'], lp['norm2_g'], lp['mlp_w1'], lp['mlp_w2'],
                     final_g, norm_out=(l == len(layers) - 1))
        n_lru.append(h_l)
        n_conv.append(c_l)
    y = x.reshape(B, T, D_MODEL)
    if T == 1:
        s_wkv, s_hgrn = jnp.transpose(s_wkv_t, (0, 4, 1, 2, 3)), jnp.transpose(s_hgrn_t, (0, 4, 1, 2, 3))
    else:
        s_wkv, s_hgrn = jnp.stack(n_wkv), jnp.stack(n_hgrn)
    return (y, s_wkv, jnp.stack(n_shift), s_hgrn, jnp.stack(n_lru), jnp.stack(n_conv))


def kernel(x_prompt, x_sample, state_wkv, state_shift, state_hgrn, state_lru, state_conv, norm1_g, w_in, mu_shift, rwkv_w0, rwkv_w_up, rwkv_a0, rwkv_a_up, rwkv_g_up, rwkv_k_k, rwkv_k_a, rwkv_r_k, rwkv_ln_w, rwkv_ln_b, hgrn_lb, hgrn_norm_g, lru_conv_w, lru_conv_b, lru_wa, lru_ba, lru_wx, lru_bx, lru_lambda, w_out, norm2_g, mlp_w1, mlp_w2, final_g):
    prm = dict(norm1_g=norm1_g, w_in=w_in, mu_shift=mu_shift, rwkv_w0=rwkv_w0, rwkv_w_up=rwkv_w_up,
               rwkv_a0=rwkv_a0, rwkv_a_up=rwkv_a_up, rwkv_g_up=rwkv_g_up, rwkv_k_k=rwkv_k_k,
               rwkv_k_a=rwkv_k_a, rwkv_r_k=rwkv_r_k.reshape(DEPTH, C_HEADS), rwkv_ln_w=rwkv_ln_w,
               rwkv_ln_b=rwkv_ln_b, hgrn_norm_g=hgrn_norm_g, lru_conv_w=lru_conv_w,
               lru_conv_b=lru_conv_b, lru_wa=lru_wa, lru_ba=lru_ba, lru_wx=lru_wx, lru_bx=lru_bx,
               lru_lambda=lru_lambda, w_out=w_out, norm2_g=norm2_g, mlp_w1=mlp_w1, mlp_w2=mlp_w2)
    layers = _prepare_params(prm)
    lb_all = _hgrn_lower_bounds(hgrn_lb)
    fg = final_g.reshape(1, D_MODEL)
    Bp = x_prompt.shape[0]
    dt = x_prompt.dtype
    z_wkv = [jnp.zeros((Bp, N_HEADS, HEAD_DIM, HEAD_DIM), dt)] * DEPTH
    z_shift = [jnp.zeros((Bp, D_MODEL), dt)] * DEPTH
    z_hgrn = [jnp.zeros((Bp, N_HEADS, HEAD_DIM, HEAD_DIM), dt)] * DEPTH
    z_lru = [jnp.zeros((Bp, C_LRU), dt)] * DEPTH
    z_conv = [jnp.zeros((Bp, CONV_WIDTH - 1, C_LRU), dt)] * DEPTH
    y_p, p_wkv, p_shift, p_hgrn, p_lru, p_conv = _trunk(x_prompt, z_wkv, z_shift, z_hgrn, z_lru, z_conv,
                                                         layers, lb_all, fg)
    y_s, s_wkv, s_shift, s_hgrn, s_lru, s_conv = _trunk(x_sample, state_wkv, state_shift, state_hgrn,
                                                         state_lru, state_conv, layers, lb_all, fg)
    return (y_p, y_s, p_wkv, p_shift, p_hgrn, p_lru, p_conv, s_wkv, s_shift, s_hgrn, s_lru, s_conv)
```
